```python
import jax
import jax.numpy as jnp
from jax import lax
import numpy as np

D_MODEL = 1024
BATCH = 16
SEQ = 256
DEPTH = 4
DEC_BATCH = 2
DEC_SEQ = 2048
PAST_LEN = 256

GRID_W = 64
CHUNK = 64
N_EVEN = (DEPTH + 1) // 2
N_ODD = DEPTH // 2
H_M = 4
DK_M = 128
DV_M = 128
H_R = 4
DK_R = 128
DV_R = 128
H_G = 4
DK_G = 128
DV_G = 256
GLA_RANK = 16
GLA_TAU = 16.0
D_FF = 2816
EPS = 1e-6

MIX_EVEN = H_M * DV_M + H_R * DV_R
MIX_ODD = H_G * DV_G
EVEN_SIZES = (H_M * DK_M, H_M * DK_M, H_M * DV_M, H_M * DV_M, 4 * H_M,
              H_R * DK_R, H_R * DK_R, H_R * DV_R, H_R * DV_R)
ODD_SIZES = (H_G * DK_G, H_G * DK_G, H_G * DV_G, H_G * DV_G, 2 * GLA_RANK)
P_EVEN = sum(EVEN_SIZES)
P_ODD = sum(ODD_SIZES)

kernel_name = 'bidir_mlstm_retention_gla_diffusion_step'


def _split(x, sizes):
    out, off = [], 0
    for s in sizes:
        out.append(x[..., off:off + s])
        off += s
    return out


def rmsnorm(x, g):
    xf = x.astype(jnp.float32)
    y = xf * lax.rsqrt(jnp.mean(xf * xf, axis=-1, keepdims=True) + EPS)
    return (y * g.astype(jnp.float32)).astype(x.dtype)


def headnorm(h):
    return h * lax.rsqrt(jnp.mean(h * h, axis=-1, keepdims=True) + EPS)


def to_heads(t, n_heads):
    b, t_len, _ = t.shape
    return t.reshape(b, t_len, n_heads, -1).transpose(0, 2, 1, 3).astype(jnp.float32)


def from_heads(h):
    b, nh, t_len, d = h.shape
    return h.transpose(0, 2, 1, 3).reshape(b, t_len, nh * d)


def chunked(t):
    return t.reshape(t.shape[:2] + (t.shape[2] // CHUNK, CHUNK) + t.shape[3:])


def unchunk(t):
    return t.reshape(t.shape[:2] + (-1, t.shape[-1]))


def flip_t(t):
    return jnp.flip(t, axis=2)


def linear_state_scan(decay, update, s0):
    def step(s, du):
        d, u = du
        return d * s + u, s
    final, starts = lax.scan(step, s0, (decay, update))
    return starts, final


def retention_dir(q, k, v, log_gamma, s0):
    qc, kc, vc = chunked(q), chunked(k), chunked(v)
    nc = qc.shape[2]
    pos = jnp.arange(CHUNK, dtype=jnp.float32)
    lg = log_gamma.astype(jnp.float32)[:, None]
    diff = pos[:, None] - pos[None, :]
    dmat = jnp.where(diff >= 0, jnp.exp(lg[:, :, None] * jnp.maximum(diff, 0.0)), 0.0)
    scores = jnp.einsum('bhcid,bhcjd->bhcij', qc, kc) * dmat[None, :, None]
    intra = jnp.einsum('bhcij,bhcjv->bhciv', scores, vc)
    w_upd = jnp.exp(lg * (CHUNK - 1.0 - pos))
    upd = jnp.einsum('bhcjd,hj,bhcjv->cbhdv', kc, w_upd, vc)
    decay = jnp.broadcast_to(jnp.exp(lg * CHUNK)[None, None, :, :, None], (nc, 1, q.shape[1], 1, 1))
    starts, final = linear_state_scan(decay, upd, s0.astype(jnp.float32))
    q_in = qc * jnp.exp(lg * (pos + 1.0))[None, :, None, :, None]
    cross = jnp.einsum('bhcid,cbhdv->bhciv', q_in, starts)
    return unchunk(intra + cross), final


def gla_dir(q, k, v, log_alpha, s0):
    qc, kc, vc, lac = chunked(q), chunked(k), chunked(v), chunked(log_alpha)
    b = jnp.cumsum(lac, axis=3)
    b_last = b[:, :, :, -1:, :]
    qe = qc * jnp.exp(b)
    ke = kc * jnp.exp(-b)
    lower = jnp.tril(jnp.ones((CHUNK, CHUNK), dtype=bool))
    att = jnp.where(lower, jnp.einsum('bhcid,bhcjd->bhcij', qe, ke), 0.0)
    intra = jnp.einsum('bhcij,bhcjv->bhciv', att, vc)
    upd = jnp.einsum('bhcjd,bhcjv->cbhdv', kc * jnp.exp(b_last - b), vc)
    decay = jnp.exp(b_last[:, :, :, 0, :]).transpose(2, 0, 1, 3)[..., None]
    starts, final = linear_state_scan(decay, upd, s0.astype(jnp.float32))
    cross = jnp.einsum('bhcid,cbhdv->bhciv', qe, starts)
    return unchunk(intra + cross), final


def mlstm_dir(q, k, v, i_pre, f_pre, state0):
    c0, n0, m0 = (s.astype(jnp.float32) for s in state0)
    qc, kc, vc = chunked(q), chunked(k), chunked(v)
    ic = chunked(i_pre)
    fcum = jnp.cumsum(chunked(jax.nn.log_sigmoid(f_pre)), axis=3)
    lower = jnp.tril(jnp.ones((CHUNK, CHUNK), dtype=bool))
    dlog = jnp.where(lower, fcum[..., :, None] - fcum[..., None, :] + ic[..., None, :], -jnp.inf)
    mloc = jnp.max(dlog, axis=-1)
    w_last = jnp.exp(dlog[..., -1, :] - mloc[..., -1:])
    upd_c = jnp.einsum('bhcj,bhcjd,bhcjv->cbhdv', w_last, kc, vc)
    upd_n = jnp.einsum('bhcj,bhcjd->cbhd', w_last, kc)
    f_last = fcum[..., -1].transpose(2, 0, 1)
    m_last = mloc[..., -1].transpose(2, 0, 1)

    def step(carry, xs):
        cm, nm, mm = carry
        fl, ml, uc, un = xs
        m_new = jnp.maximum(fl + mm, ml)
        a = jnp.exp(fl + mm - m_new)
        bb = jnp.exp(ml - m_new)
        c_new = a[..., None, None] * cm + bb[..., None, None] * uc
        n_new = a[..., None] * nm + bb[..., None] * un
        return (c_new, n_new, m_new), (cm, nm, mm)

    final, (c_st, n_st, m_st) = lax.scan(step, (c0, n0, m0), (f_last, m_last, upd_c, upd_n))
    a_i = fcum + m_st.transpose(1, 2, 0)[..., None]
    m_i = jnp.maximum(a_i, mloc)
    w_inter = jnp.exp(a_i - m_i)
    pw = jnp.exp(dlog - m_i[..., None]) * jnp.einsum('bhcid,bhcjd->bhcij', qc, kc)
    num = (jnp.einsum('bhcij,bhcjv->bhciv', pw, vc)
           + w_inter[..., None] * jnp.einsum('bhcid,cbhdv->bhciv', qc, c_st))
    den = pw.sum(-1) + w_inter * jnp.einsum('bhcid,cbhd->bhci', qc, n_st)
    h = num / jnp.maximum(jnp.abs(den), jnp.exp(-m_i))[..., None]
    return unchunk(h), final


def even_mixer(xn, w_in, w_out, ig_b, fg_b, ret_logit, init):
    b, t_len, _ = xn.shape
    mq, mk, mv, mo, mg, rq, rk, rv, rg = _split(xn @ w_in, EVEN_SIZES)
    m_c0, m_n0, m_m0, r_s0 = init
    mq = to_heads(mq, H_M)
    mk = to_heads(mk, H_M) * DK_M ** -0.5
    mv = to_heads(mv, H_M)
    gates = mg.astype(jnp.float32).reshape(b, t_len, 2, 2, H_M).transpose(2, 3, 0, 4, 1)
    i_pre = gates[:, 0] + ig_b.astype(jnp.float32)[:, None, :, None]
    f_pre = gates[:, 1] + fg_b.astype(jnp.float32)[:, None, :, None]
    hf, (cf, nf, mf) = mlstm_dir(mq, mk, mv, i_pre[0], f_pre[0], (m_c0[:, 0], m_n0[:, 0], m_m0[:, 0]))
    hb, (cb, nb, mb) = mlstm_dir(flip_t(mq), flip_t(mk), flip_t(mv), flip_t(i_pre[1]), flip_t(f_pre[1]),
                                 (m_c0[:, 1], m_n0[:, 1], m_m0[:, 1]))
    h_m = from_heads(headnorm(hf + flip_t(hb))) * jax.nn.sigmoid(mo.astype(jnp.float32))
    rq = to_heads(rq, H_R)
    rk = to_heads(rk, H_R) * DK_R ** -0.5
    rv = to_heads(rv, H_R)
    log_gamma = jax.nn.log_sigmoid(ret_logit.astype(jnp.float32))
    of, sf = retention_dir(rq, rk, rv, log_gamma[0], r_s0[:, 0])
    ob, sb = retention_dir(flip_t(rq), flip_t(rk), flip_t(rv), log_gamma[1], r_s0[:, 1])
    h_r = from_heads(headnorm(of + flip_t(ob))) * jax.nn.silu(rg.astype(jnp.float32))
    y = jnp.concatenate([h_m, h_r], axis=-1).astype(xn.dtype) @ w_out
    new = (jnp.stack([cf, cb], axis=1), jnp.stack([nf, nb], axis=1),
           jnp.stack([mf, mb], axis=1), jnp.stack([sf, sb], axis=1))
    return y, new


def odd_mixer(xn, w_in, w2, b2, w_out, init):
    b, t_len, _ = xn.shape
    gq, gk, gv, gr, glr = _split(xn @ w_in, ODD_SIZES)
    gq = to_heads(gq, H_G) * DK_G ** -0.5
    gk = to_heads(gk, H_G)
    gv = to_heads(gv, H_G)
    glr = glr.reshape(b, t_len, 2, GLA_RANK)
    z = jnp.einsum('btdr,drk->dbtk', glr, w2) + b2[:, None, None, :]
    log_alpha = jax.nn.log_sigmoid(z.astype(jnp.float32)) / GLA_TAU
    la_f = to_heads(log_alpha[0], H_G)
    la_b = to_heads(log_alpha[1], H_G)
    of, sf = gla_dir(gq, gk, gv, la_f, init[:, 0])
    ob, sb = gla_dir(flip_t(gq), flip_t(gk), flip_t(gv), flip_t(la_b), init[:, 1])
    h = from_heads(headnorm(of + flip_t(ob))) * jax.nn.silu(gr.astype(jnp.float32))
    return h.astype(xn.dtype) @ w_out, jnp.stack([sf, sb], axis=1)


def dwconv_seq(a, w):
    f = a.shape[-1]
    return lax.conv_general_dilated(a, w[:, None, :].astype(a.dtype), (1,), 'SAME',
                                    dimension_numbers=('NWC', 'WIO', 'NWC'), feature_group_count=f)


def dwconv_grid(a, w, rows):
    b, t_len, f = a.shape
    a4 = a.reshape(b, rows, t_len // rows, f)
    out = lax.conv_general_dilated(a4, w[:, :, None, :].astype(a.dtype), (1, 1), 'SAME',
                                   dimension_numbers=('NHWC', 'HWIO', 'NHWC'), feature_group_count=f)
    return out.reshape(b, t_len, f)


def conv_ffn(xn, w_in, conv_w, conv_b, w_out, rows):
    a, v = _split(xn @ w_in, (D_FF, D_FF))
    if rows is None:
        a = dwconv_seq(a, conv_w[1])
    else:
        a = dwconv_grid(a, conv_w, rows)
    h = jax.nn.gelu(a + conv_b) * v
    return h @ w_out


def trunk_layer(x, mod, l, p, init, rows):
    shift1, scale1, gate1, shift2, scale2, gate2 = jnp.split(mod[:, None, :], 6, axis=-1)
    xn = rmsnorm(x, p['norm_mix_pre'][l]) * (1.0 + scale1) + shift1
    if l % 2 == 0:
        e = l // 2
        y, new = even_mixer(xn, p['w_in_even'][e], p['w_out_even'][e], p['mlstm_igate_b'][e],
                            p['mlstm_fgate_b'][e], p['ret_decay_logit'][e], init)
    else:
        o = l // 2
        y, new = odd_mixer(xn, p['w_in_odd'][o], p['gla_gate_w2'][o], p['gla_gate_b'][o],
                           p['w_out_odd'][o], init)
    x = x + gate1 * rmsnorm(y, p['norm_mix_post'][l])
    xn = rmsnorm(x, p['norm_ffn_pre'][l]) * (1.0 + scale2) + shift2
    y = conv_ffn(xn, p['ffn_w_in'][l], p['ffn_conv_w'][l], p['ffn_conv_b'][l], p['ffn_w_out'][l], rows)
    x = x + gate2 * rmsnorm(y, p['norm_ffn_post'][l])
    return x, new


def setup_inputs(seed: int = 0) -> dict:
    key = jax.random.key(seed)
    ks = jax.random.split(key, 32)
    f32 = jnp.float32

    def nrm(k, shape, scale):
        return jax.random.normal(k, shape, f32) * scale

    inp = {}
    inp['x_prompt'] = nrm(ks[0], (BATCH, SEQ, D_MODEL), 1.0)
    inp['x_sample'] = nrm(ks[1], (DEC_BATCH, DEC_SEQ, D_MODEL), 1.0)
    inp['c'] = nrm(ks[2], (DEC_BATCH, D_MODEL), 1.0)
    inp['c_ctx'] = nrm(ks[3], (D_MODEL,), 1.0)
    inp['state_mlstm_C'] = nrm(ks[4], (DEC_BATCH, N_EVEN, 2, H_M, DK_M, DV_M), 0.1)
    inp['state_mlstm_n'] = nrm(ks[5], (DEC_BATCH, N_EVEN, 2, H_M, DK_M), 0.1)
    inp['state_mlstm_m'] = nrm(ks[6], (DEC_BATCH, N_EVEN, 2, H_M), 1.0)
    inp['state_ret'] = nrm(ks[7], (DEC_BATCH, N_EVEN, 2, H_R, DK_R, DV_R), 0.3)
    inp['state_gla'] = nrm(ks[8], (DEC_BATCH, N_ODD, 2, H_G, DK_G, DV_G), 0.3)
    inp['ada_w'] = nrm(ks[9], (DEPTH, D_MODEL, 6 * D_MODEL), 0.5 * D_MODEL ** -0.5)
    inp['ada_b'] = nrm(ks[10], (DEPTH, 6 * D_MODEL), 0.02)
    inp['norm_mix_pre'] = 1.0 + nrm(ks[11], (DEPTH, D_MODEL), 0.02)
    inp['norm_mix_post'] = 1.0 + nrm(ks[12], (DEPTH, D_MODEL), 0.02)
    inp['norm_ffn_pre'] = 1.0 + nrm(ks[13], (DEPTH, D_MODEL), 0.02)
    inp['norm_ffn_post'] = 1.0 + nrm(ks[14], (DEPTH, D_MODEL), 0.02)
    inp['w_in_even'] = nrm(ks[15], (N_EVEN, D_MODEL, P_EVEN), D_MODEL ** -0.5)
    inp['w_out_even'] = nrm(ks[16], (N_EVEN, MIX_EVEN, D_MODEL), MIX_EVEN ** -0.5)
    inp['mlstm_igate_b'] = nrm(ks[17], (N_EVEN, 2, H_M), 0.1)
    inp['mlstm_fgate_b'] = jnp.linspace(3.0, 6.0, H_M, dtype=f32) + nrm(ks[18], (N_EVEN, 2, H_M), 0.1)
    base_logit = jnp.log(2.0 ** (5.0 + jnp.arange(H_R, dtype=f32)) - 1.0)
    inp['ret_decay_logit'] = base_logit + nrm(ks[19], (N_EVEN, 2, H_R), 0.1)
    inp['w_in_odd'] = nrm(ks[20], (N_ODD, D_MODEL, P_ODD), D_MODEL ** -0.5)
    inp['gla_gate_w2'] = nrm(ks[21], (N_ODD, 2, GLA_RANK, H_G * DK_G), GLA_RANK ** -0.5)
    inp['gla_gate_b'] = nrm(ks[22], (N_ODD, 2, H_G * DK_G), 0.1)
    inp['w_out_odd'] = nrm(ks[23], (N_ODD, MIX_ODD, D_MODEL), MIX_ODD ** -0.5)
    inp['ffn_w_in'] = nrm(ks[24], (DEPTH, D_MODEL, 2 * D_FF), D_MODEL ** -0.5)
    inp['ffn_conv_w'] = nrm(ks[25], (DEPTH, 3, 3, D_FF), 1.0 / 3.0)
    inp['ffn_conv_b'] = nrm(ks[26], (DEPTH, D_FF), 0.02)
    inp['ffn_w_out'] = nrm(ks[27], (DEPTH, D_FF, D_MODEL), D_FF ** -0.5)
    return inp


def reference(x_prompt, x_sample, c, c_ctx, state_mlstm_C, state_mlstm_n, state_mlstm_m, state_ret,
              state_gla, ada_w, ada_b, norm_mix_pre, norm_mix_post, norm_ffn_pre, norm_ffn_post,
              w_in_even, w_out_even, mlstm_igate_b, mlstm_fgate_b, ret_decay_logit,
              w_in_odd, gla_gate_w2, gla_gate_b, w_out_odd,
              ffn_w_in, ffn_conv_w, ffn_conv_b, ffn_w_out):
    p = dict(norm_mix_pre=norm_mix_pre, norm_mix_post=norm_mix_post, norm_ffn_pre=norm_ffn_pre,
             norm_ffn_post=norm_ffn_post, w_in_even=w_in_even, w_out_even=w_out_even,
             mlstm_igate_b=mlstm_igate_b, mlstm_fgate_b=mlstm_fgate_b, ret_decay_logit=ret_decay_logit,
             w_in_odd=w_in_odd, gla_gate_w2=gla_gate_w2, gla_gate_b=gla_gate_b, w_out_odd=w_out_odd,
             ffn_w_in=ffn_w_in, ffn_conv_w=ffn_conv_w, ffn_conv_b=ffn_conv_b, ffn_w_out=ffn_w_out)
    f32 = jnp.float32

    bp = x_prompt.shape[0]
    cond_ctx = jax.nn.silu(c_ctx)[None, :]
    h = x_prompt
    new_c, new_n, new_m, new_r, new_g = [], [], [], [], []
    for l in range(DEPTH):
        mod = cond_ctx @ ada_w[l] + ada_b[l]
        if l % 2 == 0:
            init = (jnp.zeros((bp, 2, H_M, DK_M, DV_M), f32), jnp.zeros((bp, 2, H_M, DK_M), f32),
                    jnp.zeros((bp, 2, H_M), f32), jnp.zeros((bp, 2, H_R, DK_R, DV_R), f32))
            h, (sc, sn, sm, sr) = trunk_layer(h, mod, l, p, init, None)
            new_c.append(sc)
            new_n.append(sn)
            new_m.append(sm)
            new_r.append(sr)
        else:
            init = jnp.zeros((bp, 2, H_G, DK_G, DV_G), f32)
            h, sg = trunk_layer(h, mod, l, p, init, None)
            new_g.append(sg)
    y_prompt = h

    rows = x_sample.shape[1] // GRID_W
    cond = jax.nn.silu(c)
    g = x_sample
    for l in range(DEPTH):
        mod = cond @ ada_w[l] + ada_b[l]
        if l % 2 == 0:
            e = l // 2
            init = (state_mlstm_C[:, e], state_mlstm_n[:, e], state_mlstm_m[:, e], state_ret[:, e])
        else:
            init = state_gla[:, l // 2]
        g, _ = trunk_layer(g, mod, l, p, init, rows)
    y_sample = g

    return (y_prompt, y_sample, jnp.stack(new_c, axis=1), jnp.stack(new_n, axis=1),
            jnp.stack(new_m, axis=1), jnp.stack(new_r, axis=1), jnp.stack(new_g, axis=1))
```

```python
import functools

import jax
import jax.numpy as jnp
from jax import lax
from jax.experimental import pallas as pl
from jax.experimental.pallas import tpu as pltpu

F32 = jnp.float32
BF16 = jnp.bfloat16

D = 1024
DEPTH = 4
N_CTX_SEQ, CTX_LEN = 16, 256
N_LAT_SEQ, LAT_LEN = 2, 2048
GRID_W = 64
N_CTX_TOK = N_CTX_SEQ * CTX_LEN
N_TOK = N_CTX_TOK + N_LAT_SEQ * LAT_LEN
HEADS = 4
DK = 128
DV_GLA = 256
GLA_RANK = 16
GLA_TAU = 16.0
D_FF = 2816
EPS = 1e-6
NEG = -1e30

ROW_TILE = 256
MM_ROW_TILE = 512
EVEN_CHUNK = 256
GLA_CHUNK = 64
FF_CHUNK = 256
VMEM_LIMIT = 56 * 1024 * 1024


def _cparams(*sem):
    return pltpu.CompilerParams(dimension_semantics=sem, vmem_limit_bytes=VMEM_LIMIT)


def _mod_row(i, tile):
    n_ctx = N_CTX_TOK // tile
    return jnp.where(i < n_ctx, 0, 1 + (i - n_ctx) // (LAT_LEN // tile))


def _rms(x):
    return x * lax.rsqrt(jnp.mean(x * x, axis=-1, keepdims=True) + EPS)


def _log_sigmoid(x):
    return jnp.minimum(x, 0.0) - jnp.log1p(jnp.exp(-jnp.abs(x)))


def _dot(a, b):
    return jnp.dot(a, b, preferred_element_type=F32)


def _dot_nt(a, b):
    return lax.dot_general(a, b, (((1,), (1,)), ((), ())), preferred_element_type=F32)


def _dot_tn(a, b):
    return lax.dot_general(a, b, (((0,), (0,)), ((), ())), preferred_element_type=F32)


def _seg_scan(x, seg, pos, op, ident, reverse):
    n = x.shape[0]
    s = 1
    while s < seg:
        if reverse:
            x = op(x, jnp.where(pos < seg - s, pltpu.roll(x, n - s, 0), ident))
        else:
            x = op(x, jnp.where(pos >= s, pltpu.roll(x, s, 0), ident))
        s *= 2
    return x


def _mod_kernel(c_ref, w_ref, b_ref, o_ref):
    c = c_ref[...]
    s = (c * jax.nn.sigmoid(c)).astype(BF16)
    o_ref[0] = _dot(s, w_ref[0].astype(BF16)) + b_ref[0]


def _mod_table(cond, ada_w, ada_b):
    tn = 1024
    out = pl.pallas_call(
        _mod_kernel,
        grid=(DEPTH, 6 * D // tn),
        in_specs=[pl.BlockSpec((8, D), lambda l, j: (0, 0)),
                  pl.BlockSpec((1, D, tn), lambda l, j: (l, 0, j)),
                  pl.BlockSpec((1, 1, tn), lambda l, j: (l, 0, j))],
        out_specs=pl.BlockSpec((1, 8, tn), lambda l, j: (l, 0, j)),
        out_shape=jax.ShapeDtypeStruct((DEPTH, 8, 6 * D), F32),
        compiler_params=_cparams("parallel", "parallel"),
        name="mod_table",
    )(cond, ada_w, ada_b.reshape(DEPTH, 1, 6 * D))
    return out.reshape(DEPTH, 8, 6, D)


def _prenorm_kernel(xp_ref, xs_ref, g_ref, mod_ref, x_ref, xn_ref):
    i = pl.program_id(0)

    def run(src_ref):
        x = src_ref[...]
        m = mod_ref[...]
        x_ref[...] = x
        xn_ref[...] = (_rms(x) * g_ref[...] * (1.0 + m[1:2]) + m[0:1]).astype(BF16)

    pl.when(i < N_CTX_TOK // ROW_TILE)(lambda: run(xp_ref))
    pl.when(i >= N_CTX_TOK // ROW_TILE)(lambda: run(xs_ref))


def _prenorm(xp, xs, g, mods, layer):
    nc = N_CTX_TOK // ROW_TILE
    return pl.pallas_call(
        _prenorm_kernel,
        grid=(N_TOK // ROW_TILE,),
        in_specs=[pl.BlockSpec((ROW_TILE, D), lambda i: (jnp.minimum(i, nc - 1), 0)),
                  pl.BlockSpec((ROW_TILE, D), lambda i: (jnp.maximum(i - nc, 0), 0)),
                  pl.BlockSpec((None, 1, D), lambda i: (layer, 0, 0)),
                  pl.BlockSpec((None, None, 6, D), lambda i: (layer, _mod_row(i, ROW_TILE), 0, 0))],
        out_specs=[pl.BlockSpec((ROW_TILE, D), lambda i: (i, 0)),
                   pl.BlockSpec((ROW_TILE, D), lambda i: (i, 0))],
        out_shape=[jax.ShapeDtypeStruct((N_TOK, D), F32), jax.ShapeDtypeStruct((N_TOK, D), BF16)],
        compiler_params=_cparams("parallel"),
        name="prenorm",
    )(xp, xs, g, mods)


def _mm_kernel(x_ref, w_ref, o_ref):
    o_ref[...] = _dot(x_ref[...], w_ref[...]).astype(o_ref.dtype)


def _matmul(x, w, tm, tn, name):
    m, k = x.shape
    n = w.shape[1]
    return pl.pallas_call(
        _mm_kernel,
        grid=(n // tn, m // tm),
        in_specs=[pl.BlockSpec((tm, k), lambda j, i: (i, 0)),
                  pl.BlockSpec((k, tn), lambda j, i: (0, j))],
        out_specs=pl.BlockSpec((tm, tn), lambda j, i: (i, j)),
        out_shape=jax.ShapeDtypeStruct((m, n), F32),
        compiler_params=_cparams("parallel", "parallel"),
        name=name,
    )(x, w)


def _even_kernel(seq_len, has_init, want_final, *refs):
    chunk = EVEN_CHUNK
    n_chunks = seq_len // chunk
    it = iter(refs)
    par_ref = next(it)
    m0_ref = next(it) if has_init else None
    mq_ref, mk_ref, mv_ref, mo_ref, rq_ref, rk_ref, rv_ref, rg_ref, gate_ref = (next(it) for _ in range(9))
    if has_init:
        c0_ref, n0_ref, r0_ref = next(it), next(it), next(it)
    out_ref = next(it)
    if want_final:
        cn_out_ref, m_out_ref, r_out_ref = next(it), next(it), next(it)
    fs_ref, gs_ref, gm_ref = next(it), next(it), next(it)
    if n_chunks > 1:
        cnb_ref, mb_ref, rb_ref = next(it), next(it), next(it)

    b = pl.program_id(0)
    h = pl.program_id(1)
    lane = lax.broadcasted_iota(jnp.int32, (1, 128), 1)
    ii = lax.broadcasted_iota(jnp.int32, (chunk, chunk), 0)
    jj = lax.broadcasted_iota(jnp.int32, (chunk, chunk), 1)
    pos_col = lax.broadcasted_iota(jnp.int32, (chunk, 1), 0).astype(F32)
    ones_col = (lax.broadcasted_iota(jnp.int32, (chunk, 128), 1) == 0).astype(BF16)

    g_all = pltpu.roll(gate_ref[...], 4 * h, 1)
    bias_i = jnp.where(lane == 0, par_ref[h], par_ref[4 + h])
    bias_f = jnp.where(lane == 0, par_ref[8 + h], par_ref[12 + h])
    i_pre = g_all + bias_i
    log_f = _log_sigmoid(pltpu.roll(g_all, 126, 1) + bias_f)
    pos = lax.broadcasted_iota(jnp.int32, (seq_len, 1), 0) % chunk
    fcum = jnp.where(lane == 0, _seg_scan(log_f, chunk, pos, jnp.add, 0.0, False),
                     _seg_scan(log_f, chunk, pos, jnp.add, 0.0, True))
    gsc = i_pre - fcum
    gmax = jnp.where(lane == 0, _seg_scan(gsc, chunk, pos, jnp.maximum, NEG, False),
                     _seg_scan(gsc, chunk, pos, jnp.maximum, NEG, True))
    fs_ref[...] = fcum
    gs_ref[...] = gsc
    gm_ref[...] = gmax

    lg_f = _log_sigmoid(jnp.full((1, 1), par_ref[16 + h], F32))
    lg_b = _log_sigmoid(jnp.full((1, 1), par_ref[20 + h], F32))
    diff = (ii - jj).astype(F32)
    ret_mask = (jnp.where(diff >= 0, jnp.exp(lg_f * jnp.maximum(diff, 0.0)), 0.0)
                + jnp.where(diff <= 0, jnp.exp(lg_b * jnp.maximum(-diff, 0.0)), 0.0))
    rq_dec_f = jnp.exp(lg_f * (pos_col + 1.0))
    rq_dec_b = jnp.exp(lg_b * (chunk - pos_col))
    rk_dec_f = jnp.exp(lg_f * (chunk - 1.0 - pos_col))
    rk_dec_b = jnp.exp(lg_b * pos_col)
    r_chunk_f = jnp.exp(lg_f * chunk)
    r_chunk_b = jnp.exp(lg_b * chunk)

    def rows(c):
        if isinstance(c, int):
            return pl.ds(c * chunk, chunk)
        return pl.ds(pl.multiple_of(c * chunk, chunk), chunk)

    def load_kv(c):
        k = mk_ref[rows(c), :] * (DK ** -0.5)
        v = mv_ref[rows(c), :].astype(BF16)
        return k, jnp.concatenate([v, ones_col], axis=1)

    def ret_kv(c):
        return rk_ref[rows(c), :] * (DK ** -0.5), rv_ref[rows(c), :].astype(BF16)

    def mlstm_update(c, d, cn, m):
        edge = c * chunk + (chunk - 1 if d == 0 else 0)
        f_edge = fs_ref[pl.ds(edge, 1), :][:, d:d + 1]
        g_edge = gm_ref[pl.ds(edge, 1), :][:, d:d + 1]
        mu = jnp.maximum(m, g_edge)
        k, vext = load_kv(c)
        kw = (k * jnp.exp(gs_ref[rows(c), :][:, d:d + 1] - mu)).astype(BF16)
        return jnp.exp(m - mu) * cn + _dot_tn(kw, vext), f_edge + mu

    def ret_update(c, d, s):
        k, v = ret_kv(c)
        kw = (k * (rk_dec_f if d == 0 else rk_dec_b)).astype(BF16)
        return (r_chunk_f if d == 0 else r_chunk_b) * s + _dot_tn(kw, v)

    if has_init:
        base = b * 8
        m_f0 = jnp.full((1, 1), m0_ref[base + h], F32)
        m_b0 = jnp.full((1, 1), m0_ref[base + 4 + h], F32)
        lane256 = lax.broadcasted_iota(jnp.int32, (DK, 128), 1)
        cn_f0 = jnp.concatenate([c0_ref[0], jnp.where(lane256 == 0, n0_ref[0], 0.0)], axis=1)
        cn_b0 = jnp.concatenate([c0_ref[1], jnp.where(lane256 == 0, n0_ref[1], 0.0)], axis=1)
        s_f0, s_b0 = r0_ref[0], r0_ref[1]
    else:
        m_f0 = m_b0 = jnp.zeros((1, 1), F32)
        cn_f0 = cn_b0 = jnp.zeros((DK, 256), F32)
        s_f0 = s_b0 = jnp.zeros((DK, DK), F32)
    use_cross = has_init or n_chunks > 1

    if n_chunks > 1:
        def bwd_step(t, carry):
            cn_b, m_b, s_b = carry
            c = n_chunks - 1 - t
            cnb_ref[c] = cn_b
            mb_ref[c] = jnp.broadcast_to(m_b, (8, 128))
            rb_ref[c] = s_b
            cn_b, m_b = mlstm_update(c, 1, cn_b, m_b)
            return cn_b, m_b, ret_update(c, 1, s_b)

        cn_b_fin, m_b_fin, s_b_fin = lax.fori_loop(0, n_chunks, bwd_step, (cn_b0, m_b0, s_b0))
    elif want_final:
        cn_b_fin, m_b_fin = mlstm_update(0, 1, cn_b0, m_b0)
        s_b_fin = ret_update(0, 1, s_b0)

    def fwd_step(c, carry):
        cn_f, m_f, s_f = carry
        if n_chunks > 1:
            cn_b, m_b, s_b = cnb_ref[c], mb_ref[c][0:1, 0:1], rb_ref[c]
        else:
            cn_b, m_b, s_b = cn_b0, m_b0, s_b0
        q = mq_ref[rows(c), :].astype(BF16)
        k, vext = load_kv(c)
        scores = _dot_nt(q, k.astype(BF16))
        gcol = gs_ref[rows(c), :]
        gmx = gm_ref[rows(c), :]
        fcol = fs_ref[rows(c), :]
        eye = ii == jj
        grow_f = jnp.sum(jnp.where(eye, gcol[:, 0:1], 0.0), axis=0, keepdims=True)
        grow_b = jnp.sum(jnp.where(eye, gcol[:, 1:2], 0.0), axis=0, keepdims=True)
        mu_f = jnp.maximum(m_f, gmx[:, 0:1])
        mu_b = jnp.maximum(m_b, gmx[:, 1:2])
        p_f = jnp.exp(jnp.where(jj <= ii, grow_f - mu_f, NEG)) * scores
        p_b = jnp.exp(jnp.where(jj >= ii, grow_b - mu_b, NEG)) * scores
        tot = _dot(jnp.concatenate([p_f, p_b], axis=0).astype(BF16), vext)
        tot_f, tot_b = tot[:chunk], tot[chunk:]
        if use_cross:
            cross = _dot(q, jnp.concatenate([cn_f, cn_b], axis=1).astype(BF16))
            tot_f = tot_f + jnp.exp(m_f - mu_f) * cross[:, :256]
            tot_b = tot_b + jnp.exp(m_b - mu_b) * cross[:, 256:]
        h_f = tot_f[:, :128] / jnp.maximum(jnp.abs(tot_f[:, 128:129]), jnp.exp(-(fcol[:, 0:1] + mu_f)))
        h_b = tot_b[:, :128] / jnp.maximum(jnp.abs(tot_b[:, 128:129]), jnp.exp(-(fcol[:, 1:2] + mu_b)))
        out_m = _rms(h_f + h_b) * jax.nn.sigmoid(mo_ref[rows(c), :])
        rq = rq_ref[rows(c), :]
        rk, rv = ret_kv(c)
        o_r = _dot((_dot_nt(rq.astype(BF16), rk.astype(BF16)) * ret_mask).astype(BF16), rv)
        if use_cross:
            q_in = jnp.concatenate([rq * rq_dec_f, rq * rq_dec_b], axis=1).astype(BF16)
            o_r = o_r + _dot(q_in, jnp.concatenate([s_f, s_b], axis=0).astype(BF16))
        rg = rg_ref[rows(c), :]
        out_r = _rms(o_r) * (rg * jax.nn.sigmoid(rg))
        out_ref[rows(c), :] = jnp.concatenate([out_m, out_r], axis=1).astype(BF16)
        if want_final or n_chunks > 1:
            cn_f, m_f = mlstm_update(c, 0, cn_f, m_f)
            s_f = ret_update(c, 0, s_f)
        return cn_f, m_f, s_f

    if n_chunks > 1:
        cn_f_fin, m_f_fin, s_f_fin = lax.fori_loop(0, n_chunks, fwd_step, (cn_f0, m_f0, s_f0))
    else:
        cn_f_fin, m_f_fin, s_f_fin = fwd_step(0, (cn_f0, m_f0, s_f0))

    if want_final:
        cn_out_ref[0] = cn_f_fin
        cn_out_ref[1] = cn_b_fin
        r_out_ref[0] = s_f_fin
        r_out_ref[1] = s_b_fin
        m_out_ref[...] = jnp.concatenate([jnp.broadcast_to(m_f_fin, (1, 128)), jnp.broadcast_to(m_b_fin, (1, 128)),
                                          jnp.zeros((6, 128), F32)], axis=0)


def _even_mixer(proj, gates, params, group, init):
    ctx = group == 0
    n_seq, seq_len = (N_CTX_SEQ, CTX_LEN) if ctx else (N_LAT_SEQ, LAT_LEN)
    row_off = 0 if ctx else N_CTX_TOK // LAT_LEN
    has_init, want_final = not ctx, ctx
    n_chunks = seq_len // EVEN_CHUNK

    def col(k):
        return pl.BlockSpec((seq_len, 128), lambda b, h: (row_off + b, 4 * k + h))

    smem = pl.BlockSpec(memory_space=pltpu.SMEM)
    in_specs = [smem]
    args = [params]
    if has_init:
        c0, n0, m0, r0 = init
        in_specs.append(smem)
        args.append(m0.reshape(-1))
    in_specs += [col(k) for k in range(8)]
    in_specs.append(pl.BlockSpec((seq_len, 128), lambda b, h: (row_off + b, 0)))
    args += [proj] * 8 + [gates]
    if has_init:
        in_specs += [pl.BlockSpec((None, 2, None, DK, 128), lambda b, h: (b, 0, h, 0, 0)),
                     pl.BlockSpec((None, 2, None, DK, 1), lambda b, h: (b, 0, h, 0, 0)),
                     pl.BlockSpec((None, 2, None, DK, 128), lambda b, h: (b, 0, h, 0, 0))]
        args += [c0, n0.reshape(n0.shape + (1,)), r0]
    out_specs = [pl.BlockSpec((seq_len, 256), lambda b, h: (b, h))]
    out_shape = [jax.ShapeDtypeStruct((n_seq * seq_len, D), BF16)]
    if want_final:
        out_specs += [pl.BlockSpec((None, 2, None, DK, 256), lambda b, h: (b, 0, h, 0, 0)),
                      pl.BlockSpec((None, None, 8, 128), lambda b, h: (b, h, 0, 0)),
                      pl.BlockSpec((None, 2, None, DK, 128), lambda b, h: (b, 0, h, 0, 0))]
        out_shape += [jax.ShapeDtypeStruct((n_seq, 2, HEADS, DK, 256), F32),
                      jax.ShapeDtypeStruct((n_seq, HEADS, 8, 128), F32),
                      jax.ShapeDtypeStruct((n_seq, 2, HEADS, DK, 128), F32)]
    scratch = [pltpu.VMEM((seq_len, 128), F32)] * 3
    if n_chunks > 1:
        scratch += [pltpu.VMEM((n_chunks, DK, 256), F32), pltpu.VMEM((n_chunks, 8, 128), F32),
                    pltpu.VMEM((n_chunks, DK, 128), F32)]
    return pl.pallas_call(
        functools.partial(_even_kernel, seq_len, has_init, want_final),
        grid=(n_seq, HEADS),
        in_specs=in_specs,
        out_specs=out_specs,
        out_shape=out_shape,
        scratch_shapes=scratch,
        compiler_params=_cparams("parallel", "parallel"),
        name="even_mixer_ctx" if ctx else "even_mixer_lat",
    )(*args)


def _gla_kernel(seq_len, has_init, want_final, *refs):
    chunk = GLA_CHUNK
    n_chunks = seq_len // chunk
    it = iter(refs)
    q_ref, k_ref, v_ref, gr_ref, lr_ref, wz_ref, bz_ref = (next(it) for _ in range(7))
    s0_ref = next(it) if has_init else None
    out_ref = next(it)
    s_out_ref = next(it) if want_final else None
    qe_ref, ke_ref, kw_ref, dec_ref, stb_ref = (next(it) for _ in range(5))

    z = jnp.dot(lr_ref[...], wz_ref[...], precision=lax.Precision.HIGHEST, preferred_element_type=F32) + bz_ref[...]
    la = _log_sigmoid(z) / GLA_TAU
    pos = lax.broadcasted_iota(jnp.int32, (seq_len, 1), 0) % chunk
    pre = _seg_scan(la, chunk, pos, jnp.add, 0.0, False)
    suf = _seg_scan(la, chunk, pos, jnp.add, 0.0, True)
    q = q_ref[...] * (DK ** -0.5)
    k = k_ref[...]
    b_f, b_b = pre[:, :128], suf[:, 128:]
    qe_ref[...] = jnp.concatenate([q * jnp.exp(b_f), q * jnp.exp(b_b)], axis=1).astype(BF16)
    ke_ref[...] = jnp.concatenate([k * jnp.exp(-b_f), k * jnp.exp(-b_b)], axis=1).astype(BF16)
    kw_ref[...] = jnp.concatenate([k * jnp.exp(suf[:, :128] - la[:, :128]),
                                   k * jnp.exp(pre[:, 128:] - la[:, 128:])], axis=1).astype(BF16)
    dec_ref[...] = jnp.concatenate([pre[:, :128], suf[:, 128:]], axis=1)

    ii = lax.broadcasted_iota(jnp.int32, (chunk, chunk), 0)
    jj = lax.broadcasted_iota(jnp.int32, (chunk, chunk), 1)

    def rows(c):
        return pl.ds(pl.multiple_of(c * chunk, chunk), chunk)

    def update(c, d, st):
        edge = c * chunk + (chunk - 1 if d == 0 else 0)
        dec = jnp.exp(dec_ref[pl.ds(edge, 1), :][:, 128 * d:128 * d + 128])
        kw = kw_ref[rows(c), :][:, 128 * d:128 * d + 128]
        return st * dec + _dot_tn(v_ref[rows(c), :].astype(BF16), kw)

    if has_init:
        st_f0, st_b0 = s0_ref[0].T, s0_ref[1].T
    else:
        st_f0 = st_b0 = jnp.zeros((DV_GLA, DK), F32)

    def bwd_step(t, st_b):
        c = n_chunks - 1 - t
        stb_ref[c] = st_b.astype(BF16)
        return update(c, 1, st_b)

    st_b_fin = lax.fori_loop(0, n_chunks, bwd_step, st_b0)

    def fwd_step(c, st_f):
        qe = qe_ref[rows(c), :]
        ke = ke_ref[rows(c), :]
        att = (jnp.where(jj <= ii, _dot_nt(qe[:, :128], ke[:, :128]), 0.0)
               + jnp.where(jj >= ii, _dot_nt(qe[:, 128:], ke[:, 128:]), 0.0))
        o = _dot(att.astype(BF16), v_ref[rows(c), :].astype(BF16))
        o = o + _dot_nt(qe, jnp.concatenate([st_f.astype(BF16), stb_ref[c]], axis=1))
        gr = gr_ref[rows(c), :]
        out_ref[rows(c), :] = (_rms(o) * (gr * jax.nn.sigmoid(gr))).astype(BF16)
        return update(c, 0, st_f)

    st_f_fin = lax.fori_loop(0, n_chunks, fwd_step, st_f0)
    if want_final:
        s_out_ref[0] = st_f_fin.T
        s_out_ref[1] = st_b_fin.T


def _gla_mixer(proj, lowrank, wz, bz, group, init):
    ctx = group == 0
    n_seq, seq_len = (N_CTX_SEQ, CTX_LEN) if ctx else (N_LAT_SEQ, LAT_LEN)
    row_off = 0 if ctx else N_CTX_TOK // LAT_LEN
    has_init, want_final = not ctx, ctx
    n_chunks = seq_len // GLA_CHUNK
    in_specs = [pl.BlockSpec((seq_len, 128), lambda b, h: (row_off + b, h)),
                pl.BlockSpec((seq_len, 128), lambda b, h: (row_off + b, 4 + h)),
                pl.BlockSpec((seq_len, 256), lambda b, h: (row_off + b, 4 + h)),
                pl.BlockSpec((seq_len, 256), lambda b, h: (row_off + b, 8 + h)),
                pl.BlockSpec((seq_len, 128), lambda b, h: (row_off + b, 0)),
                pl.BlockSpec((None, 128, 256), lambda b, h: (h, 0, 0)),
                pl.BlockSpec((None, 1, 256), lambda b, h: (h, 0, 0))]
    args = [proj, proj, proj, proj, lowrank, wz, bz]
    if has_init:
        in_specs.append(pl.BlockSpec((None, 2, None, DK, DV_GLA), lambda b, h: (b, 0, h, 0, 0)))
        args.append(init)
    out_specs = [pl.BlockSpec((seq_len, 256), lambda b, h: (b, h))]
    out_shape = [jax.ShapeDtypeStruct((n_seq * seq_len, D), BF16)]
    if want_final:
        out_specs.append(pl.BlockSpec((None, 2, None, DK, DV_GLA), lambda b, h: (b, 0, h, 0, 0)))
        out_shape.append(jax.ShapeDtypeStruct((n_seq, 2, HEADS, DK, DV_GLA), F32))
    scratch = [pltpu.VMEM((seq_len, 256), BF16)] * 3 + [pltpu.VMEM((seq_len, 256), F32),
                                                        pltpu.VMEM((n_chunks, DV_GLA, DK), BF16)]
    return pl.pallas_call(
        functools.partial(_gla_kernel, seq_len, has_init, want_final),
        grid=(n_seq, HEADS),
        in_specs=in_specs,
        out_specs=out_specs,
        out_shape=out_shape,
        scratch_shapes=scratch,
        compiler_params=_cparams("parallel", "parallel"),
        name="gla_mixer_ctx" if ctx else "gla_mixer_lat",
    )(*args)


def _residual_and_next(y, x_ref, mod_ref, g_post_ref, gate_row, x_out_ref, nxt):
    m = mod_ref[...]
    x = x_ref[...] + m[gate_row:gate_row + 1] * (_rms(y) * g_post_ref[...])
    x_out_ref[...] = x
    if nxt is not None:
        g_next_ref, mod_next_ref, shift_row, xn_ref = nxt
        mn = mod_next_ref[...]
        xn_ref[...] = (_rms(x) * g_next_ref[...] * (1.0 + mn[shift_row + 1:shift_row + 2])
                       + mn[shift_row:shift_row + 1]).astype(BF16)


def _post_kernel(hp_ref, hs_ref, w_ref, x_ref, mod_ref, g_post_ref, g_next_ref, x_out_ref, xn_ref):
    i = pl.program_id(0)

    def run(h_ref):
        y = _dot(h_ref[...], w_ref[...])
        _residual_and_next(y, x_ref, mod_ref, g_post_ref, 2, x_out_ref, (g_next_ref, mod_ref, 3, xn_ref))

    pl.when(i < N_CTX_TOK // ROW_TILE)(lambda: run(hp_ref))
    pl.when(i >= N_CTX_TOK // ROW_TILE)(lambda: run(hs_ref))


def _post_mixer(h_ctx, h_lat, w_out, x, mods, g_post, g_next, layer):
    nc = N_CTX_TOK // ROW_TILE
    row = pl.BlockSpec((ROW_TILE, D), lambda i: (i, 0))
    return pl.pallas_call(
        _post_kernel,
        grid=(N_TOK // ROW_TILE,),
        in_specs=[pl.BlockSpec((ROW_TILE, D), lambda i: (jnp.minimum(i, nc - 1), 0)),
                  pl.BlockSpec((ROW_TILE, D), lambda i: (jnp.maximum(i - nc, 0), 0)),
                  pl.BlockSpec((D, D), lambda i: (0, 0)),
                  row,
                  pl.BlockSpec((None, None, 6, D), lambda i: (layer, _mod_row(i, ROW_TILE), 0, 0)),
                  pl.BlockSpec((None, 1, D), lambda i: (layer, 0, 0)),
                  pl.BlockSpec((None, 1, D), lambda i: (layer, 0, 0))],
        out_specs=[row, row],
        out_shape=[jax.ShapeDtypeStruct((N_TOK, D), F32), jax.ShapeDtypeStruct((N_TOK, D), BF16)],
        compiler_params=_cparams("parallel"),
        name="post_mixer",
    )(h_ctx, h_lat, w_out, x, mods, g_post, g_next)


def _ffn_kernel(last, *refs):
    it = iter(refs)
    a_ref, up_ref, dn_ref, v_ref, cw_ref, cb_ref, w_ref, x_ref, mod_ref, g_post_ref = (next(it) for _ in range(10))
    if not last:
        g_next_ref, mod_next_ref = next(it), next(it)
    x_out_ref = next(it)
    xn_ref = None if last else next(it)
    hbuf_ref = next(it)
    i = pl.program_id(0)
    n_ctx = N_CTX_TOK // ROW_TILE
    tiles_per_seq = LAT_LEN // ROW_TILE

    def act(conv, sl):
        return (jax.nn.gelu(conv + cb_ref[:, sl]) * v_ref[:, sl]).astype(BF16)

    def ctx_branch():
        pos = lax.broadcasted_iota(jnp.int32, (ROW_TILE, 1), 0)
        for f in range(D_FF // FF_CHUNK):
            sl = slice(f * FF_CHUNK, (f + 1) * FF_CHUNK)
            a = a_ref[:, sl]
            left = jnp.where(pos >= 1, pltpu.roll(a, 1, 0), 0.0)
            right = jnp.where(pos < ROW_TILE - 1, pltpu.roll(a, ROW_TILE - 1, 0), 0.0)
            conv = cw_ref[3:4, sl] * left + cw_ref[4:5, sl] * a + cw_ref[5:6, sl] * right
            hbuf_ref[:, sl] = act(conv, sl)

    def lat_branch():
        t = (i - n_ctx) % tiles_per_seq
        n_rows = ROW_TILE + 2 * GRID_W
        colpos = lax.broadcasted_iota(jnp.int32, (n_rows, 1), 0) % GRID_W
        for f in range(D_FF // FF_CHUNK):
            sl = slice(f * FF_CHUNK, (f + 1) * FF_CHUNK)
            up = jnp.where(t > 0, up_ref[:, sl], 0.0)
            dn = jnp.where(t < tiles_per_seq - 1, dn_ref[:, sl], 0.0)
            buf = jnp.concatenate([up, a_ref[:, sl], dn], axis=0)
            left = jnp.where(colpos >= 1, pltpu.roll(buf, 1, 0), 0.0)
            right = jnp.where(colpos < GRID_W - 1, pltpu.roll(buf, n_rows - 1, 0), 0.0)
            conv = None
            for dr in range(3):
                rs = slice(dr * GRID_W, dr * GRID_W + ROW_TILE)
                term = (cw_ref[3 * dr:3 * dr + 1, sl] * left[rs] + cw_ref[3 * dr + 1:3 * dr + 2, sl] * buf[rs]
                        + cw_ref[3 * dr + 2:3 * dr + 3, sl] * right[rs])
                conv = term if conv is None else conv + term
            hbuf_ref[:, sl] = act(conv, sl)

    pl.when(i < n_ctx)(ctx_branch)
    pl.when(i >= n_ctx)(lat_branch)
    y = _dot(hbuf_ref[...], w_ref[...])
    nxt = None if last else (g_next_ref, mod_next_ref, 0, xn_ref)
    _residual_and_next(y, x_ref, mod_ref, g_post_ref, 5, x_out_ref, nxt)


def _ffn_tail(av, conv_w, conv_b, w_out, x, mods, g_post, g_next, layer):
    last = layer == DEPTH - 1
    halo_per_tile = ROW_TILE // GRID_W
    n_halo = N_TOK // GRID_W
    row = pl.BlockSpec((ROW_TILE, D), lambda i: (i, 0))
    in_specs = [pl.BlockSpec((ROW_TILE, D_FF), lambda i: (i, 0)),
                pl.BlockSpec((GRID_W, D_FF), lambda i: (jnp.maximum(i * halo_per_tile - 1, 0), 0)),
                pl.BlockSpec((GRID_W, D_FF), lambda i: (jnp.minimum((i + 1) * halo_per_tile, n_halo - 1), 0)),
                pl.BlockSpec((ROW_TILE, D_FF), lambda i: (i, 1)),
                pl.BlockSpec((None, 9, D_FF), lambda i: (layer, 0, 0)),
                pl.BlockSpec((None, 1, D_FF), lambda i: (layer, 0, 0)),
                pl.BlockSpec((D_FF, D), lambda i: (0, 0)),
                row,
                pl.BlockSpec((None, None, 6, D), lambda i: (layer, _mod_row(i, ROW_TILE), 0, 0)),
                pl.BlockSpec((None, 1, D), lambda i: (layer, 0, 0))]
    args = [av, av, av, av, conv_w.reshape(DEPTH, 9, D_FF), conv_b, w_out, x, mods, g_post]
    out_specs = [row]
    out_shape = [jax.ShapeDtypeStruct((N_TOK, D), F32)]
    if not last:
        in_specs += [pl.BlockSpec((None, 1, D), lambda i: (layer + 1, 0, 0)),
                     pl.BlockSpec((None, None, 6, D), lambda i: (layer + 1, _mod_row(i, ROW_TILE), 0, 0))]
        args += [g_next, mods]
        out_specs.append(row)
        out_shape.append(jax.ShapeDtypeStruct((N_TOK, D), BF16))
    return pl.pallas_call(
        functools.partial(_ffn_kernel, last),
        grid=(N_TOK // ROW_TILE,),
        in_specs=in_specs,
        out_specs=out_specs,
        out_shape=out_shape,
        scratch_shapes=[pltpu.VMEM((ROW_TILE, D_FF), BF16)],
        compiler_params=_cparams("parallel"),
        name="ffn_tail",
    )(*args)


def _even_weights(w_in, w_out):
    w_main = jnp.concatenate([w_in[:, :2048], w_in[:, 2064:]], axis=1).astype(BF16)
    src = [2048] * 128
    used = [0.0] * 128
    for h in range(HEADS):
        for k, col in enumerate((h, 8 + h, 4 + h, 12 + h)):
            src[(128 - 4 * h) % 128 + k] = 2048 + col
            used[(128 - 4 * h) % 128 + k] = 1.0
    w_gate = (w_in[:, jnp.array(src)] * jnp.array(used, F32)).astype(BF16)
    w_o = w_out.reshape(2, HEADS, 128, D).transpose(1, 0, 2, 3).reshape(D, D).astype(BF16)
    return w_main, w_gate, w_o


def _odd_weights(w_in, w2, b2):
    w_main = w_in[:, :3072].astype(BF16)
    w_lr = jnp.pad(w_in[:, 3072:3072 + 2 * GLA_RANK], ((0, 0), (0, 128 - 2 * GLA_RANK))).astype(BF16)
    w2h = w2.reshape(2, GLA_RANK, HEADS, DK).transpose(2, 0, 1, 3)
    wz = jnp.zeros((HEADS, 128, 256), F32)
    wz = wz.at[:, :GLA_RANK, :DK].set(w2h[:, 0]).at[:, GLA_RANK:2 * GLA_RANK, DK:].set(w2h[:, 1])
    bz = b2.reshape(2, HEADS, DK).transpose(1, 0, 2).reshape(HEADS, 1, 2 * DK)
    return w_main, w_lr, wz, bz


def kernel(x_prompt, x_sample, c, c_ctx, state_mlstm_C, state_mlstm_n, state_mlstm_m, state_ret, state_gla, ada_w, ada_b, norm_mix_pre, norm_mix_post, norm_ffn_pre, norm_ffn_post, w_in_even, w_out_even, mlstm_igate_b, mlstm_fgate_b, ret_decay_logit, w_in_odd, gla_gate_w2, gla_gate_b, w_out_odd, ffn_w_in, ffn_conv_w, ffn_conv_b, ffn_w_out):
    cond = jnp.concatenate([c_ctx[None, :], c, jnp.zeros((8 - 1 - N_LAT_SEQ, D), F32)], axis=0)
    mods = _mod_table(cond, ada_w, ada_b)
    norm_mix_pre, norm_mix_post, norm_ffn_pre, norm_ffn_post = (
        g.reshape(DEPTH, 1, D) for g in (norm_mix_pre, norm_mix_post, norm_ffn_pre, norm_ffn_post))
    ffn_conv_b = ffn_conv_b.reshape(DEPTH, 1, D_FF)

    x, xn = _prenorm(x_prompt.reshape(N_CTX_TOK, D), x_sample.reshape(N_LAT_SEQ * LAT_LEN, D),
                     norm_mix_pre, mods, 0)
    new_c, new_n, new_m, new_r, new_g = [], [], [], [], []
    for layer in range(DEPTH):
        idx = layer // 2
        if layer % 2 == 0:
            w_main, w_gate, w_o = _even_weights(w_in_even[idx], w_out_even[idx])
            proj = _matmul(xn, w_main, MM_ROW_TILE, 1024, "proj_even")
            gates = _matmul(xn, w_gate, 1024, 128, "proj_gates")
            params = jnp.concatenate([mlstm_igate_b[idx].reshape(-1), mlstm_fgate_b[idx].reshape(-1),
                                      ret_decay_logit[idx].reshape(-1)])
            h_ctx, cn, m_fin, r_fin = _even_mixer(proj, gates, params, 0, None)
            init = (state_mlstm_C[:, idx], state_mlstm_n[:, idx], state_mlstm_m[:, idx], state_ret[:, idx])
            (h_lat,) = _even_mixer(proj, gates, params, 1, init)
            new_c.append(cn[..., :DK])
            new_n.append(cn[..., DK])
            new_m.append(m_fin[:, :, 0:2, 0].transpose(0, 2, 1))
            new_r.append(r_fin)
        else:
            w_main, w_lr, wz, bz = _odd_weights(w_in_odd[idx], gla_gate_w2[idx], gla_gate_b[idx])
            w_o = w_out_odd[idx].astype(BF16)
            proj = _matmul(xn, w_main, MM_ROW_TILE, 1024, "proj_odd")
            lowrank = _matmul(xn, w_lr, 1024, 128, "proj_lowrank")
            h_ctx, s_fin = _gla_mixer(proj, lowrank, wz, bz, 0, None)
            (h_lat,) = _gla_mixer(proj, lowrank, wz, bz, 1, state_gla[:, idx])
            new_g.append(s_fin)
        x, xn = _post_mixer(h_ctx, h_lat, w_o, x, mods, norm_mix_post, norm_ffn_pre, layer)
        av = _matmul(xn, ffn_w_in[layer].astype(BF16), MM_ROW_TILE, D_FF // 2, "ffn_in")
        res = _ffn_tail(av, ffn_conv_w, ffn_conv_b, ffn_w_out[layer].astype(BF16), x, mods,
                        norm_ffn_post, norm_mix_pre, layer)
        x = res[0]
        xn = res[1] if layer < DEPTH - 1 else None

    y_prompt = x[:N_CTX_TOK].reshape(N_CTX_SEQ, CTX_LEN, D)
    y_sample = x[N_CTX_TOK:].reshape(N_LAT_SEQ, LAT_LEN, D)
    return (y_prompt, y_sample, jnp.stack(new_c, axis=1), jnp.stack(new_n, axis=1), jnp.stack(new_m, axis=1),
            jnp.stack(new_r, axis=1), jnp.stack(new_g, axis=1))
```

```python
import functools

import jax
import jax.numpy as jnp
from jax import lax
from jax.experimental import pallas as pl
from jax.experimental.pallas import tpu as pltpu

F32 = jnp.float32
BF16 = jnp.bfloat16

D = 1024
DEPTH = 4
N_CTX_SEQ, CTX_LEN = 16, 256
N_LAT_SEQ, LAT_LEN = 2, 2048
GRID_W = 64
N_CTX_TOK = N_CTX_SEQ * CTX_LEN
N_TOK = N_CTX_TOK + N_LAT_SEQ * LAT_LEN
HEADS = 4
DK = 128
DV_GLA = 256
GLA_RANK = 16
GLA_TAU = 16.0
D_FF = 2816
EPS = 1e-6
NEG = -1e30

ROW_TILE = 256
MM_ROW_TILE = 512
EVEN_CHUNK = 256
GLA_CHUNK = 64
GLA_UNROLL = 4
FF_CHUNK = 256
VMEM_LIMIT = 56 * 1024 * 1024


def _cparams(*sem):
    return pltpu.CompilerParams(dimension_semantics=sem, vmem_limit_bytes=VMEM_LIMIT)


def _mod_row(i, tile):
    n_ctx = N_CTX_TOK // tile
    return jnp.where(i < n_ctx, 0, 1 + (i - n_ctx) // (LAT_LEN // tile))


def _rms(x):
    return x * lax.rsqrt(jnp.mean(x * x, axis=-1, keepdims=True) + EPS)


def _log_sigmoid(x):
    return jnp.minimum(x, 0.0) - jnp.log1p(jnp.exp(-jnp.abs(x)))


def _dot(a, b):
    return jnp.dot(a, b, preferred_element_type=F32)


def _dot_nt(a, b):
    return lax.dot_general(a, b, (((1,), (1,)), ((), ())), preferred_element_type=F32)


def _dot_tn(a, b):
    return lax.dot_general(a, b, (((0,), (0,)), ((), ())), preferred_element_type=F32)


def _seg_scan(x, seg, pos, op, ident, reverse):
    n = x.shape[0]
    s = 1
    while s < seg:
        if reverse:
            x = op(x, jnp.where(pos < seg - s, pltpu.roll(x, n - s, 0), ident))
        else:
            x = op(x, jnp.where(pos >= s, pltpu.roll(x, s, 0), ident))
        s *= 2
    return x


def _mod_kernel(c_ref, w_ref, b_ref, o_ref):
    c = c_ref[...]
    s = (c * jax.nn.sigmoid(c)).astype(BF16)
    o_ref[0] = _dot(s, w_ref[0].astype(BF16)) + b_ref[0]


def _mod_table(cond, ada_w, ada_b):
    tn = 1024
    out = pl.pallas_call(
        _mod_kernel,
        grid=(DEPTH, 6 * D // tn),
        in_specs=[pl.BlockSpec((8, D), lambda l, j: (0, 0)),
                  pl.BlockSpec((1, D, tn), lambda l, j: (l, 0, j)),
                  pl.BlockSpec((1, 1, tn), lambda l, j: (l, 0, j))],
        out_specs=pl.BlockSpec((1, 8, tn), lambda l, j: (l, 0, j)),
        out_shape=jax.ShapeDtypeStruct((DEPTH, 8, 6 * D), F32),
        compiler_params=_cparams("parallel", "parallel"),
        name="mod_table",
    )(cond, ada_w, ada_b.reshape(DEPTH, 1, 6 * D))
    return out.reshape(DEPTH, 8, 6, D)


def _prenorm_kernel(xp_ref, xs_ref, g_ref, mod_ref, x_ref, xn_ref):
    i = pl.program_id(0)

    def run(src_ref):
        x = src_ref[...]
        m = mod_ref[...]
        x_ref[...] = x
        xn_ref[...] = (_rms(x) * g_ref[...] * (1.0 + m[1:2]) + m[0:1]).astype(BF16)

    pl.when(i < N_CTX_TOK // ROW_TILE)(lambda: run(xp_ref))
    pl.when(i >= N_CTX_TOK // ROW_TILE)(lambda: run(xs_ref))


def _prenorm(xp, xs, g, mods, layer):
    nc = N_CTX_TOK // ROW_TILE
    return pl.pallas_call(
        _prenorm_kernel,
        grid=(N_TOK // ROW_TILE,),
        in_specs=[pl.BlockSpec((ROW_TILE, D), lambda i: (jnp.minimum(i, nc - 1), 0)),
                  pl.BlockSpec((ROW_TILE, D), lambda i: (jnp.maximum(i - nc, 0), 0)),
                  pl.BlockSpec((None, 1, D), lambda i: (layer, 0, 0)),
                  pl.BlockSpec((None, None, 6, D), lambda i: (layer, _mod_row(i, ROW_TILE), 0, 0))],
        out_specs=[pl.BlockSpec((ROW_TILE, D), lambda i: (i, 0)),
                   pl.BlockSpec((ROW_TILE, D), lambda i: (i, 0))],
        out_shape=[jax.ShapeDtypeStruct((N_TOK, D), F32), jax.ShapeDtypeStruct((N_TOK, D), BF16)],
        compiler_params=_cparams("parallel"),
        name="prenorm",
    )(xp, xs, g, mods)


def _mm_kernel(x_ref, w_ref, o_ref):
    o_ref[...] = _dot(x_ref[...], w_ref[...]).astype(o_ref.dtype)


def _matmul(x, w, tm, tn, name):
    m, k = x.shape
    n = w.shape[1]
    return pl.pallas_call(
        _mm_kernel,
        grid=(n // tn, m // tm),
        in_specs=[pl.BlockSpec((tm, k), lambda j, i: (i, 0)),
                  pl.BlockSpec((k, tn), lambda j, i: (0, j))],
        out_specs=pl.BlockSpec((tm, tn), lambda j, i: (i, j)),
        out_shape=jax.ShapeDtypeStruct((m, n), F32),
        compiler_params=_cparams("parallel", "parallel"),
        name=name,
    )(x, w)


def _even_kernel(seq_len, init_layer, want_final, *refs):
    has_init = init_layer is not None
    layer_idx, n_layers = init_layer if has_init else (0, 1)
    chunk = EVEN_CHUNK
    n_chunks = seq_len // chunk
    it = iter(refs)
    par_ref = next(it)
    m0_ref = next(it) if has_init else None
    mq_ref, mk_ref, mv_ref, mo_ref, rq_ref, rk_ref, rv_ref, rg_ref, gate_ref = (next(it) for _ in range(9))
    if has_init:
        c0_ref, n0_ref, r0_ref = next(it), next(it), next(it)
    out_ref = next(it)
    if want_final:
        cn_out_ref, m_out_ref, r_out_ref = next(it), next(it), next(it)
    fs_ref, gs_ref, gm_ref = next(it), next(it), next(it)
    if n_chunks > 1:
        cnb_ref, mb_ref, rb_ref = next(it), next(it), next(it)

    b = pl.program_id(0)
    h = pl.program_id(1)
    lane = lax.broadcasted_iota(jnp.int32, (1, 128), 1)
    ii = lax.broadcasted_iota(jnp.int32, (chunk, chunk), 0)
    jj = lax.broadcasted_iota(jnp.int32, (chunk, chunk), 1)
    pos_col = lax.broadcasted_iota(jnp.int32, (chunk, 1), 0).astype(F32)
    ones_col = (lax.broadcasted_iota(jnp.int32, (chunk, 128), 1) == 0).astype(BF16)

    g_all = pltpu.roll(gate_ref[...], 4 * h, 1)
    bias_i = jnp.where(lane == 0, par_ref[h], par_ref[4 + h])
    bias_f = jnp.where(lane == 0, par_ref[8 + h], par_ref[12 + h])
    i_pre = g_all + bias_i
    log_f = _log_sigmoid(pltpu.roll(g_all, 126, 1) + bias_f)
    pos = lax.broadcasted_iota(jnp.int32, (seq_len, 1), 0) % chunk
    fcum = jnp.where(lane == 0, _seg_scan(log_f, chunk, pos, jnp.add, 0.0, False),
                     _seg_scan(log_f, chunk, pos, jnp.add, 0.0, True))
    gsc = i_pre - fcum
    gmax = jnp.where(lane == 0, _seg_scan(gsc, chunk, pos, jnp.maximum, NEG, False),
                     _seg_scan(gsc, chunk, pos, jnp.maximum, NEG, True))
    fs_ref[...] = fcum
    gs_ref[...] = gsc
    gm_ref[...] = gmax

    lg_f = _log_sigmoid(jnp.full((1, 1), par_ref[16 + h], F32))
    lg_b = _log_sigmoid(jnp.full((1, 1), par_ref[20 + h], F32))
    diff = (ii - jj).astype(F32)
    ret_mask = (jnp.where(diff >= 0, jnp.exp(lg_f * jnp.maximum(diff, 0.0)), 0.0)
                + jnp.where(diff <= 0, jnp.exp(lg_b * jnp.maximum(-diff, 0.0)), 0.0))
    rq_dec_f = jnp.exp(lg_f * (pos_col + 1.0))
    rq_dec_b = jnp.exp(lg_b * (chunk - pos_col))
    rk_dec_f = jnp.exp(lg_f * (chunk - 1.0 - pos_col))
    rk_dec_b = jnp.exp(lg_b * pos_col)
    r_chunk_f = jnp.exp(lg_f * chunk)
    r_chunk_b = jnp.exp(lg_b * chunk)

    def rows(c):
        if isinstance(c, int):
            return pl.ds(c * chunk, chunk)
        return pl.ds(pl.multiple_of(c * chunk, chunk), chunk)

    def load_kv(c):
        k = mk_ref[rows(c), :] * (DK ** -0.5)
        v = mv_ref[rows(c), :].astype(BF16)
        return k, jnp.concatenate([v, ones_col], axis=1)

    def ret_kv(c):
        return rk_ref[rows(c), :] * (DK ** -0.5), rv_ref[rows(c), :].astype(BF16)

    def mlstm_update(c, d, cn, m):
        edge = c * chunk + (chunk - 1 if d == 0 else 0)
        f_edge = fs_ref[pl.ds(edge, 1), :][:, d:d + 1]
        g_edge = gm_ref[pl.ds(edge, 1), :][:, d:d + 1]
        mu = jnp.maximum(m, g_edge)
        k, vext = load_kv(c)
        kw = (k * jnp.exp(gs_ref[rows(c), :][:, d:d + 1] - mu)).astype(BF16)
        return jnp.exp(m - mu) * cn + _dot_tn(kw, vext), f_edge + mu

    def ret_update(c, d, s):
        k, v = ret_kv(c)
        kw = (k * (rk_dec_f if d == 0 else rk_dec_b)).astype(BF16)
        return (r_chunk_f if d == 0 else r_chunk_b) * s + _dot_tn(kw, v)

    if has_init:
        base = (b * n_layers + layer_idx) * 8
        m_f0 = jnp.full((1, 1), m0_ref[base + h], F32)
        m_b0 = jnp.full((1, 1), m0_ref[base + 4 + h], F32)
        lane256 = lax.broadcasted_iota(jnp.int32, (DK, 128), 1)
        cn_f0 = jnp.concatenate([c0_ref[0], jnp.where(lane256 == 0, n0_ref[0], 0.0)], axis=1)
        cn_b0 = jnp.concatenate([c0_ref[1], jnp.where(lane256 == 0, n0_ref[1], 0.0)], axis=1)
        s_f0, s_b0 = r0_ref[0], r0_ref[1]
    else:
        m_f0 = m_b0 = jnp.zeros((1, 1), F32)
        cn_f0 = cn_b0 = jnp.zeros((DK, 256), F32)
        s_f0 = s_b0 = jnp.zeros((DK, DK), F32)
    use_cross = has_init or n_chunks > 1

    if n_chunks > 1:
        def bwd_step(t, carry):
            cn_b, m_b, s_b = carry
            c = n_chunks - 1 - t
            cnb_ref[c] = cn_b
            mb_ref[c] = jnp.broadcast_to(m_b, (8, 128))
            rb_ref[c] = s_b
            cn_b, m_b = mlstm_update(c, 1, cn_b, m_b)
            return cn_b, m_b, ret_update(c, 1, s_b)

        cn_b_fin, m_b_fin, s_b_fin = lax.fori_loop(0, n_chunks, bwd_step, (cn_b0, m_b0, s_b0))
    elif want_final:
        cn_b_fin, m_b_fin = mlstm_update(0, 1, cn_b0, m_b0)
        s_b_fin = ret_update(0, 1, s_b0)

    def fwd_step(c, carry):
        cn_f, m_f, s_f = carry
        if n_chunks > 1:
            cn_b, m_b, s_b = cnb_ref[c], mb_ref[c][0:1, 0:1], rb_ref[c]
        else:
            cn_b, m_b, s_b = cn_b0, m_b0, s_b0
        q = mq_ref[rows(c), :].astype(BF16)
        k, vext = load_kv(c)
        scores = _dot_nt(q, k.astype(BF16))
        gcol = gs_ref[rows(c), :]
        gmx = gm_ref[rows(c), :]
        fcol = fs_ref[rows(c), :]
        eye = ii == jj
        grow_f = jnp.sum(jnp.where(eye, gcol[:, 0:1], 0.0), axis=0, keepdims=True)
        grow_b = jnp.sum(jnp.where(eye, gcol[:, 1:2], 0.0), axis=0, keepdims=True)
        mu_f = jnp.maximum(m_f, gmx[:, 0:1])
        mu_b = jnp.maximum(m_b, gmx[:, 1:2])
        p_f = jnp.exp(jnp.where(jj <= ii, grow_f - mu_f, NEG)) * scores
        p_b = jnp.exp(jnp.where(jj >= ii, grow_b - mu_b, NEG)) * scores
        tot = _dot(jnp.concatenate([p_f, p_b], axis=0).astype(BF16), vext)
        tot_f, tot_b = tot[:chunk], tot[chunk:]
        if use_cross:
            cross = _dot(q, jnp.concatenate([cn_f, cn_b], axis=1).astype(BF16))
            tot_f = tot_f + jnp.exp(m_f - mu_f) * cross[:, :256]
            tot_b = tot_b + jnp.exp(m_b - mu_b) * cross[:, 256:]
        h_f = tot_f[:, :128] / jnp.maximum(jnp.abs(tot_f[:, 128:129]), jnp.exp(-(fcol[:, 0:1] + mu_f)))
        h_b = tot_b[:, :128] / jnp.maximum(jnp.abs(tot_b[:, 128:129]), jnp.exp(-(fcol[:, 1:2] + mu_b)))
        out_m = _rms(h_f + h_b) * jax.nn.sigmoid(mo_ref[rows(c), :])
        rq = rq_ref[rows(c), :]
        rk, rv = ret_kv(c)
        o_r = _dot((_dot_nt(rq.astype(BF16), rk.astype(BF16)) * ret_mask).astype(BF16), rv)
        if use_cross:
            q_in = jnp.concatenate([rq * rq_dec_f, rq * rq_dec_b], axis=1).astype(BF16)
            o_r = o_r + _dot(q_in, jnp.concatenate([s_f, s_b], axis=0).astype(BF16))
        rg = rg_ref[rows(c), :]
        out_r = _rms(o_r) * (rg * jax.nn.sigmoid(rg))
        out_ref[rows(c), :] = jnp.concatenate([out_m, out_r], axis=1).astype(BF16)
        if want_final or n_chunks > 1:
            cn_f, m_f = mlstm_update(c, 0, cn_f, m_f)
            s_f = ret_update(c, 0, s_f)
        return cn_f, m_f, s_f

    if n_chunks > 1:
        cn_f_fin, m_f_fin, s_f_fin = lax.fori_loop(0, n_chunks, fwd_step, (cn_f0, m_f0, s_f0))
    else:
        cn_f_fin, m_f_fin, s_f_fin = fwd_step(0, (cn_f0, m_f0, s_f0))

    if want_final:
        cn_out_ref[0] = cn_f_fin
        cn_out_ref[1] = cn_b_fin
        r_out_ref[0] = s_f_fin
        r_out_ref[1] = s_b_fin
        m_out_ref[...] = jnp.concatenate([jnp.broadcast_to(m_f_fin, (1, 128)), jnp.broadcast_to(m_b_fin, (1, 128)),
                                          jnp.zeros((6, 128), F32)], axis=0)


def _even_mixer(proj, gates, params, group, init, layer_idx=0):
    ctx = group == 0
    n_seq, seq_len = (N_CTX_SEQ, CTX_LEN) if ctx else (N_LAT_SEQ, LAT_LEN)
    row_off = 0 if ctx else N_CTX_TOK // LAT_LEN
    has_init, want_final = not ctx, ctx
    n_chunks = seq_len // EVEN_CHUNK

    def col(k):
        return pl.BlockSpec((seq_len, 128), lambda b, h: (row_off + b, 4 * k + h))

    smem = pl.BlockSpec(memory_space=pltpu.SMEM)
    in_specs = [smem]
    args = [params]
    if has_init:
        c0, n0, m0, r0 = init
        n_layers = c0.shape[1]
        in_specs.append(smem)
        args.append(m0.reshape(-1))
    in_specs += [col(k) for k in range(8)]
    in_specs.append(pl.BlockSpec((seq_len, 128), lambda b, h: (row_off + b, 0)))
    args += [proj] * 8 + [gates]
    if has_init:
        in_specs += [pl.BlockSpec((None, None, 2, None, DK, 128), lambda b, h: (b, layer_idx, 0, h, 0, 0)),
                     pl.BlockSpec((None, None, 2, None, DK, 1), lambda b, h: (b, layer_idx, 0, h, 0, 0)),
                     pl.BlockSpec((None, None, 2, None, DK, 128), lambda b, h: (b, layer_idx, 0, h, 0, 0))]
        args += [c0, n0.reshape(n0.shape + (1,)), r0]
    out_specs = [pl.BlockSpec((seq_len, 256), lambda b, h: (b, h))]
    out_shape = [jax.ShapeDtypeStruct((n_seq * seq_len, D), BF16)]
    if want_final:
        out_specs += [pl.BlockSpec((None, 2, None, DK, 256), lambda b, h: (b, 0, h, 0, 0)),
                      pl.BlockSpec((None, None, 8, 128), lambda b, h: (b, h, 0, 0)),
                      pl.BlockSpec((None, 2, None, DK, 128), lambda b, h: (b, 0, h, 0, 0))]
        out_shape += [jax.ShapeDtypeStruct((n_seq, 2, HEADS, DK, 256), F32),
                      jax.ShapeDtypeStruct((n_seq, HEADS, 8, 128), F32),
                      jax.ShapeDtypeStruct((n_seq, 2, HEADS, DK, 128), F32)]
    scratch = [pltpu.VMEM((seq_len, 128), F32)] * 3
    if n_chunks > 1:
        scratch += [pltpu.VMEM((n_chunks, DK, 256), F32), pltpu.VMEM((n_chunks, 8, 128), F32),
                    pltpu.VMEM((n_chunks, DK, 128), F32)]
    return pl.pallas_call(
        functools.partial(_even_kernel, seq_len, (layer_idx, n_layers) if has_init else None, want_final),
        grid=(n_seq, HEADS),
        in_specs=in_specs,
        out_specs=out_specs,
        out_shape=out_shape,
        scratch_shapes=scratch,
        compiler_params=_cparams("parallel", "parallel"),
        name="even_mixer_ctx" if ctx else "even_mixer_lat",
    )(*args)


def _gla_kernel(seq_len, has_init, want_final, *refs):
    chunk = GLA_CHUNK
    n_chunks = seq_len // chunk
    it = iter(refs)
    q_ref, k_ref, v_ref, gr_ref, lr_ref, wz_ref, bz_ref = (next(it) for _ in range(7))
    s0_ref = next(it) if has_init else None
    out_ref = next(it)
    s_out_ref = next(it) if want_final else None
    qe_ref, ke_ref, cum_ref, sts_ref, st_ref = (next(it) for _ in range(5))
    unroll = min(GLA_UNROLL, n_chunks)

    z = jnp.dot(lr_ref[...], wz_ref[...], precision=lax.Precision.HIGHEST, preferred_element_type=F32) + bz_ref[...]
    la = _log_sigmoid(z) / GLA_TAU
    pos = lax.broadcasted_iota(jnp.int32, (seq_len, 1), 0) % chunk
    b_f = _seg_scan(la[:, :128], chunk, pos, jnp.add, 0.0, False)
    b_b = _seg_scan(la[:, 128:], chunk, pos, jnp.add, 0.0, True)
    q = q_ref[...] * (DK ** -0.5)
    k = k_ref[...]
    qe_ref[...] = jnp.concatenate([q * jnp.exp(b_f), q * jnp.exp(b_b)], axis=1).astype(BF16)
    ke_ref[...] = jnp.concatenate([k * jnp.exp(-b_f), k * jnp.exp(-b_b)], axis=1).astype(BF16)
    cum_ref[...] = jnp.concatenate([b_f, b_b], axis=1)

    ii = lax.broadcasted_iota(jnp.int32, (chunk, chunk), 0)
    jj = lax.broadcasted_iota(jnp.int32, (chunk, chunk), 1)

    def rows(c):
        if isinstance(c, int):
            return pl.ds(c * chunk, chunk)
        return pl.ds(pl.multiple_of(c * chunk, chunk), chunk)

    def advance(c, d):
        lanes = slice(128 * d, 128 * d + 128)
        edge = c * chunk + (chunk - 1 if d == 0 else 0)
        total = cum_ref[pl.ds(edge, 1), :][:, lanes]
        kw = (k_ref[rows(c), :] * jnp.exp(total - cum_ref[rows(c), :][:, lanes])).astype(BF16)
        st = st_ref[d]
        sts_ref[c, :, lanes] = st.astype(BF16)
        st_ref[d] = st * jnp.exp(total) + _dot_tn(v_ref[rows(c), :].astype(BF16), kw)

    if has_init:
        st_ref[0] = s0_ref[0].T
        st_ref[1] = s0_ref[1].T
    else:
        st_ref[...] = jnp.zeros((2, DV_GLA, DK), F32)

    def state_group(g, carry):
        for u in range(unroll):
            c = g * unroll + u
            advance(c, 0)
            advance(n_chunks - 1 - c, 1)
        return carry

    def output_group(g, carry):
        for u in range(unroll):
            c = g * unroll + u
            qe = qe_ref[rows(c), :]
            ke = ke_ref[rows(c), :]
            att = (jnp.where(jj <= ii, _dot_nt(qe[:, :128], ke[:, :128]), 0.0)
                   + jnp.where(jj >= ii, _dot_nt(qe[:, 128:], ke[:, 128:]), 0.0))
            o = _dot(att.astype(BF16), v_ref[rows(c), :].astype(BF16)) + _dot_nt(qe, sts_ref[c])
            gr = gr_ref[rows(c), :]
            out_ref[rows(c), :] = (_rms(o) * (gr * jax.nn.sigmoid(gr))).astype(BF16)
        return carry

    n_groups = n_chunks // unroll
    if n_groups == 1:
        state_group(0, 0)
        output_group(0, 0)
    else:
        lax.fori_loop(0, n_groups, state_group, 0)
        lax.fori_loop(0, n_groups, output_group, 0)
    if want_final:
        s_out_ref[0] = st_ref[0].T
        s_out_ref[1] = st_ref[1].T


def _gla_mixer(proj, lowrank, wz, bz, group, init, layer_idx=0):
    ctx = group == 0
    n_seq, seq_len = (N_CTX_SEQ, CTX_LEN) if ctx else (N_LAT_SEQ, LAT_LEN)
    row_off = 0 if ctx else N_CTX_TOK // LAT_LEN
    has_init, want_final = not ctx, ctx
    n_chunks = seq_len // GLA_CHUNK
    in_specs = [pl.BlockSpec((seq_len, 128), lambda b, h: (row_off + b, h)),
                pl.BlockSpec((seq_len, 128), lambda b, h: (row_off + b, 4 + h)),
                pl.BlockSpec((seq_len, 256), lambda b, h: (row_off + b, 4 + h)),
                pl.BlockSpec((seq_len, 256), lambda b, h: (row_off + b, 8 + h)),
                pl.BlockSpec((seq_len, 128), lambda b, h: (row_off + b, 0)),
                pl.BlockSpec((None, 128, 256), lambda b, h: (h, 0, 0)),
                pl.BlockSpec((None, 1, 256), lambda b, h: (h, 0, 0))]
    args = [proj, proj, proj, proj, lowrank, wz, bz]
    if has_init:
        in_specs.append(pl.BlockSpec((None, None, 2, None, DK, DV_GLA), lambda b, h: (b, layer_idx, 0, h, 0, 0)))
        args.append(init)
    out_specs = [pl.BlockSpec((seq_len, 256), lambda b, h: (b, h))]
    out_shape = [jax.ShapeDtypeStruct((n_seq * seq_len, D), BF16)]
    if want_final:
        out_specs.append(pl.BlockSpec((None, 2, None, DK, DV_GLA), lambda b, h: (b, 0, h, 0, 0)))
        out_shape.append(jax.ShapeDtypeStruct((n_seq, 2, HEADS, DK, DV_GLA), F32))
    scratch = [pltpu.VMEM((seq_len, 256), BF16)] * 2 + [pltpu.VMEM((seq_len, 256), F32),
                                                        pltpu.VMEM((n_chunks, DV_GLA, 2 * DK), BF16),
                                                        pltpu.VMEM((2, DV_GLA, DK), F32)]
    return pl.pallas_call(
        functools.partial(_gla_kernel, seq_len, has_init, want_final),
        grid=(n_seq, HEADS),
        in_specs=in_specs,
        out_specs=out_specs,
        out_shape=out_shape,
        scratch_shapes=scratch,
        compiler_params=_cparams("parallel", "parallel"),
        name="gla_mixer_ctx" if ctx else "gla_mixer_lat",
    )(*args)


def _residual_and_next(y, x_ref, mod_ref, g_post_ref, gate_row, x_out_ref, nxt):
    m = mod_ref[...]
    x = x_ref[...] + m[gate_row:gate_row + 1] * (_rms(y) * g_post_ref[...])
    x_out_ref[...] = x
    if nxt is not None:
        g_next_ref, mod_next_ref, shift_row, xn_ref = nxt
        mn = mod_next_ref[...]
        xn_ref[...] = (_rms(x) * g_next_ref[...] * (1.0 + mn[shift_row + 1:shift_row + 2])
                       + mn[shift_row:shift_row + 1]).astype(BF16)


def _post_kernel(hp_ref, hs_ref, w_ref, x_ref, mod_ref, g_post_ref, g_next_ref, x_out_ref, xn_ref):
    i = pl.program_id(0)

    def run(h_ref):
        y = _dot(h_ref[...], w_ref[...])
        _residual_and_next(y, x_ref, mod_ref, g_post_ref, 2, x_out_ref, (g_next_ref, mod_ref, 3, xn_ref))

    pl.when(i < N_CTX_TOK // ROW_TILE)(lambda: run(hp_ref))
    pl.when(i >= N_CTX_TOK // ROW_TILE)(lambda: run(hs_ref))


def _post_mixer(h_ctx, h_lat, w_out, x, mods, g_post, g_next, layer):
    nc = N_CTX_TOK // ROW_TILE
    row = pl.BlockSpec((ROW_TILE, D), lambda i: (i, 0))
    return pl.pallas_call(
        _post_kernel,
        grid=(N_TOK // ROW_TILE,),
        in_specs=[pl.BlockSpec((ROW_TILE, D), lambda i: (jnp.minimum(i, nc - 1), 0)),
                  pl.BlockSpec((ROW_TILE, D), lambda i: (jnp.maximum(i - nc, 0), 0)),
                  pl.BlockSpec((D, D), lambda i: (0, 0)),
                  row,
                  pl.BlockSpec((None, None, 6, D), lambda i: (layer, _mod_row(i, ROW_TILE), 0, 0)),
                  pl.BlockSpec((None, 1, D), lambda i: (layer, 0, 0)),
                  pl.BlockSpec((None, 1, D), lambda i: (layer, 0, 0))],
        out_specs=[row, row],
        out_shape=[jax.ShapeDtypeStruct((N_TOK, D), F32), jax.ShapeDtypeStruct((N_TOK, D), BF16)],
        compiler_params=_cparams("parallel"),
        name="post_mixer",
    )(h_ctx, h_lat, w_out, x, mods, g_post, g_next)


def _ffn_kernel(last, tile_off, n_tiles, *refs):
    it = iter(refs)
    a_ref, up_ref, dn_ref, v_ref, cw_ref, cb_ref, w_ref, x_ref, mod_ref, g_post_ref = (next(it) for _ in range(10))
    if not last:
        g_next_ref, mod_next_ref = next(it), next(it)
    x_out_ref = next(it)
    xn_ref = None if last else next(it)
    hbuf_ref = next(it)
    i = pl.program_id(0) + tile_off
    n_ctx = N_CTX_TOK // ROW_TILE
    tiles_per_seq = LAT_LEN // ROW_TILE

    def act(conv, sl):
        return (jax.nn.gelu(conv + cb_ref[:, sl]) * v_ref[:, sl]).astype(BF16)

    def ctx_branch():
        pos = lax.broadcasted_iota(jnp.int32, (ROW_TILE, 1), 0)
        for f in range(D_FF // FF_CHUNK):
            sl = slice(f * FF_CHUNK, (f + 1) * FF_CHUNK)
            a = a_ref[:, sl]
            left = jnp.where(pos >= 1, pltpu.roll(a, 1, 0), 0.0)
            right = jnp.where(pos < ROW_TILE - 1, pltpu.roll(a, ROW_TILE - 1, 0), 0.0)
            conv = cw_ref[3:4, sl] * left + cw_ref[4:5, sl] * a + cw_ref[5:6, sl] * right
            hbuf_ref[:, sl] = act(conv, sl)

    def lat_branch():
        t = (i - n_ctx) % tiles_per_seq
        n_rows = ROW_TILE + 2 * GRID_W
        colpos = lax.broadcasted_iota(jnp.int32, (n_rows, 1), 0) % GRID_W
        for f in range(D_FF // FF_CHUNK):
            sl = slice(f * FF_CHUNK, (f + 1) * FF_CHUNK)
            up = jnp.where(t > 0, up_ref[:, sl], 0.0)
            dn = jnp.where(t < tiles_per_seq - 1, dn_ref[:, sl], 0.0)
            buf = jnp.concatenate([up, a_ref[:, sl], dn], axis=0)
            left = jnp.where(colpos >= 1, pltpu.roll(buf, 1, 0), 0.0)
            right = jnp.where(colpos < GRID_W - 1, pltpu.roll(buf, n_rows - 1, 0), 0.0)
            conv = None
            for dr in range(3):
                rs = slice(dr * GRID_W, dr * GRID_W + ROW_TILE)
                term = (cw_ref[3 * dr:3 * dr + 1, sl] * left[rs] + cw_ref[3 * dr + 1:3 * dr + 2, sl] * buf[rs]
                        + cw_ref[3 * dr + 2:3 * dr + 3, sl] * right[rs])
                conv = term if conv is None else conv + term
            hbuf_ref[:, sl] = act(conv, sl)

    if tile_off + n_tiles <= n_ctx:
        ctx_branch()
    elif tile_off >= n_ctx:
        lat_branch()
    else:
        pl.when(i < n_ctx)(ctx_branch)
        pl.when(i >= n_ctx)(lat_branch)
    y = _dot(hbuf_ref[...], w_ref[...])
    nxt = None if last else (g_next_ref, mod_next_ref, 0, xn_ref)
    _residual_and_next(y, x_ref, mod_ref, g_post_ref, 5, x_out_ref, nxt)


def _ffn_tail(av, conv_w, conv_b, w_out, x, mods, g_post, g_next, layer, tile_off=0, n_tiles=N_TOK // ROW_TILE):
    last = layer == DEPTH - 1
    halo_per_tile = ROW_TILE // GRID_W
    n_halo = N_TOK // GRID_W
    row = pl.BlockSpec((ROW_TILE, D), lambda i: (i + tile_off, 0))
    in_specs = [pl.BlockSpec((ROW_TILE, D_FF), lambda i: (i + tile_off, 0)),
                pl.BlockSpec((GRID_W, D_FF), lambda i: (jnp.maximum((i + tile_off) * halo_per_tile - 1, 0), 0)),
                pl.BlockSpec((GRID_W, D_FF),
                             lambda i: (jnp.minimum((i + tile_off + 1) * halo_per_tile, n_halo - 1), 0)),
                pl.BlockSpec((ROW_TILE, D_FF), lambda i: (i + tile_off, 1)),
                pl.BlockSpec((None, 9, D_FF), lambda i: (layer, 0, 0)),
                pl.BlockSpec((None, 1, D_FF), lambda i: (layer, 0, 0)),
                pl.BlockSpec((D_FF, D), lambda i: (0, 0)),
                row,
                pl.BlockSpec((None, None, 6, D), lambda i: (layer, _mod_row(i + tile_off, ROW_TILE), 0, 0)),
                pl.BlockSpec((None, 1, D), lambda i: (layer, 0, 0))]
    args = [av, av, av, av, conv_w.reshape(DEPTH, 9, D_FF), conv_b, w_out, x, mods, g_post]
    out_specs = [pl.BlockSpec((ROW_TILE, D), lambda i: (i, 0))]
    out_shape = [jax.ShapeDtypeStruct((n_tiles * ROW_TILE, D), F32)]
    if not last:
        in_specs += [pl.BlockSpec((None, 1, D), lambda i: (layer + 1, 0, 0)),
                     pl.BlockSpec((None, None, 6, D), lambda i: (layer + 1, _mod_row(i, ROW_TILE), 0, 0))]
        args += [g_next, mods]
        out_specs.append(pl.BlockSpec((ROW_TILE, D), lambda i: (i, 0)))
        out_shape.append(jax.ShapeDtypeStruct((n_tiles * ROW_TILE, D), BF16))
    return pl.pallas_call(
        functools.partial(_ffn_kernel, last, tile_off, n_tiles),
        grid=(n_tiles,),
        in_specs=in_specs,
        out_specs=out_specs,
        out_shape=out_shape,
        scratch_shapes=[pltpu.VMEM((ROW_TILE, D_FF), BF16)],
        compiler_params=_cparams("parallel"),
        name="ffn_tail",
    )(*args)


def _even_weights(w_in, w_out):
    w_main = jnp.concatenate([w_in[:, :2048], w_in[:, 2064:]], axis=1).astype(BF16)
    src = [2048] * 128
    used = [0.0] * 128
    for h in range(HEADS):
        for k, col in enumerate((h, 8 + h, 4 + h, 12 + h)):
            src[(128 - 4 * h) % 128 + k] = 2048 + col
            used[(128 - 4 * h) % 128 + k] = 1.0
    w_gate = (w_in[:, jnp.array(src)] * jnp.array(used, F32)).astype(BF16)
    w_o = w_out.reshape(2, HEADS, 128, D).transpose(1, 0, 2, 3).reshape(D, D).astype(BF16)
    return w_main, w_gate, w_o


def _odd_weights(w_in, w2, b2):
    w_main = w_in[:, :3072].astype(BF16)
    w_lr = jnp.pad(w_in[:, 3072:3072 + 2 * GLA_RANK], ((0, 0), (0, 128 - 2 * GLA_RANK))).astype(BF16)
    w2h = w2.reshape(2, GLA_RANK, HEADS, DK).transpose(2, 0, 1, 3)
    wz = jnp.zeros((HEADS, 128, 256), F32)
    wz = wz.at[:, :GLA_RANK, :DK].set(w2h[:, 0]).at[:, GLA_RANK:2 * GLA_RANK, DK:].set(w2h[:, 1])
    bz = b2.reshape(2, HEADS, DK).transpose(1, 0, 2).reshape(HEADS, 1, 2 * DK)
    return w_main, w_lr, wz, bz


def kernel(x_prompt, x_sample, c, c_ctx, state_mlstm_C, state_mlstm_n, state_mlstm_m, state_ret, state_gla, ada_w, ada_b, norm_mix_pre, norm_mix_post, norm_ffn_pre, norm_ffn_post, w_in_even, w_out_even, mlstm_igate_b, mlstm_fgate_b, ret_decay_logit, w_in_odd, gla_gate_w2, gla_gate_b, w_out_odd, ffn_w_in, ffn_conv_w, ffn_conv_b, ffn_w_out):
    cond = jnp.concatenate([c_ctx[None, :], c, jnp.zeros((8 - 1 - N_LAT_SEQ, D), F32)], axis=0)
    mods = _mod_table(cond, ada_w, ada_b)
    norm_mix_pre, norm_mix_post, norm_ffn_pre, norm_ffn_post = (
        g.reshape(DEPTH, 1, D) for g in (norm_mix_pre, norm_mix_post, norm_ffn_pre, norm_ffn_post))
    ffn_conv_b = ffn_conv_b.reshape(DEPTH, 1, D_FF)

    x, xn = _prenorm(x_prompt.reshape(N_CTX_TOK, D), x_sample.reshape(N_LAT_SEQ * LAT_LEN, D),
                     norm_mix_pre, mods, 0)
    new_c, new_n, new_m, new_r, new_g = [], [], [], [], []
    for layer in range(DEPTH):
        idx = layer // 2
        if layer % 2 == 0:
            w_main, w_gate, w_o = _even_weights(w_in_even[idx], w_out_even[idx])
            proj = _matmul(xn, w_main, MM_ROW_TILE, 1024, "proj_even")
            gates = _matmul(xn, w_gate, 1024, 128, "proj_gates")
            params = jnp.concatenate([mlstm_igate_b[idx].reshape(-1), mlstm_fgate_b[idx].reshape(-1),
                                      ret_decay_logit[idx].reshape(-1)])
            h_ctx, cn, m_fin, r_fin = _even_mixer(proj, gates, params, 0, None)
            init = (state_mlstm_C, state_mlstm_n, state_mlstm_m, state_ret)
            (h_lat,) = _even_mixer(proj, gates, params, 1, init, idx)
            new_c.append(cn[..., :DK])
            new_n.append(cn[..., DK])
            new_m.append(m_fin[:, :, 0:2, 0].transpose(0, 2, 1))
            new_r.append(r_fin)
        else:
            w_main, w_lr, wz, bz = _odd_weights(w_in_odd[idx], gla_gate_w2[idx], gla_gate_b[idx])
            w_o = w_out_odd[idx].astype(BF16)
            proj = _matmul(xn, w_main, MM_ROW_TILE, 1024, "proj_odd")
            lowrank = _matmul(xn, w_lr, 1024, 128, "proj_lowrank")
            h_ctx, s_fin = _gla_mixer(proj, lowrank, wz, bz, 0, None)
            (h_lat,) = _gla_mixer(proj, lowrank, wz, bz, 1, state_gla, idx)
            new_g.append(s_fin)
        x, xn = _post_mixer(h_ctx, h_lat, w_o, x, mods, norm_mix_post, norm_ffn_pre, layer)
        av = _matmul(xn, ffn_w_in[layer].astype(BF16), MM_ROW_TILE, D_FF // 2, "ffn_in")
        tail = functools.partial(_ffn_tail, av, ffn_conv_w, ffn_conv_b, ffn_w_out[layer].astype(BF16), x, mods,
                                 norm_ffn_post, norm_mix_pre, layer)
        if layer < DEPTH - 1:
            x, xn = tail()
        else:
            n_ctx_tiles = N_CTX_TOK // ROW_TILE
            (y_ctx,) = tail(0, n_ctx_tiles)
            (y_lat,) = tail(n_ctx_tiles, N_TOK // ROW_TILE - n_ctx_tiles)

    y_prompt = y_ctx.reshape(N_CTX_SEQ, CTX_LEN, D)
    y_sample = y_lat.reshape(N_LAT_SEQ, LAT_LEN, D)
    return (y_prompt, y_sample, jnp.stack(new_c, axis=1), jnp.stack(new_n, axis=1), jnp.stack(new_m, axis=1),
            jnp.stack(new_r, axis=1), jnp.stack(new_g, axis=1))
```

```python
import functools

import jax
import jax.numpy as jnp
from jax import lax
from jax.experimental import pallas as pl
from jax.experimental.pallas import tpu as pltpu

F32 = jnp.float32
BF16 = jnp.bfloat16

D = 1024
DEPTH = 4
N_CTX_SEQ, CTX_LEN = 16, 256
N_LAT_SEQ, LAT_LEN = 2, 2048
GRID_W = 64
N_CTX_TOK = N_CTX_SEQ * CTX_LEN
N_TOK = N_CTX_TOK + N_LAT_SEQ * LAT_LEN
HEADS = 4
DK = 128
DV_GLA = 256
GLA_RANK = 16
GLA_TAU = 16.0
D_FF = 2816
EPS = 1e-6
NEG = -1e30

ROW_TILE = 256
MM_ROW_TILE = 512
EVEN_CHUNK = 256
EVEN_UNROLL = 2
EVEN_CTX_SEQS = 4
GLA_CHUNK = 64
GLA_UNROLL = 4
FF_CHUNK = 256
FFN_TILE = 512
VMEM_LIMIT = 56 * 1024 * 1024


def _cparams(*sem):
    return pltpu.CompilerParams(dimension_semantics=sem, vmem_limit_bytes=VMEM_LIMIT)


def _mod_row(i, tile):
    n_ctx = N_CTX_TOK // tile
    return jnp.where(i < n_ctx, 0, 1 + (i - n_ctx) // (LAT_LEN // tile))


def _rms(x):
    return x * lax.rsqrt(jnp.mean(x * x, axis=-1, keepdims=True) + EPS)


def _log_sigmoid(x):
    return jnp.minimum(x, 0.0) - jnp.log1p(jnp.exp(-jnp.abs(x)))


def _dot(a, b):
    return jnp.dot(a, b, preferred_element_type=F32)


def _dot_nt(a, b):
    return lax.dot_general(a, b, (((1,), (1,)), ((), ())), preferred_element_type=F32)


def _dot_tn(a, b):
    return lax.dot_general(a, b, (((0,), (0,)), ((), ())), preferred_element_type=F32)


def _seg_scan(x, seg, pos, op, ident, reverse):
    n = x.shape[0]
    s = 1
    while s < seg:
        if reverse:
            x = op(x, jnp.where(pos < seg - s, pltpu.roll(x, n - s, 0), ident))
        else:
            x = op(x, jnp.where(pos >= s, pltpu.roll(x, s, 0), ident))
        s *= 2
    return x


def _mod_kernel(c_ref, w_ref, b_ref, o_ref):
    c = c_ref[...]
    s = (c * jax.nn.sigmoid(c)).astype(BF16)
    o_ref[0] = _dot(s, w_ref[0].astype(BF16)) + b_ref[0]


def _mod_table(cond, ada_w, ada_b):
    tn = 1024
    out = pl.pallas_call(
        _mod_kernel,
        grid=(DEPTH, 6 * D // tn),
        in_specs=[pl.BlockSpec((8, D), lambda l, j: (0, 0)),
                  pl.BlockSpec((1, D, tn), lambda l, j: (l, 0, j)),
                  pl.BlockSpec((1, 1, tn), lambda l, j: (l, 0, j))],
        out_specs=pl.BlockSpec((1, 8, tn), lambda l, j: (l, 0, j)),
        out_shape=jax.ShapeDtypeStruct((DEPTH, 8, 6 * D), F32),
        compiler_params=_cparams("parallel", "parallel"),
        name="mod_table",
    )(cond, ada_w, ada_b.reshape(DEPTH, 1, 6 * D))
    return out.reshape(DEPTH, 8, 6, D)


def _prenorm_kernel(xp_ref, xs_ref, g_ref, mod_ref, x_ref, xn_ref):
    i = pl.program_id(0)

    def run(src_ref):
        x = src_ref[...]
        m = mod_ref[...]
        x_ref[...] = x
        xn_ref[...] = (_rms(x) * g_ref[...] * (1.0 + m[1:2]) + m[0:1]).astype(BF16)

    pl.when(i < N_CTX_TOK // ROW_TILE)(lambda: run(xp_ref))
    pl.when(i >= N_CTX_TOK // ROW_TILE)(lambda: run(xs_ref))


def _prenorm(xp, xs, g, mods, layer):
    nc = N_CTX_TOK // ROW_TILE
    return pl.pallas_call(
        _prenorm_kernel,
        grid=(N_TOK // ROW_TILE,),
        in_specs=[pl.BlockSpec((ROW_TILE, D), lambda i: (jnp.minimum(i, nc - 1), 0)),
                  pl.BlockSpec((ROW_TILE, D), lambda i: (jnp.maximum(i - nc, 0), 0)),
                  pl.BlockSpec((None, 1, D), lambda i: (layer, 0, 0)),
                  pl.BlockSpec((None, None, 6, D), lambda i: (layer, _mod_row(i, ROW_TILE), 0, 0))],
        out_specs=[pl.BlockSpec((ROW_TILE, D), lambda i: (i, 0)),
                   pl.BlockSpec((ROW_TILE, D), lambda i: (i, 0))],
        out_shape=[jax.ShapeDtypeStruct((N_TOK, D), F32), jax.ShapeDtypeStruct((N_TOK, D), BF16)],
        compiler_params=_cparams("parallel"),
        name="prenorm",
    )(xp, xs, g, mods)


def _mm_kernel(x_ref, w_ref, o_ref):
    o_ref[...] = _dot(x_ref[...], w_ref[...]).astype(o_ref.dtype)


def _matmul(x, w, tm, tn, name):
    m, k = x.shape
    n = w.shape[1]
    return pl.pallas_call(
        _mm_kernel,
        grid=(n // tn, m // tm),
        in_specs=[pl.BlockSpec((tm, k), lambda j, i: (i, 0)),
                  pl.BlockSpec((k, tn), lambda j, i: (0, j))],
        out_specs=pl.BlockSpec((tm, tn), lambda j, i: (i, j)),
        out_shape=jax.ShapeDtypeStruct((m, n), F32),
        compiler_params=_cparams("parallel", "parallel"),
        name=name,
    )(x, w)


def _gates_t_kernel(w_ref, x_ref, o_ref):
    o_ref[...] = _dot_nt(w_ref[...], x_ref[...])


def _gates_t(xn, w_gate_t):
    tm = 1024
    return pl.pallas_call(
        _gates_t_kernel,
        grid=(N_TOK // tm,),
        in_specs=[pl.BlockSpec((8 * HEADS, D), lambda i: (0, 0)),
                  pl.BlockSpec((tm, D), lambda i: (i, 0))],
        out_specs=pl.BlockSpec((8 * HEADS, tm), lambda i: (0, i)),
        out_shape=jax.ShapeDtypeStruct((8 * HEADS, N_TOK), F32),
        compiler_params=_cparams("parallel"),
        name="proj_gates",
    )(w_gate_t, xn)


def _even_kernel(seq_len, n_seqs, init_layer, want_final, *refs):
    has_init = init_layer is not None
    layer_idx, n_layers = init_layer if has_init else (0, 1)
    chunk = EVEN_CHUNK
    n_chunks = seq_len // chunk
    total = n_seqs * n_chunks
    use_cross = has_init or n_chunks > 1
    unroll = min(EVEN_UNROLL, n_chunks)
    assert not has_init or n_seqs == 1
    it = iter(refs)
    par_ref = next(it)
    m0_ref = next(it) if has_init else None
    mq_ref, mk_ref, mv_ref, mo_ref, rq_ref, rk_ref, rv_ref, rg_ref, gate_ref = (next(it) for _ in range(9))
    if has_init:
        c0_ref, n0_ref, r0_ref = next(it), next(it), next(it)
    out_ref = next(it)
    if want_final:
        cn_out_ref, m_out_ref, r_out_ref = next(it), next(it), next(it)
    grow_ref, rep_ref, edge_ref, cn_ref, s_ref = (next(it) for _ in range(5))
    if use_cross:
        cns_ref, ss_ref, ms_ref = next(it), next(it), next(it)

    b = pl.program_id(0)
    h = pl.program_id(1)
    ii = lax.broadcasted_iota(jnp.int32, (chunk, chunk), 0)
    jj = lax.broadcasted_iota(jnp.int32, (chunk, chunk), 1)
    pos_col = lax.broadcasted_iota(jnp.int32, (chunk, 1), 0).astype(F32)
    pos_int = lax.broadcasted_iota(jnp.int32, (chunk, 1), 0)
    ones_blk = jnp.ones((chunk, 128), BF16)
    ones_sum = jnp.ones((256, 128), BF16)

    sub = lax.broadcasted_iota(jnp.int32, (8, 1), 0)
    bias = jnp.where(sub == 0, par_ref[h], jnp.where(sub == 1, par_ref[4 + h],
                     jnp.where(sub == 2, par_ref[8 + h], jnp.where(sub == 3, par_ref[12 + h], 0.0))))
    lane = lax.broadcasted_iota(jnp.int32, (1, 128), 1)
    tri_pre = (ii <= jj).astype(BF16)
    tri_suf = (ii >= jj).astype(BF16)

    def lanes(c):
        if isinstance(c, int):
            return slice(c * chunk, (c + 1) * chunk)
        return pl.ds(pl.multiple_of(c * chunk, chunk), chunk)

    def rows(c):
        if isinstance(c, int):
            return pl.ds(c * chunk, chunk)
        return pl.ds(pl.multiple_of(c * chunk, chunk), chunk)

    def prepare_chunk(c):
        pre = gate_ref[:, lanes(c)] + bias
        log_f = pltpu.roll(_log_sigmoid(pre), 6, 0)
        hi = log_f.astype(BF16)
        rest = log_f - hi.astype(F32)
        mid = rest.astype(BF16)
        low = (rest - mid.astype(F32)).astype(BF16)
        parts = jnp.concatenate([hi, mid, low, jnp.zeros_like(hi)], axis=0)
        cum_pre = _dot(parts, tri_pre)
        cum_suf = _dot(parts, tri_suf)
        fcum = jnp.where(sub == 0, cum_pre[0:8] + cum_pre[8:16] + cum_pre[16:24],
                         cum_suf[0:8] + cum_suf[8:16] + cum_suf[16:24])
        gsc = pre - fcum
        packed = jnp.where(sub < 2, gsc, pltpu.roll(fcum, 2, 0))
        grow_ref[:, lanes(c)] = packed
        pcol = packed.T
        g_rep = [jnp.broadcast_to(pcol[:, k:k + 1], (chunk, 128)) for k in range(4)]
        g_rep.append(_seg_scan(g_rep[0], chunk, pos_int, jnp.maximum, NEG, False))
        g_rep.append(_seg_scan(g_rep[1], chunk, pos_int, jnp.maximum, NEG, True))
        rep_ref[rows(c), :] = jnp.concatenate(g_rep, axis=1)
        f_tot = jnp.sum(log_f, axis=1, keepdims=True)
        g_top = jnp.max(gsc, axis=1, keepdims=True)
        edge_ref[c] = jnp.where(lane == 0, f_tot, jnp.where(lane == 1, g_top, 0.0))

    def for_all_chunks(body):
        if total <= 2 * EVEN_UNROLL:
            for c in range(total):
                body(c)
        else:
            def group(g, carry):
                for u in range(EVEN_UNROLL):
                    body(g * EVEN_UNROLL + u)
                return carry
            lax.fori_loop(0, total // EVEN_UNROLL, group, 0)

    for_all_chunks(prepare_chunk)

    lg_f = _log_sigmoid(jnp.full((1, 1), par_ref[16 + h], F32))
    lg_b = _log_sigmoid(jnp.full((1, 1), par_ref[20 + h], F32))
    diff = (ii - jj).astype(F32)
    ret_mask = (jnp.where(diff >= 0, jnp.exp(lg_f * jnp.maximum(diff, 0.0)), 0.0)
                + jnp.where(diff <= 0, jnp.exp(lg_b * jnp.maximum(-diff, 0.0)), 0.0))
    rq_dec = (jnp.exp(lg_f * (pos_col + 1.0)), jnp.exp(lg_b * (chunk - pos_col)))
    rk_dec = (jnp.exp(lg_f * (chunk - 1.0 - pos_col)), jnp.exp(lg_b * pos_col))
    r_chunk = (jnp.exp(lg_f * chunk), jnp.exp(lg_b * chunk))

    def load_kv(c):
        k = mk_ref[rows(c), :] * (DK ** -0.5)
        v = mv_ref[rows(c), :].astype(BF16)
        return k, jnp.concatenate([v, ones_blk], axis=1)

    def ret_kv(c):
        return rk_ref[rows(c), :] * (DK ** -0.5), rv_ref[rows(c), :].astype(BF16)

    def advance(c, d, m, slot):
        if use_cross:
            cns_ref[c, :, 256 * d:256 * d + 256] = cn_ref[slot].astype(BF16)
            ss_ref[c, 128 * d:128 * d + 128, :] = s_ref[slot].astype(BF16)
            ms_ref[c, d:d + 1, :] = jnp.broadcast_to(m, (1, 128))
        edge = edge_ref[c]
        mu = jnp.maximum(m, edge[d:d + 1, 1:2])
        k, vext = load_kv(c)
        kw = (k * jnp.exp(rep_ref[rows(c), 128 * d:128 * d + 128] - mu)).astype(BF16)
        cn_ref[slot] = jnp.exp(m - mu) * cn_ref[slot] + _dot_tn(kw, vext)
        rk, rv = ret_kv(c)
        s_ref[slot] = r_chunk[d] * s_ref[slot] + _dot_tn((rk * rk_dec[d]).astype(BF16), rv)
        return edge[d:d + 1, 0:1] + mu

    if has_init:
        base = (b * n_layers + layer_idx) * 8
        m_init = (jnp.full((1, 1), m0_ref[base + h], F32), jnp.full((1, 1), m0_ref[base + 4 + h], F32))
        for d in range(2):
            cn_ref[d] = jnp.concatenate([c0_ref[d], jnp.broadcast_to(n0_ref[d], (DK, 128))], axis=1)
            s_ref[d] = r0_ref[d]
    else:
        m_init = (jnp.zeros((1, 1), F32), jnp.zeros((1, 1), F32))
        cn_ref[...] = jnp.zeros((2 * n_seqs, DK, 256), F32)
        s_ref[...] = jnp.zeros((2 * n_seqs, DK, DK), F32)

    m_fin = []
    for seq in range(n_seqs):
        def state_group(g, carry, seq=seq):
            m_f, m_b = carry
            for u in range(unroll):
                c = g * unroll + u
                m_f = advance(seq * n_chunks + c, 0, m_f, 2 * seq)
                m_b = advance(seq * n_chunks + n_chunks - 1 - c, 1, m_b, 2 * seq + 1)
            return m_f, m_b

        if n_chunks // unroll > 1:
            m_fin.append(lax.fori_loop(0, n_chunks // unroll, state_group, m_init))
        elif use_cross or want_final:
            m_fin.append(state_group(0, m_init))

    def rms_rep(x):
        sq = x * x
        hi = sq.astype(BF16)
        low = (sq - hi.astype(F32)).astype(BF16)
        ssum = _dot(jnp.concatenate([hi, low], axis=1), ones_sum)
        return x * lax.rsqrt(ssum * (1.0 / 128) + EPS)

    def output_chunk(c):
        q = mq_ref[rows(c), :].astype(BF16)
        k, vext = load_kv(c)
        scores = _dot_nt(q, k.astype(BF16))
        cols = rep_ref[rows(c), :]
        f_f, f_b, gm_f, gm_b = cols[:, 256:384], cols[:, 384:512], cols[:, 512:640], cols[:, 640:768]
        row = grow_ref[:, lanes(c)]
        if use_cross:
            m_f = ms_ref[c, 0:1, :][:, 0:1]
            m_b = ms_ref[c, 1:2, :][:, 0:1]
        else:
            m_f, m_b = m_init
        mu_f = jnp.maximum(m_f, gm_f)
        mu_b = jnp.maximum(m_b, gm_b)
        p_f = jnp.exp(jnp.where(jj <= ii, row[0:1, :], NEG) - jnp.concatenate([mu_f, mu_f], axis=1)) * scores
        p_b = jnp.exp(jnp.where(jj >= ii, row[1:2, :], NEG) - jnp.concatenate([mu_b, mu_b], axis=1)) * scores
        tot = _dot(jnp.concatenate([p_f, p_b], axis=0).astype(BF16), vext)
        num_f, den_f, num_b, den_b = tot[:chunk, :128], tot[:chunk, 128:], tot[chunk:, :128], tot[chunk:, 128:]
        if use_cross:
            cross = _dot(q, cns_ref[c])
            w_f = jnp.exp(m_f - mu_f)
            w_b = jnp.exp(m_b - mu_b)
            num_f, den_f = num_f + w_f * cross[:, 0:128], den_f + w_f * cross[:, 128:256]
            num_b, den_b = num_b + w_b * cross[:, 256:384], den_b + w_b * cross[:, 384:512]
        h_f = num_f / jnp.maximum(jnp.abs(den_f), jnp.exp(-(f_f + mu_f)))
        h_b = num_b / jnp.maximum(jnp.abs(den_b), jnp.exp(-(f_b + mu_b)))
        out_m = rms_rep(h_f + h_b) * jax.nn.sigmoid(mo_ref[rows(c), :])
        rq = rq_ref[rows(c), :]
        rk, rv = ret_kv(c)
        o_r = _dot((_dot_nt(rq.astype(BF16), rk.astype(BF16)) * ret_mask).astype(BF16), rv)
        if use_cross:
            q_in = jnp.concatenate([rq * rq_dec[0], rq * rq_dec[1]], axis=1).astype(BF16)
            o_r = o_r + _dot(q_in, ss_ref[c])
        rg = rg_ref[rows(c), :]
        out_r = rms_rep(o_r) * (rg * jax.nn.sigmoid(rg))
        out_ref[rows(c), :] = jnp.concatenate([out_m, out_r], axis=1).astype(BF16)

    for_all_chunks(output_chunk)

    if want_final:
        for seq in range(n_seqs):
            for d in range(2):
                cn_out_ref[seq, d] = cn_ref[2 * seq + d]
                r_out_ref[seq, d] = s_ref[2 * seq + d]
            m_out_ref[seq] = jnp.concatenate([jnp.broadcast_to(m_fin[seq][0], (1, 128)),
                                              jnp.broadcast_to(m_fin[seq][1], (1, 128)), jnp.zeros((6, 128), F32)], axis=0)


def _even_mixer(proj, gates_t, params, group, init, layer_idx=0):
    ctx = group == 0
    n_seq, seq_len = (N_CTX_SEQ, CTX_LEN) if ctx else (N_LAT_SEQ, LAT_LEN)
    row_off = 0 if ctx else N_CTX_TOK // LAT_LEN
    has_init, want_final = not ctx, ctx
    n_seqs = EVEN_CTX_SEQS if ctx else 1
    rows = n_seqs * seq_len
    n_chunks = seq_len // EVEN_CHUNK
    total = n_seqs * n_chunks
    use_cross = has_init or n_chunks > 1

    def col(k):
        return pl.BlockSpec((rows, 128), lambda b, h: (row_off + b, 4 * k + h))

    smem = pl.BlockSpec(memory_space=pltpu.SMEM)
    in_specs = [smem]
    args = [params]
    if has_init:
        c0, n0, m0, r0 = init
        n_layers = c0.shape[1]
        in_specs.append(smem)
        args.append(m0.reshape(-1))
    in_specs += [col(k) for k in range(8)]
    in_specs.append(pl.BlockSpec((8, rows), lambda b, h: (h, row_off + b)))
    args += [proj] * 8 + [gates_t]
    if has_init:
        in_specs += [pl.BlockSpec((None, None, 2, None, DK, 128), lambda b, h: (b, layer_idx, 0, h, 0, 0)),
                     pl.BlockSpec((None, None, 2, None, DK, 1), lambda b, h: (b, layer_idx, 0, h, 0, 0)),
                     pl.BlockSpec((None, None, 2, None, DK, 128), lambda b, h: (b, layer_idx, 0, h, 0, 0))]
        args += [c0, n0.reshape(n0.shape + (1,)), r0]
    out_specs = [pl.BlockSpec((rows, 256), lambda b, h: (b, h))]
    out_shape = [jax.ShapeDtypeStruct((n_seq * seq_len, D), BF16)]
    if want_final:
        out_specs += [pl.BlockSpec((n_seqs, 2, None, DK, 256), lambda b, h: (b, 0, h, 0, 0)),
                      pl.BlockSpec((n_seqs, None, 8, 128), lambda b, h: (b, h, 0, 0)),
                      pl.BlockSpec((n_seqs, 2, None, DK, 128), lambda b, h: (b, 0, h, 0, 0))]
        out_shape += [jax.ShapeDtypeStruct((n_seq, 2, HEADS, DK, 256), F32),
                      jax.ShapeDtypeStruct((n_seq, HEADS, 8, 128), F32),
                      jax.ShapeDtypeStruct((n_seq, 2, HEADS, DK, 128), F32)]
    scratch = [pltpu.VMEM((8, rows), F32), pltpu.VMEM((rows, 768), F32), pltpu.VMEM((total, 8, 128), F32),
               pltpu.VMEM((2 * n_seqs, DK, 256), F32), pltpu.VMEM((2 * n_seqs, DK, DK), F32)]
    if use_cross:
        scratch += [pltpu.VMEM((total, DK, 512), BF16), pltpu.VMEM((total, 2 * DK, DK), BF16),
                    pltpu.VMEM((total, 8, 128), F32)]
    return pl.pallas_call(
        functools.partial(_even_kernel, seq_len, n_seqs, (layer_idx, n_layers) if has_init else None, want_final),
        grid=(n_seq // n_seqs, HEADS),
        in_specs=in_specs,
        out_specs=out_specs,
        out_shape=out_shape,
        scratch_shapes=scratch,
        compiler_params=_cparams("parallel", "parallel"),
        name="even_mixer_ctx" if ctx else "even_mixer_lat",
    )(*args)


def _gla_kernel(seq_len, has_init, want_final, *refs):
    chunk = GLA_CHUNK
    n_chunks = seq_len // chunk
    it = iter(refs)
    q_ref, k_ref, v_ref, gr_ref, lr_ref, wz_ref, bz_ref = (next(it) for _ in range(7))
    s0_ref = next(it) if has_init else None
    out_ref = next(it)
    s_out_ref = next(it) if want_final else None
    qe_ref, ke_ref, cum_ref, sts_ref, st_ref = (next(it) for _ in range(5))
    unroll = min(GLA_UNROLL, n_chunks)

    z = jnp.dot(lr_ref[...], wz_ref[...], precision=lax.Precision.HIGHEST, preferred_element_type=F32) + bz_ref[...]
    la = _log_sigmoid(z) / GLA_TAU
    pos = lax.broadcasted_iota(jnp.int32, (seq_len, 1), 0) % chunk
    b_f = _seg_scan(la[:, :128], chunk, pos, jnp.add, 0.0, False)
    b_b = _seg_scan(la[:, 128:], chunk, pos, jnp.add, 0.0, True)
    q = q_ref[...] * (DK ** -0.5)
    k = k_ref[...]
    qe_ref[...] = jnp.concatenate([q * jnp.exp(b_f), q * jnp.exp(b_b)], axis=1).astype(BF16)
    ke_ref[...] = jnp.concatenate([k * jnp.exp(-b_f), k * jnp.exp(-b_b)], axis=1).astype(BF16)
    cum_ref[...] = jnp.concatenate([b_f, b_b], axis=1)

    ii = lax.broadcasted_iota(jnp.int32, (chunk, chunk), 0)
    jj = lax.broadcasted_iota(jnp.int32, (chunk, chunk), 1)

    def rows(c):
        if isinstance(c, int):
            return pl.ds(c * chunk, chunk)
        return pl.ds(pl.multiple_of(c * chunk, chunk), chunk)

    def advance(c, d):
        lanes = slice(128 * d, 128 * d + 128)
        edge = c * chunk + (chunk - 1 if d == 0 else 0)
        total = cum_ref[pl.ds(edge, 1), :][:, lanes]
        kw = (k_ref[rows(c), :] * jnp.exp(total - cum_ref[rows(c), :][:, lanes])).astype(BF16)
        st = st_ref[d]
        sts_ref[c, :, lanes] = st.astype(BF16)
        st_ref[d] = st * jnp.exp(total) + _dot_tn(v_ref[rows(c), :].astype(BF16), kw)

    if has_init:
        st_ref[0] = s0_ref[0].T
        st_ref[1] = s0_ref[1].T
    else:
        st_ref[...] = jnp.zeros((2, DV_GLA, DK), F32)

    def state_group(g, carry):
        for u in range(unroll):
            c = g * unroll + u
            advance(c, 0)
            advance(n_chunks - 1 - c, 1)
        return carry

    def output_group(g, carry):
        for u in range(unroll):
            c = g * unroll + u
            qe = qe_ref[rows(c), :]
            ke = ke_ref[rows(c), :]
            att = (jnp.where(jj <= ii, _dot_nt(qe[:, :128], ke[:, :128]), 0.0)
                   + jnp.where(jj >= ii, _dot_nt(qe[:, 128:], ke[:, 128:]), 0.0))
            o = _dot(att.astype(BF16), v_ref[rows(c), :].astype(BF16)) + _dot_nt(qe, sts_ref[c])
            gr = gr_ref[rows(c), :]
            out_ref[rows(c), :] = (_rms(o) * (gr * jax.nn.sigmoid(gr))).astype(BF16)
        return carry

    n_groups = n_chunks // unroll
    if n_groups == 1:
        state_group(0, 0)
        output_group(0, 0)
    else:
        lax.fori_loop(0, n_groups, state_group, 0)
        lax.fori_loop(0, n_groups, output_group, 0)
    if want_final:
        s_out_ref[0] = st_ref[0].T
        s_out_ref[1] = st_ref[1].T


def _gla_mixer(proj, lowrank, wz, bz, group, init, layer_idx=0):
    ctx = group == 0
    n_seq, seq_len = (N_CTX_SEQ, CTX_LEN) if ctx else (N_LAT_SEQ, LAT_LEN)
    row_off = 0 if ctx else N_CTX_TOK // LAT_LEN
    has_init, want_final = not ctx, ctx
    n_chunks = seq_len // GLA_CHUNK
    in_specs = [pl.BlockSpec((seq_len, 128), lambda b, h: (row_off + b, h)),
                pl.BlockSpec((seq_len, 128), lambda b, h: (row_off + b, 4 + h)),
                pl.BlockSpec((seq_len, 256), lambda b, h: (row_off + b, 4 + h)),
                pl.BlockSpec((seq_len, 256), lambda b, h: (row_off + b, 8 + h)),
                pl.BlockSpec((seq_len, 128), lambda b, h: (row_off + b, 0)),
                pl.BlockSpec((None, 128, 256), lambda b, h: (h, 0, 0)),
                pl.BlockSpec((None, 1, 256), lambda b, h: (h, 0, 0))]
    args = [proj, proj, proj, proj, lowrank, wz, bz]
    if has_init:
        in_specs.append(pl.BlockSpec((None, None, 2, None, DK, DV_GLA), lambda b, h: (b, layer_idx, 0, h, 0, 0)))
        args.append(init)
    out_specs = [pl.BlockSpec((seq_len, 256), lambda b, h: (b, h))]
    out_shape = [jax.ShapeDtypeStruct((n_seq * seq_len, D), BF16)]
    if want_final:
        out_specs.append(pl.BlockSpec((None, 2, None, DK, DV_GLA), lambda b, h: (b, 0, h, 0, 0)))
        out_shape.append(jax.ShapeDtypeStruct((n_seq, 2, HEADS, DK, DV_GLA), F32))
    scratch = [pltpu.VMEM((seq_len, 256), BF16)] * 2 + [pltpu.VMEM((seq_len, 256), F32),
                                                        pltpu.VMEM((n_chunks, DV_GLA, 2 * DK), BF16),
                                                        pltpu.VMEM((2, DV_GLA, DK), F32)]
    return pl.pallas_call(
        functools.partial(_gla_kernel, seq_len, has_init, want_final),
        grid=(n_seq, HEADS),
        in_specs=in_specs,
        out_specs=out_specs,
        out_shape=out_shape,
        scratch_shapes=scratch,
        compiler_params=_cparams("parallel", "parallel"),
        name="gla_mixer_ctx" if ctx else "gla_mixer_lat",
    )(*args)


def _residual_and_next(y, x_ref, mod_ref, g_post_ref, gate_row, x_out_ref, nxt):
    m = mod_ref[...]
    x = x_ref[...] + m[gate_row:gate_row + 1] * (_rms(y) * g_post_ref[...])
    x_out_ref[...] = x
    if nxt is not None:
        g_next_ref, mod_next_ref, shift_row, xn_ref = nxt
        mn = mod_next_ref[...]
        xn_ref[...] = (_rms(x) * g_next_ref[...] * (1.0 + mn[shift_row + 1:shift_row + 2])
                       + mn[shift_row:shift_row + 1]).astype(BF16)


def _post_kernel(hp_ref, hs_ref, w_ref, x_ref, mod_ref, g_post_ref, g_next_ref, x_out_ref, xn_ref):
    i = pl.program_id(0)

    def run(h_ref):
        y = _dot(h_ref[...], w_ref[...])
        _residual_and_next(y, x_ref, mod_ref, g_post_ref, 2, x_out_ref, (g_next_ref, mod_ref, 3, xn_ref))

    pl.when(i < N_CTX_TOK // ROW_TILE)(lambda: run(hp_ref))
    pl.when(i >= N_CTX_TOK // ROW_TILE)(lambda: run(hs_ref))


def _post_mixer(h_ctx, h_lat, w_out, x, mods, g_post, g_next, layer):
    nc = N_CTX_TOK // ROW_TILE
    row = pl.BlockSpec((ROW_TILE, D), lambda i: (i, 0))
    return pl.pallas_call(
        _post_kernel,
        grid=(N_TOK // ROW_TILE,),
        in_specs=[pl.BlockSpec((ROW_TILE, D), lambda i: (jnp.minimum(i, nc - 1), 0)),
                  pl.BlockSpec((ROW_TILE, D), lambda i: (jnp.maximum(i - nc, 0), 0)),
                  pl.BlockSpec((D, D), lambda i: (0, 0)),
                  row,
                  pl.BlockSpec((None, None, 6, D), lambda i: (layer, _mod_row(i, ROW_TILE), 0, 0)),
                  pl.BlockSpec((None, 1, D), lambda i: (layer, 0, 0)),
                  pl.BlockSpec((None, 1, D), lambda i: (layer, 0, 0))],
        out_specs=[row, row],
        out_shape=[jax.ShapeDtypeStruct((N_TOK, D), F32), jax.ShapeDtypeStruct((N_TOK, D), BF16)],
        compiler_params=_cparams("parallel"),
        name="post_mixer",
    )(h_ctx, h_lat, w_out, x, mods, g_post, g_next)


def _ffn_kernel(last, tile_off, n_tiles, *refs):
    it = iter(refs)
    xn_ref, up_ref, dn_ref, win_ref, cw_ref, cb_ref, w_ref, x_ref, mod_ref, g_post_ref = (next(it) for _ in range(10))
    if not last:
        g_next_ref, mod_next_ref = next(it), next(it)
    x_out_ref = next(it)
    xn_out_ref = None if last else next(it)
    hbuf_ref = next(it)
    i = pl.program_id(0) + tile_off
    n_ctx = N_CTX_TOK // FFN_TILE
    tiles_per_seq = LAT_LEN // FFN_TILE
    n_chunks = D_FF // FF_CHUNK

    def cols(f):
        return slice(f * FF_CHUNK, (f + 1) * FF_CHUNK), slice(D_FF + f * FF_CHUNK, D_FF + (f + 1) * FF_CHUNK)

    def act(conv, v, sl):
        return (jax.nn.gelu(conv + cb_ref[:, sl]) * v).astype(BF16)

    def ctx_branch():
        xn = xn_ref[...]
        pos = lax.broadcasted_iota(jnp.int32, (FFN_TILE, 1), 0) % CTX_LEN
        for f in range(n_chunks):
            sl, sv = cols(f)
            a = _dot(xn, win_ref[:, sl])
            left = jnp.where(pos >= 1, pltpu.roll(a, 1, 0), 0.0)
            right = jnp.where(pos < CTX_LEN - 1, pltpu.roll(a, FFN_TILE - 1, 0), 0.0)
            conv = cw_ref[3:4, sl] * left + cw_ref[4:5, sl] * a + cw_ref[5:6, sl] * right
            hbuf_ref[:, sl] = act(conv, _dot(xn, win_ref[:, sv]), sl)

    def lat_branch():
        t = (i - n_ctx) % tiles_per_seq
        n_rows = FFN_TILE + 2 * GRID_W
        xn = xn_ref[...]
        up = up_ref[...]
        dn = dn_ref[...]
        xe = jnp.concatenate([jnp.where(t > 0, up, jnp.zeros_like(up)), xn,
                              jnp.where(t < tiles_per_seq - 1, dn, jnp.zeros_like(dn))], axis=0)
        colpos = lax.broadcasted_iota(jnp.int32, (n_rows, 1), 0) % GRID_W
        for f in range(n_chunks):
            sl, sv = cols(f)
            buf = _dot(xe, win_ref[:, sl])
            left = jnp.where(colpos >= 1, pltpu.roll(buf, 1, 0), 0.0)
            right = jnp.where(colpos < GRID_W - 1, pltpu.roll(buf, n_rows - 1, 0), 0.0)
            conv = None
            for dr in range(3):
                rs = slice(dr * GRID_W, dr * GRID_W + FFN_TILE)
                term = (cw_ref[3 * dr:3 * dr + 1, sl] * left[rs] + cw_ref[3 * dr + 1:3 * dr + 2, sl] * buf[rs]
                        + cw_ref[3 * dr + 2:3 * dr + 3, sl] * right[rs])
                conv = term if conv is None else conv + term
            hbuf_ref[:, sl] = act(conv, _dot(xn, win_ref[:, sv]), sl)

    if tile_off + n_tiles <= n_ctx:
        ctx_branch()
    elif tile_off >= n_ctx:
        lat_branch()
    else:
        pl.when(i < n_ctx)(ctx_branch)
        pl.when(i >= n_ctx)(lat_branch)
    y = _dot(hbuf_ref[...], w_ref[...])
    nxt = None if last else (g_next_ref, mod_next_ref, 0, xn_out_ref)
    _residual_and_next(y, x_ref, mod_ref, g_post_ref, 5, x_out_ref, nxt)


def _ffn(xn, w_in, conv_w, conv_b, w_out, x, mods, g_post, g_next, layer, tile_off=0, n_tiles=N_TOK // FFN_TILE):
    last = layer == DEPTH - 1
    halo_per_tile = FFN_TILE // GRID_W
    n_halo = N_TOK // GRID_W
    row = pl.BlockSpec((FFN_TILE, D), lambda i: (i + tile_off, 0))
    resident = dict(pipeline_mode=pl.Buffered(1))
    in_specs = [row,
                pl.BlockSpec((GRID_W, D), lambda i: (jnp.maximum((i + tile_off) * halo_per_tile - 1, 0), 0)),
                pl.BlockSpec((GRID_W, D), lambda i: (jnp.minimum((i + tile_off + 1) * halo_per_tile, n_halo - 1), 0)),
                pl.BlockSpec((D, 2 * D_FF), lambda i: (0, 0), **resident),
                pl.BlockSpec((None, 9, D_FF), lambda i: (layer, 0, 0)),
                pl.BlockSpec((None, 1, D_FF), lambda i: (layer, 0, 0)),
                pl.BlockSpec((D_FF, D), lambda i: (0, 0), **resident),
                row,
                pl.BlockSpec((None, None, 6, D), lambda i: (layer, _mod_row(i + tile_off, FFN_TILE), 0, 0)),
                pl.BlockSpec((None, 1, D), lambda i: (layer, 0, 0))]
    args = [xn, xn, xn, w_in, conv_w.reshape(DEPTH, 9, D_FF), conv_b, w_out, x, mods, g_post]
    out_specs = [pl.BlockSpec((FFN_TILE, D), lambda i: (i, 0))]
    out_shape = [jax.ShapeDtypeStruct((n_tiles * FFN_TILE, D), F32)]
    if not last:
        in_specs += [pl.BlockSpec((None, 1, D), lambda i: (layer + 1, 0, 0)),
                     pl.BlockSpec((None, None, 6, D), lambda i: (layer + 1, _mod_row(i, FFN_TILE), 0, 0))]
        args += [g_next, mods]
        out_specs.append(pl.BlockSpec((FFN_TILE, D), lambda i: (i, 0)))
        out_shape.append(jax.ShapeDtypeStruct((n_tiles * FFN_TILE, D), BF16))
    return pl.pallas_call(
        functools.partial(_ffn_kernel, last, tile_off, n_tiles),
        grid=(n_tiles,),
        in_specs=in_specs,
        out_specs=out_specs,
        out_shape=out_shape,
        scratch_shapes=[pltpu.VMEM((FFN_TILE, D_FF), BF16)],
        compiler_params=_cparams("parallel"),
        name="conv_ffn",
    )(*args)


def _even_weights(w_in, w_out):
    w_main = jnp.concatenate([w_in[:, :2048], w_in[:, 2064:]], axis=1).astype(BF16)
    src = [2048] * (8 * HEADS)
    used = [0.0] * (8 * HEADS)
    for h in range(HEADS):
        for k, col in enumerate((h, 8 + h, 4 + h, 12 + h)):
            src[8 * h + k] = 2048 + col
            used[8 * h + k] = 1.0
    w_gate = (w_in[:, jnp.array(src)] * jnp.array(used, F32)).T.astype(BF16)
    w_o = w_out.reshape(2, HEADS, 128, D).transpose(1, 0, 2, 3).reshape(D, D).astype(BF16)
    return w_main, w_gate, w_o


def _odd_weights(w_in, w2, b2):
    w_main = w_in[:, :3072].astype(BF16)
    w_lr = jnp.pad(w_in[:, 3072:3072 + 2 * GLA_RANK], ((0, 0), (0, 128 - 2 * GLA_RANK))).astype(BF16)
    w2h = w2.reshape(2, GLA_RANK, HEADS, DK).transpose(2, 0, 1, 3)
    wz = jnp.zeros((HEADS, 128, 256), F32)
    wz = wz.at[:, :GLA_RANK, :DK].set(w2h[:, 0]).at[:, GLA_RANK:2 * GLA_RANK, DK:].set(w2h[:, 1])
    bz = b2.reshape(2, HEADS, DK).transpose(1, 0, 2).reshape(HEADS, 1, 2 * DK)
    return w_main, w_lr, wz, bz


def kernel(x_prompt, x_sample, c, c_ctx, state_mlstm_C, state_mlstm_n, state_mlstm_m, state_ret, state_gla, ada_w, ada_b, norm_mix_pre, norm_mix_post, norm_ffn_pre, norm_ffn_post, w_in_even, w_out_even, mlstm_igate_b, mlstm_fgate_b, ret_decay_logit, w_in_odd, gla_gate_w2, gla_gate_b, w_out_odd, ffn_w_in, ffn_conv_w, ffn_conv_b, ffn_w_out):
    cond = jnp.concatenate([c_ctx[None, :], c, jnp.zeros((8 - 1 - N_LAT_SEQ, D), F32)], axis=0)
    mods = _mod_table(cond, ada_w, ada_b)
    norm_mix_pre, norm_mix_post, norm_ffn_pre, norm_ffn_post = (
        g.reshape(DEPTH, 1, D) for g in (norm_mix_pre, norm_mix_post, norm_ffn_pre, norm_ffn_post))
    ffn_conv_b = ffn_conv_b.reshape(DEPTH, 1, D_FF)

    x, xn = _prenorm(x_prompt.reshape(N_CTX_TOK, D), x_sample.reshape(N_LAT_SEQ * LAT_LEN, D),
                     norm_mix_pre, mods, 0)
    new_c, new_n, new_m, new_r, new_g = [], [], [], [], []
    for layer in range(DEPTH):
        idx = layer // 2
        if layer % 2 == 0:
            w_main, w_gate, w_o = _even_weights(w_in_even[idx], w_out_even[idx])
            proj = _matmul(xn, w_main, MM_ROW_TILE, 1024, "proj_even")
            gates = _gates_t(xn, w_gate)
            params = jnp.concatenate([mlstm_igate_b[idx].reshape(-1), mlstm_fgate_b[idx].reshape(-1),
                                      ret_decay_logit[idx].reshape(-1)])
            h_ctx, cn, m_fin, r_fin = _even_mixer(proj, gates, params, 0, None)
            init = (state_mlstm_C, state_mlstm_n, state_mlstm_m, state_ret)
            (h_lat,) = _even_mixer(proj, gates, params, 1, init, idx)
            new_c.append(cn[..., :DK])
            new_n.append(cn[..., DK])
            new_m.append(m_fin[:, :, 0:2, 0].transpose(0, 2, 1))
            new_r.append(r_fin)
        else:
            w_main, w_lr, wz, bz = _odd_weights(w_in_odd[idx], gla_gate_w2[idx], gla_gate_b[idx])
            w_o = w_out_odd[idx].astype(BF16)
            proj = _matmul(xn, w_main, MM_ROW_TILE, 1024, "proj_odd")
            lowrank = _matmul(xn, w_lr, 1024, 128, "proj_lowrank")
            h_ctx, s_fin = _gla_mixer(proj, lowrank, wz, bz, 0, None)
            (h_lat,) = _gla_mixer(proj, lowrank, wz, bz, 1, state_gla, idx)
            new_g.append(s_fin)
        x, xn = _post_mixer(h_ctx, h_lat, w_o, x, mods, norm_mix_post, norm_ffn_pre, layer)
        ffn = functools.partial(_ffn, xn, ffn_w_in[layer].astype(BF16), ffn_conv_w, ffn_conv_b,
                                ffn_w_out[layer].astype(BF16), x, mods, norm_ffn_post, norm_mix_pre, layer)
        if layer < DEPTH - 1:
            x, xn = ffn()
        else:
            n_ctx_tiles = N_CTX_TOK // FFN_TILE
            (y_ctx,) = ffn(0, n_ctx_tiles)
            (y_lat,) = ffn(n_ctx_tiles, N_TOK // FFN_TILE - n_ctx_tiles)

    y_prompt = y_ctx.reshape(N_CTX_SEQ, CTX_LEN, D)
    y_sample = y_lat.reshape(N_LAT_SEQ, LAT_LEN, D)
    return (y_prompt, y_sample, jnp.stack(new_c, axis=1), jnp.stack(new_n, axis=1), jnp.stack(new_m, axis=1),
            jnp.stack(new_r, axis=1), jnp.stack(new_g, axis=1))
```

```python
import functools

import jax
import jax.numpy as jnp
from jax import lax
from jax.experimental import pallas as pl
from jax.experimental.pallas import tpu as pltpu

F32 = jnp.float32
BF16 = jnp.bfloat16

D = 1024
DEPTH = 4
N_CTX_SEQ, CTX_LEN = 16, 256
N_LAT_SEQ, LAT_LEN = 2, 2048
GRID_W = 64
N_CTX_TOK = N_CTX_SEQ * CTX_LEN
N_TOK = N_CTX_TOK + N_LAT_SEQ * LAT_LEN
HEADS = 4
DK = 128
DV_GLA = 256
GLA_RANK = 16
GLA_TAU = 16.0
D_FF = 2816
EPS = 1e-6
NEG = -1e30

ROW_TILE = 256
MM_ROW_TILE = 512
EVEN_CHUNK = 256
EVEN_UNROLL = 2
EVEN_CTX_SEQS = 4
GLA_CHUNK = 64
GLA_UNROLL = 4
FF_CHUNK = 256
FFN_TILE = 512
VMEM_LIMIT = 56 * 1024 * 1024


def _cparams(*sem):
    return pltpu.CompilerParams(dimension_semantics=sem, vmem_limit_bytes=VMEM_LIMIT)


def _mod_row(i, tile):
    n_ctx = N_CTX_TOK // tile
    return jnp.where(i < n_ctx, 0, 1 + (i - n_ctx) // (LAT_LEN // tile))


def _rms(x):
    return x * lax.rsqrt(jnp.mean(x * x, axis=-1, keepdims=True) + EPS)


def _log_sigmoid(x):
    return jnp.minimum(x, 0.0) - jnp.log1p(jnp.exp(-jnp.abs(x)))


def _dot(a, b):
    return jnp.dot(a, b, preferred_element_type=F32)


def _dot_nt(a, b):
    return lax.dot_general(a, b, (((1,), (1,)), ((), ())), preferred_element_type=F32)


def _dot_tn(a, b):
    return lax.dot_general(a, b, (((0,), (0,)), ((), ())), preferred_element_type=F32)


def _seg_scan(x, seg, pos, op, ident, reverse):
    n = x.shape[0]
    s = 1
    while s < seg:
        if reverse:
            x = op(x, jnp.where(pos < seg - s, pltpu.roll(x, n - s, 0), ident))
        else:
            x = op(x, jnp.where(pos >= s, pltpu.roll(x, s, 0), ident))
        s *= 2
    return x


def _mod_kernel(c_ref, w_ref, b_ref, o_ref):
    c = c_ref[...]
    s = (c * jax.nn.sigmoid(c)).astype(BF16)
    o_ref[0] = _dot(s, w_ref[0].astype(BF16)) + b_ref[0]


def _mod_table(cond, ada_w, ada_b):
    tn = 1024
    out = pl.pallas_call(
        _mod_kernel,
        grid=(DEPTH, 6 * D // tn),
        in_specs=[pl.BlockSpec((8, D), lambda l, j: (0, 0)),
                  pl.BlockSpec((1, D, tn), lambda l, j: (l, 0, j)),
                  pl.BlockSpec((1, 1, tn), lambda l, j: (l, 0, j))],
        out_specs=pl.BlockSpec((1, 8, tn), lambda l, j: (l, 0, j)),
        out_shape=jax.ShapeDtypeStruct((DEPTH, 8, 6 * D), F32),
        compiler_params=_cparams("parallel", "parallel"),
        name="mod_table",
    )(cond, ada_w, ada_b.reshape(DEPTH, 1, 6 * D))
    return out.reshape(DEPTH, 8, 6, D)


def _prenorm_kernel(xp_ref, xs_ref, g_ref, mod_ref, x_ref, xn_ref):
    i = pl.program_id(0)

    def run(src_ref):
        x = src_ref[...]
        m = mod_ref[...]
        x_ref[...] = x
        xn_ref[...] = (_rms(x) * g_ref[...] * (1.0 + m[1:2]) + m[0:1]).astype(BF16)

    pl.when(i < N_CTX_TOK // ROW_TILE)(lambda: run(xp_ref))
    pl.when(i >= N_CTX_TOK // ROW_TILE)(lambda: run(xs_ref))


def _prenorm(xp, xs, g, mods, layer):
    nc = N_CTX_TOK // ROW_TILE
    return pl.pallas_call(
        _prenorm_kernel,
        grid=(N_TOK // ROW_TILE,),
        in_specs=[pl.BlockSpec((ROW_TILE, D), lambda i: (jnp.minimum(i, nc - 1), 0)),
                  pl.BlockSpec((ROW_TILE, D), lambda i: (jnp.maximum(i - nc, 0), 0)),
                  pl.BlockSpec((None, 1, D), lambda i: (layer, 0, 0)),
                  pl.BlockSpec((None, None, 6, D), lambda i: (layer, _mod_row(i, ROW_TILE), 0, 0))],
        out_specs=[pl.BlockSpec((ROW_TILE, D), lambda i: (i, 0)),
                   pl.BlockSpec((ROW_TILE, D), lambda i: (i, 0))],
        out_shape=[jax.ShapeDtypeStruct((N_TOK, D), F32), jax.ShapeDtypeStruct((N_TOK, D), BF16)],
        compiler_params=_cparams("parallel"),
        name="prenorm",
    )(xp, xs, g, mods)


def _mm_kernel(x_ref, w_ref, o_ref):
    o_ref[...] = _dot(x_ref[...], w_ref[...]).astype(o_ref.dtype)


def _matmul(x, w, tm, tn, name, out_dtype=F32):
    m, k = x.shape
    n = w.shape[1]
    return pl.pallas_call(
        _mm_kernel,
        grid=(n // tn, m // tm),
        in_specs=[pl.BlockSpec((tm, k), lambda j, i: (i, 0)),
                  pl.BlockSpec((k, tn), lambda j, i: (0, j))],
        out_specs=pl.BlockSpec((tm, tn), lambda j, i: (i, j)),
        out_shape=jax.ShapeDtypeStruct((m, n), out_dtype),
        compiler_params=_cparams("parallel", "parallel"),
        name=name,
    )(x, w)


def _gates_t_kernel(w_ref, x_ref, o_ref):
    o_ref[...] = _dot_nt(w_ref[...], x_ref[...])


def _gates_t(xn, w_gate_t):
    tm = 1024
    return pl.pallas_call(
        _gates_t_kernel,
        grid=(N_TOK // tm,),
        in_specs=[pl.BlockSpec((8 * HEADS, D), lambda i: (0, 0)),
                  pl.BlockSpec((tm, D), lambda i: (i, 0))],
        out_specs=pl.BlockSpec((8 * HEADS, tm), lambda i: (0, i)),
        out_shape=jax.ShapeDtypeStruct((8 * HEADS, N_TOK), F32),
        compiler_params=_cparams("parallel"),
        name="proj_gates",
    )(w_gate_t, xn)


def _even_kernel(seq_len, n_seqs, init_layer, want_final, n_carried, *refs):
    has_init = init_layer is not None
    layer_idx, n_layers = init_layer if has_init else (0, 1)
    chunk = EVEN_CHUNK
    n_chunks = seq_len // chunk
    total = n_seqs * n_chunks
    use_cross = has_init or n_chunks > 1
    unroll = min(EVEN_UNROLL, n_chunks)
    assert not has_init or n_seqs == 1
    it = iter(refs)
    par_ref = next(it)
    m0_ref = next(it) if has_init else None
    mq_ref, mk_ref, mv_ref, mo_ref, rq_ref, rk_ref, rv_ref, rg_ref, gate_ref = (next(it) for _ in range(9))
    if has_init:
        c0_ref, n0_ref, r0_ref = next(it), next(it), next(it)
    for _ in range(n_carried):
        next(it)
    out_ref = next(it)
    if want_final:
        c_out_ref, n_out_ref, m_out_ref, r_out_ref = next(it), next(it), next(it), next(it)
    grow_ref, rep_ref, edge_ref, cn_ref, s_ref = (next(it) for _ in range(5))
    if use_cross:
        cns_ref, ss_ref, ms_ref = next(it), next(it), next(it)

    b = pl.program_id(0)
    h = pl.program_id(1)
    ii = lax.broadcasted_iota(jnp.int32, (chunk, chunk), 0)
    jj = lax.broadcasted_iota(jnp.int32, (chunk, chunk), 1)
    pos_col = lax.broadcasted_iota(jnp.int32, (chunk, 1), 0).astype(F32)
    pos_int = lax.broadcasted_iota(jnp.int32, (chunk, 1), 0)
    ones_blk = jnp.ones((chunk, 128), BF16)
    ones_sum = jnp.ones((256, 128), BF16)

    sub = lax.broadcasted_iota(jnp.int32, (8, 1), 0)
    bias = jnp.where(sub == 0, par_ref[h], jnp.where(sub == 1, par_ref[4 + h],
                     jnp.where(sub == 2, par_ref[8 + h], jnp.where(sub == 3, par_ref[12 + h], 0.0))))
    lane = lax.broadcasted_iota(jnp.int32, (1, 128), 1)
    tri_pre = (ii <= jj).astype(BF16)
    tri_suf = (ii >= jj).astype(BF16)

    def lanes(c):
        if isinstance(c, int):
            return slice(c * chunk, (c + 1) * chunk)
        return pl.ds(pl.multiple_of(c * chunk, chunk), chunk)

    def rows(c):
        if isinstance(c, int):
            return pl.ds(c * chunk, chunk)
        return pl.ds(pl.multiple_of(c * chunk, chunk), chunk)

    def prepare_chunk(c):
        pre = gate_ref[:, lanes(c)] + bias
        log_f = pltpu.roll(_log_sigmoid(pre), 6, 0)
        hi = log_f.astype(BF16)
        rest = log_f - hi.astype(F32)
        mid = rest.astype(BF16)
        low = (rest - mid.astype(F32)).astype(BF16)
        parts = jnp.concatenate([hi, mid, low, jnp.zeros_like(hi)], axis=0)
        cum_pre = _dot(parts, tri_pre)
        cum_suf = _dot(parts, tri_suf)
        fcum = jnp.where(sub == 0, cum_pre[0:8] + cum_pre[8:16] + cum_pre[16:24],
                         cum_suf[0:8] + cum_suf[8:16] + cum_suf[16:24])
        gsc = pre - fcum
        packed = jnp.where(sub < 2, gsc, pltpu.roll(fcum, 2, 0))
        grow_ref[:, lanes(c)] = packed
        pcol = packed.T
        g_rep = [jnp.broadcast_to(pcol[:, k:k + 1], (chunk, 128)) for k in range(4)]
        g_rep.append(_seg_scan(g_rep[0], chunk, pos_int, jnp.maximum, NEG, False))
        g_rep.append(_seg_scan(g_rep[1], chunk, pos_int, jnp.maximum, NEG, True))
        rep_ref[rows(c), :] = jnp.concatenate(g_rep, axis=1)
        f_tot = jnp.sum(log_f, axis=1, keepdims=True)
        g_top = jnp.max(gsc, axis=1, keepdims=True)
        edge_ref[c] = jnp.where(lane == 0, f_tot, jnp.where(lane == 1, g_top, 0.0))

    def for_all_chunks(body):
        if total <= 2 * EVEN_UNROLL:
            for c in range(total):
                body(c)
        else:
            def group(g, carry):
                for u in range(EVEN_UNROLL):
                    body(g * EVEN_UNROLL + u)
                return carry
            lax.fori_loop(0, total // EVEN_UNROLL, group, 0)

    for_all_chunks(prepare_chunk)

    k_scale = DK ** -0.5
    lg_f = _log_sigmoid(jnp.full((1, 1), par_ref[16 + h], F32))
    lg_b = _log_sigmoid(jnp.full((1, 1), par_ref[20 + h], F32))
    diff = (ii - jj).astype(F32)
    ret_mask = k_scale * (jnp.where(diff >= 0, jnp.exp(lg_f * jnp.maximum(diff, 0.0)), 0.0)
                          + jnp.where(diff <= 0, jnp.exp(lg_b * jnp.maximum(-diff, 0.0)), 0.0))
    rq_dec = (jnp.exp(lg_f * (pos_col + 1.0)), jnp.exp(lg_b * (chunk - pos_col)))
    rk_dec = (k_scale * jnp.exp(lg_f * (chunk - 1.0 - pos_col)), k_scale * jnp.exp(lg_b * pos_col))
    r_chunk = (jnp.exp(lg_f * chunk), jnp.exp(lg_b * chunk))

    def load_kv(c):
        return mk_ref[rows(c), :], jnp.concatenate([mv_ref[rows(c), :], ones_blk], axis=1)

    def ret_kv(c):
        return rk_ref[rows(c), :], rv_ref[rows(c), :]

    def advance(c, d, m, slot):
        if use_cross:
            cns_ref[c, :, 256 * d:256 * d + 256] = cn_ref[slot].astype(BF16)
            ss_ref[c, 128 * d:128 * d + 128, :] = s_ref[slot].astype(BF16)
            ms_ref[c, d:d + 1, :] = jnp.broadcast_to(m, (1, 128))
        edge = edge_ref[c]
        mu = jnp.maximum(m, edge[d:d + 1, 1:2])
        k, vext = load_kv(c)
        kw = (k.astype(F32) * (k_scale * jnp.exp(rep_ref[rows(c), 128 * d:128 * d + 128] - mu))).astype(BF16)
        cn_ref[slot] = jnp.exp(m - mu) * cn_ref[slot] + _dot_tn(kw, vext)
        rk, rv = ret_kv(c)
        s_ref[slot] = r_chunk[d] * s_ref[slot] + _dot_tn((rk.astype(F32) * rk_dec[d]).astype(BF16), rv)
        return edge[d:d + 1, 0:1] + mu

    if has_init:
        base = (b * n_layers + layer_idx) * 8
        m_init = (jnp.full((1, 1), m0_ref[base + h], F32), jnp.full((1, 1), m0_ref[base + 4 + h], F32))
        for d in range(2):
            cn_ref[d] = jnp.concatenate([c0_ref[d], jnp.broadcast_to(n0_ref[d], (DK, 128))], axis=1)
            s_ref[d] = r0_ref[d]
    else:
        m_init = (jnp.zeros((1, 1), F32), jnp.zeros((1, 1), F32))
        cn_ref[...] = jnp.zeros((2 * n_seqs, DK, 256), F32)
        s_ref[...] = jnp.zeros((2 * n_seqs, DK, DK), F32)

    m_fin = []
    for seq in range(n_seqs):
        def state_group(g, carry, seq=seq):
            m_f, m_b = carry
            for u in range(unroll):
                c = g * unroll + u
                m_f = advance(seq * n_chunks + c, 0, m_f, 2 * seq)
                m_b = advance(seq * n_chunks + n_chunks - 1 - c, 1, m_b, 2 * seq + 1)
            return m_f, m_b

        if n_chunks // unroll > 1:
            m_fin.append(lax.fori_loop(0, n_chunks // unroll, state_group, m_init))
        elif use_cross or want_final:
            m_fin.append(state_group(0, m_init))

    def rms_rep(x):
        sq = x * x
        hi = sq.astype(BF16)
        low = (sq - hi.astype(F32)).astype(BF16)
        ssum = _dot(jnp.concatenate([hi, low], axis=1), ones_sum)
        return x * lax.rsqrt(ssum * (1.0 / 128) + EPS)

    def output_chunk(c):
        q = mq_ref[rows(c), :]
        k, vext = load_kv(c)
        scores = _dot_nt(q, k) * k_scale
        cols = rep_ref[rows(c), :]
        f_f, f_b, gm_f, gm_b = cols[:, 256:384], cols[:, 384:512], cols[:, 512:640], cols[:, 640:768]
        row = grow_ref[:, lanes(c)]
        if use_cross:
            m_f = ms_ref[c, 0:1, :][:, 0:1]
            m_b = ms_ref[c, 1:2, :][:, 0:1]
        else:
            m_f, m_b = m_init
        mu_f = jnp.maximum(m_f, gm_f)
        mu_b = jnp.maximum(m_b, gm_b)
        p_f = jnp.exp(jnp.where(jj <= ii, row[0:1, :], NEG) - jnp.concatenate([mu_f, mu_f], axis=1)) * scores
        p_b = jnp.exp(jnp.where(jj >= ii, row[1:2, :], NEG) - jnp.concatenate([mu_b, mu_b], axis=1)) * scores
        tot = _dot(jnp.concatenate([p_f, p_b], axis=0).astype(BF16), vext)
        num_f, den_f, num_b, den_b = tot[:chunk, :128], tot[:chunk, 128:], tot[chunk:, :128], tot[chunk:, 128:]
        if use_cross:
            cross = _dot(q, cns_ref[c])
            w_f = jnp.exp(m_f - mu_f)
            w_b = jnp.exp(m_b - mu_b)
            num_f, den_f = num_f + w_f * cross[:, 0:128], den_f + w_f * cross[:, 128:256]
            num_b, den_b = num_b + w_b * cross[:, 256:384], den_b + w_b * cross[:, 384:512]
        h_f = num_f / jnp.maximum(jnp.abs(den_f), jnp.exp(-(f_f + mu_f)))
        h_b = num_b / jnp.maximum(jnp.abs(den_b), jnp.exp(-(f_b + mu_b)))
        out_m = rms_rep(h_f + h_b) * jax.nn.sigmoid(mo_ref[rows(c), :].astype(F32))
        rq = rq_ref[rows(c), :]
        rk, rv = ret_kv(c)
        o_r = _dot((_dot_nt(rq, rk) * ret_mask).astype(BF16), rv)
        if use_cross:
            rq32 = rq.astype(F32)
            q_in = jnp.concatenate([rq32 * rq_dec[0], rq32 * rq_dec[1]], axis=1).astype(BF16)
            o_r = o_r + _dot(q_in, ss_ref[c])
        rg = rg_ref[rows(c), :].astype(F32)
        out_r = rms_rep(o_r) * (rg * jax.nn.sigmoid(rg))
        out_ref[rows(c), :] = jnp.concatenate([out_m, out_r], axis=1).astype(BF16)

    for_all_chunks(output_chunk)

    if want_final:
        for seq in range(n_seqs):
            for d in range(2):
                cn = cn_ref[2 * seq + d]
                c_out_ref[seq, d] = cn[:, :DK]
                n_out_ref[seq, d] = cn[:, DK:].T[0:8, :]
                r_out_ref[seq, d] = s_ref[2 * seq + d]
            m_out_ref[seq] = jnp.concatenate([jnp.broadcast_to(m_fin[seq][0], (1, 128)),
                                              jnp.broadcast_to(m_fin[seq][1], (1, 128)), jnp.zeros((6, 128), F32)], axis=0)


def _even_mixer(proj, gates_t, params, group, init, layer_idx=0, carried=None):
    ctx = group == 0
    n_seq, seq_len = (N_CTX_SEQ, CTX_LEN) if ctx else (N_LAT_SEQ, LAT_LEN)
    row_off = 0 if ctx else N_CTX_TOK // LAT_LEN
    has_init, want_final = not ctx, ctx
    n_seqs = EVEN_CTX_SEQS if ctx else 1
    rows = n_seqs * seq_len
    n_chunks = seq_len // EVEN_CHUNK
    total = n_seqs * n_chunks
    use_cross = has_init or n_chunks > 1

    def col(k):
        return pl.BlockSpec((rows, 128), lambda b, h: (row_off + b, 4 * k + h))

    smem = pl.BlockSpec(memory_space=pltpu.SMEM)
    in_specs = [smem]
    args = [params]
    if has_init:
        c0, n0, m0, r0 = init
        n_layers = c0.shape[1]
        in_specs.append(smem)
        args.append(m0.reshape(-1))
    in_specs += [col(k) for k in range(8)]
    in_specs.append(pl.BlockSpec((8, rows), lambda b, h: (h, row_off + b)))
    args += [proj] * 8 + [gates_t]
    if has_init:
        in_specs += [pl.BlockSpec((None, None, 2, None, DK, 128), lambda b, h: (b, layer_idx, 0, h, 0, 0)),
                     pl.BlockSpec((None, None, 2, None, DK, 1), lambda b, h: (b, layer_idx, 0, h, 0, 0)),
                     pl.BlockSpec((None, None, 2, None, DK, 128), lambda b, h: (b, layer_idx, 0, h, 0, 0))]
        args += [c0, n0.reshape(n0.shape + (1,)), r0]
    out_specs = [pl.BlockSpec((rows, 256), lambda b, h: (b, h))]
    out_shape = [jax.ShapeDtypeStruct((n_seq * seq_len, D), BF16)]
    aliases = {}
    if want_final:
        n_even = (DEPTH + 1) // 2
        out_specs += [pl.BlockSpec((n_seqs, None, 2, None, DK, DK), lambda b, h: (b, layer_idx, 0, h, 0, 0)),
                      pl.BlockSpec((n_seqs, None, 2, None, 8, DK), lambda b, h: (b, layer_idx, 0, h, 0, 0)),
                      pl.BlockSpec((n_seqs, None, None, 8, 128), lambda b, h: (b, layer_idx, h, 0, 0)),
                      pl.BlockSpec((n_seqs, None, 2, None, DK, DK), lambda b, h: (b, layer_idx, 0, h, 0, 0))]
        out_shape += [jax.ShapeDtypeStruct((n_seq, n_even, 2, HEADS, DK, DK), F32),
                      jax.ShapeDtypeStruct((n_seq, n_even, 2, HEADS, 8, DK), F32),
                      jax.ShapeDtypeStruct((n_seq, n_even, HEADS, 8, 128), F32),
                      jax.ShapeDtypeStruct((n_seq, n_even, 2, HEADS, DK, DK), F32)]
        if carried is not None:
            aliases = {len(args) + k: 1 + k for k in range(len(carried))}
            in_specs += [pl.BlockSpec(memory_space=pl.ANY)] * len(carried)
            args += list(carried)
    scratch = [pltpu.VMEM((8, rows), F32), pltpu.VMEM((rows, 768), F32), pltpu.VMEM((total, 8, 128), F32),
               pltpu.VMEM((2 * n_seqs, DK, 256), F32), pltpu.VMEM((2 * n_seqs, DK, DK), F32)]
    if use_cross:
        scratch += [pltpu.VMEM((total, DK, 512), BF16), pltpu.VMEM((total, 2 * DK, DK), BF16),
                    pltpu.VMEM((total, 8, 128), F32)]
    return pl.pallas_call(
        functools.partial(_even_kernel, seq_len, n_seqs, (layer_idx, n_layers) if has_init else None, want_final,
                          len(aliases)),
        grid=(n_seq // n_seqs, HEADS),
        in_specs=in_specs,
        out_specs=out_specs,
        out_shape=out_shape,
        input_output_aliases=aliases,
        scratch_shapes=scratch,
        compiler_params=_cparams("parallel", "parallel"),
        name="even_mixer_ctx" if ctx else "even_mixer_lat",
    )(*args)


def _gla_kernel(seq_len, has_init, want_final, n_carried, *refs):
    chunk = GLA_CHUNK
    n_chunks = seq_len // chunk
    it = iter(refs)
    q_ref, k_ref, v_ref, gr_ref, lr_ref, wz_ref, bz_ref = (next(it) for _ in range(7))
    s0_ref = next(it) if has_init else None
    for _ in range(n_carried):
        next(it)
    out_ref = next(it)
    s_out_ref = next(it) if want_final else None
    qe_ref, ke_ref, cum_ref, sts_ref, st_ref = (next(it) for _ in range(5))
    unroll = min(GLA_UNROLL, n_chunks)

    z = jnp.dot(lr_ref[...], wz_ref[...], precision=lax.Precision.HIGHEST, preferred_element_type=F32) + bz_ref[...]
    la = _log_sigmoid(z) / GLA_TAU
    pos = lax.broadcasted_iota(jnp.int32, (seq_len, 1), 0) % chunk
    b_f = _seg_scan(la[:, :128], chunk, pos, jnp.add, 0.0, False)
    b_b = _seg_scan(la[:, 128:], chunk, pos, jnp.add, 0.0, True)
    q = q_ref[...].astype(F32) * (DK ** -0.5)
    k = k_ref[...].astype(F32)
    qe_ref[...] = jnp.concatenate([q * jnp.exp(b_f), q * jnp.exp(b_b)], axis=1).astype(BF16)
    ke_ref[...] = jnp.concatenate([k * jnp.exp(-b_f), k * jnp.exp(-b_b)], axis=1).astype(BF16)
    cum_ref[...] = jnp.concatenate([b_f, b_b], axis=1)

    ii = lax.broadcasted_iota(jnp.int32, (chunk, chunk), 0)
    jj = lax.broadcasted_iota(jnp.int32, (chunk, chunk), 1)

    def rows(c):
        if isinstance(c, int):
            return pl.ds(c * chunk, chunk)
        return pl.ds(pl.multiple_of(c * chunk, chunk), chunk)

    def advance(c, d):
        lanes = slice(128 * d, 128 * d + 128)
        edge = c * chunk + (chunk - 1 if d == 0 else 0)
        total = cum_ref[pl.ds(edge, 1), :][:, lanes]
        kw = (k_ref[rows(c), :].astype(F32) * jnp.exp(total - cum_ref[rows(c), :][:, lanes])).astype(BF16)
        st = st_ref[d]
        sts_ref[c, :, lanes] = st.astype(BF16)
        st_ref[d] = st * jnp.exp(total) + _dot_tn(v_ref[rows(c), :], kw)

    if has_init:
        st_ref[0] = s0_ref[0].T
        st_ref[1] = s0_ref[1].T
    else:
        st_ref[...] = jnp.zeros((2, DV_GLA, DK), F32)

    def state_group(g, carry):
        for u in range(unroll):
            c = g * unroll + u
            advance(c, 0)
            advance(n_chunks - 1 - c, 1)
        return carry

    def output_group(g, carry):
        for u in range(unroll):
            c = g * unroll + u
            qe = qe_ref[rows(c), :]
            ke = ke_ref[rows(c), :]
            att = (jnp.where(jj <= ii, _dot_nt(qe[:, :128], ke[:, :128]), 0.0)
                   + jnp.where(jj >= ii, _dot_nt(qe[:, 128:], ke[:, 128:]), 0.0))
            o = _dot(att.astype(BF16), v_ref[rows(c), :]) + _dot_nt(qe, sts_ref[c])
            gr = gr_ref[rows(c), :].astype(F32)
            out_ref[rows(c), :] = (_rms(o) * (gr * jax.nn.sigmoid(gr))).astype(BF16)
        return carry

    n_groups = n_chunks // unroll
    if n_groups == 1:
        state_group(0, 0)
        output_group(0, 0)
    else:
        lax.fori_loop(0, n_groups, state_group, 0)
        lax.fori_loop(0, n_groups, output_group, 0)
    if want_final:
        s_out_ref[0] = st_ref[0].T
        s_out_ref[1] = st_ref[1].T


def _gla_mixer(proj, lowrank, wz, bz, group, init, layer_idx=0, carried=None):
    ctx = group == 0
    n_seq, seq_len = (N_CTX_SEQ, CTX_LEN) if ctx else (N_LAT_SEQ, LAT_LEN)
    row_off = 0 if ctx else N_CTX_TOK // LAT_LEN
    has_init, want_final = not ctx, ctx
    n_chunks = seq_len // GLA_CHUNK
    in_specs = [pl.BlockSpec((seq_len, 128), lambda b, h: (row_off + b, h)),
                pl.BlockSpec((seq_len, 128), lambda b, h: (row_off + b, 4 + h)),
                pl.BlockSpec((seq_len, 256), lambda b, h: (row_off + b, 4 + h)),
                pl.BlockSpec((seq_len, 256), lambda b, h: (row_off + b, 8 + h)),
                pl.BlockSpec((seq_len, 128), lambda b, h: (row_off + b, 0)),
                pl.BlockSpec((None, 128, 256), lambda b, h: (h, 0, 0)),
                pl.BlockSpec((None, 1, 256), lambda b, h: (h, 0, 0))]
    args = [proj, proj, proj, proj, lowrank, wz, bz]
    if has_init:
        in_specs.append(pl.BlockSpec((None, None, 2, None, DK, DV_GLA), lambda b, h: (b, layer_idx, 0, h, 0, 0)))
        args.append(init)
    out_specs = [pl.BlockSpec((seq_len, 256), lambda b, h: (b, h))]
    out_shape = [jax.ShapeDtypeStruct((n_seq * seq_len, D), BF16)]
    aliases = {}
    if want_final:
        out_specs.append(pl.BlockSpec((None, None, 2, None, DK, DV_GLA), lambda b, h: (b, layer_idx, 0, h, 0, 0)))
        out_shape.append(jax.ShapeDtypeStruct((n_seq, DEPTH // 2, 2, HEADS, DK, DV_GLA), F32))
        if carried is not None:
            aliases = {len(args): 1}
            in_specs.append(pl.BlockSpec(memory_space=pl.ANY))
            args.append(carried)
    scratch = [pltpu.VMEM((seq_len, 256), BF16)] * 2 + [pltpu.VMEM((seq_len, 256), F32),
                                                        pltpu.VMEM((n_chunks, DV_GLA, 2 * DK), BF16),
                                                        pltpu.VMEM((2, DV_GLA, DK), F32)]
    return pl.pallas_call(
        functools.partial(_gla_kernel, seq_len, has_init, want_final, len(aliases)),
        grid=(n_seq, HEADS),
        in_specs=in_specs,
        out_specs=out_specs,
        out_shape=out_shape,
        input_output_aliases=aliases,
        scratch_shapes=scratch,
        compiler_params=_cparams("parallel", "parallel"),
        name="gla_mixer_ctx" if ctx else "gla_mixer_lat",
    )(*args)


def _tail_kernel(last, tile_off, n_tiles, *refs):
    it = iter(refs)
    hc_ref, hl_ref, hup_ref, hdn_ref, x_ref, xup_ref, xdn_ref = (next(it) for _ in range(7))
    wo_ref, win_ref, cw_ref, cb_ref, w_ref, mod_ref, g_mix_ref, g_pre_ref, g_post_ref = (next(it) for _ in range(9))
    if not last:
        g_next_ref, mod_next_ref = next(it), next(it)
    x_out_ref = next(it)
    xn_out_ref = None if last else next(it)
    hbuf_ref, xmid_ref = next(it), next(it)
    i = pl.program_id(0) + tile_off
    n_ctx = N_CTX_TOK // FFN_TILE
    tiles_per_seq = LAT_LEN // FFN_TILE
    n_chunks = D_FF // FF_CHUNK

    def cols(f):
        return slice(f * FF_CHUNK, (f + 1) * FF_CHUNK), slice(D_FF + f * FF_CHUNK, D_FF + (f + 1) * FF_CHUNK)

    def act(conv, v, sl):
        return (jax.nn.gelu(conv + cb_ref[:, sl]) * v).astype(BF16)

    def mixer_out(h, x, keep=None):
        m = mod_ref[...]
        x_mid = x + m[2:3] * (_rms(_dot(h, wo_ref[...])) * g_mix_ref[...])
        xn = _rms(x_mid) * g_pre_ref[...] * (1.0 + m[4:5]) + m[3:4]
        if keep is not None:
            xn = jnp.where(keep, xn, 0.0)
        return x_mid, xn.astype(BF16)

    def ctx_branch():
        x_mid, xn = mixer_out(hc_ref[...], x_ref[...])
        xmid_ref[...] = x_mid
        pos = lax.broadcasted_iota(jnp.int32, (FFN_TILE, 1), 0) % CTX_LEN
        for f in range(n_chunks):
            sl, sv = cols(f)
            a = _dot(xn, win_ref[:, sl])
            left = jnp.where(pos >= 1, pltpu.roll(a, 1, 0), 0.0)
            right = jnp.where(pos < CTX_LEN - 1, pltpu.roll(a, FFN_TILE - 1, 0), 0.0)
            conv = cw_ref[3:4, sl] * left + cw_ref[4:5, sl] * a + cw_ref[5:6, sl] * right
            hbuf_ref[:, sl] = act(conv, _dot(xn, win_ref[:, sv]), sl)

    def lat_branch():
        t = (i - n_ctx) % tiles_per_seq
        n_rows = FFN_TILE + 2 * GRID_W
        rowpos = lax.broadcasted_iota(jnp.int32, (n_rows, 1), 0)
        first = jnp.where(t > 0, 0, GRID_W)
        stop = jnp.where(t < tiles_per_seq - 1, n_rows, GRID_W + FFN_TILE)
        x_mid, xe = mixer_out(jnp.concatenate([hup_ref[...], hl_ref[...], hdn_ref[...]], axis=0),
                              jnp.concatenate([xup_ref[...], x_ref[...], xdn_ref[...]], axis=0),
                              (rowpos >= first) & (rowpos < stop))
        xmid_ref[...] = x_mid[GRID_W:GRID_W + FFN_TILE]
        xn = xe[GRID_W:GRID_W + FFN_TILE]
        colpos = rowpos % GRID_W
        for f in range(n_chunks):
            sl, sv = cols(f)
            buf = _dot(xe, win_ref[:, sl])
            left = jnp.where(colpos >= 1, pltpu.roll(buf, 1, 0), 0.0)
            right = jnp.where(colpos < GRID_W - 1, pltpu.roll(buf, n_rows - 1, 0), 0.0)
            conv = None
            for dr in range(3):
                rs = slice(dr * GRID_W, dr * GRID_W + FFN_TILE)
                term = (cw_ref[3 * dr:3 * dr + 1, sl] * left[rs] + cw_ref[3 * dr + 1:3 * dr + 2, sl] * buf[rs]
                        + cw_ref[3 * dr + 2:3 * dr + 3, sl] * right[rs])
                conv = term if conv is None else conv + term
            hbuf_ref[:, sl] = act(conv, _dot(xn, win_ref[:, sv]), sl)

    if tile_off + n_tiles <= n_ctx:
        ctx_branch()
    elif tile_off >= n_ctx:
        lat_branch()
    else:
        pl.when(i < n_ctx)(ctx_branch)
        pl.when(i >= n_ctx)(lat_branch)
    x_new = xmid_ref[...] + mod_ref[5:6, :] * (_rms(_dot(hbuf_ref[...], w_ref[...])) * g_post_ref[...])
    x_out_ref[...] = x_new
    if not last:
        mn = mod_next_ref[...]
        xn_out_ref[...] = (_rms(x_new) * g_next_ref[...] * (1.0 + mn[1:2]) + mn[0:1]).astype(BF16)


def _layer_tail(h_ctx, h_lat, w_o, w_in, conv_w, conv_b, w_out, x, mods, g_mix_post, g_ffn_pre, g_post, g_next,
                layer, tile_off=0, n_tiles=N_TOK // FFN_TILE):
    last = layer == DEPTH - 1
    halo_per_tile = FFN_TILE // GRID_W
    n_halo = N_TOK // GRID_W
    n_ctx = N_CTX_TOK // FFN_TILE
    n_lat_halo = (N_TOK - N_CTX_TOK) // GRID_W
    row = pl.BlockSpec((FFN_TILE, D), lambda i: (i + tile_off, 0))
    resident = dict(pipeline_mode=pl.Buffered(1))
    layer_vec = pl.BlockSpec((None, 1, D), lambda i: (layer, 0, 0))

    def lat_halo(i, shift):
        return jnp.clip((i + tile_off - n_ctx + shift) * halo_per_tile - (1 - shift), 0, n_lat_halo - 1)

    in_specs = [pl.BlockSpec((FFN_TILE, D), lambda i: (jnp.minimum(i + tile_off, n_ctx - 1), 0)),
                pl.BlockSpec((FFN_TILE, D), lambda i: (jnp.maximum(i + tile_off - n_ctx, 0), 0)),
                pl.BlockSpec((GRID_W, D), lambda i: (lat_halo(i, 0), 0)),
                pl.BlockSpec((GRID_W, D), lambda i: (lat_halo(i, 1), 0)),
                row,
                pl.BlockSpec((GRID_W, D), lambda i: (jnp.maximum((i + tile_off) * halo_per_tile - 1, 0), 0)),
                pl.BlockSpec((GRID_W, D), lambda i: (jnp.minimum((i + tile_off + 1) * halo_per_tile, n_halo - 1), 0)),
                pl.BlockSpec((D, D), lambda i: (0, 0), **resident),
                pl.BlockSpec((D, 2 * D_FF), lambda i: (0, 0), **resident),
                pl.BlockSpec((None, 9, D_FF), lambda i: (layer, 0, 0)),
                pl.BlockSpec((None, 1, D_FF), lambda i: (layer, 0, 0)),
                pl.BlockSpec((D_FF, D), lambda i: (0, 0), **resident),
                pl.BlockSpec((None, None, 6, D), lambda i: (layer, _mod_row(i + tile_off, FFN_TILE), 0, 0)),
                layer_vec, layer_vec, layer_vec]
    args = [h_ctx, h_lat, h_lat, h_lat, x, x, x, w_o, w_in, conv_w.reshape(DEPTH, 9, D_FF), conv_b, w_out, mods,
            g_mix_post, g_ffn_pre, g_post]
    out_specs = [pl.BlockSpec((FFN_TILE, D), lambda i: (i, 0))]
    out_shape = [jax.ShapeDtypeStruct((n_tiles * FFN_TILE, D), F32)]
    if not last:
        in_specs += [pl.BlockSpec((None, 1, D), lambda i: (layer + 1, 0, 0)),
                     pl.BlockSpec((None, None, 6, D), lambda i: (layer + 1, _mod_row(i, FFN_TILE), 0, 0))]
        args += [g_next, mods]
        out_specs.append(pl.BlockSpec((FFN_TILE, D), lambda i: (i, 0)))
        out_shape.append(jax.ShapeDtypeStruct((n_tiles * FFN_TILE, D), BF16))
    return pl.pallas_call(
        functools.partial(_tail_kernel, last, tile_off, n_tiles),
        grid=(n_tiles,),
        in_specs=in_specs,
        out_specs=out_specs,
        out_shape=out_shape,
        scratch_shapes=[pltpu.VMEM((FFN_TILE, D_FF), BF16), pltpu.VMEM((FFN_TILE, D), F32)],
        compiler_params=_cparams("parallel"),
        name="layer_tail",
    )(*args)


def _even_weights(w_in, w_out):
    w_main = jnp.concatenate([w_in[:, :2048], w_in[:, 2064:]], axis=1).astype(BF16)
    src = [2048] * (8 * HEADS)
    used = [0.0] * (8 * HEADS)
    for h in range(HEADS):
        for k, col in enumerate((h, 8 + h, 4 + h, 12 + h)):
            src[8 * h + k] = 2048 + col
            used[8 * h + k] = 1.0
    w_gate = (w_in[:, jnp.array(src)] * jnp.array(used, F32)).T.astype(BF16)
    w_o = w_out.reshape(2, HEADS, 128, D).transpose(1, 0, 2, 3).reshape(D, D).astype(BF16)
    return w_main, w_gate, w_o


def _odd_weights(w_in, w2, b2):
    w_main = w_in[:, :3072].astype(BF16)
    w_lr = jnp.pad(w_in[:, 3072:3072 + 2 * GLA_RANK], ((0, 0), (0, 128 - 2 * GLA_RANK))).astype(BF16)
    w2h = w2.reshape(2, GLA_RANK, HEADS, DK).transpose(2, 0, 1, 3)
    wz = jnp.zeros((HEADS, 128, 256), F32)
    wz = wz.at[:, :GLA_RANK, :DK].set(w2h[:, 0]).at[:, GLA_RANK:2 * GLA_RANK, DK:].set(w2h[:, 1])
    bz = b2.reshape(2, HEADS, DK).transpose(1, 0, 2).reshape(HEADS, 1, 2 * DK)
    return w_main, w_lr, wz, bz


def kernel(x_prompt, x_sample, c, c_ctx, state_mlstm_C, state_mlstm_n, state_mlstm_m, state_ret, state_gla, ada_w, ada_b, norm_mix_pre, norm_mix_post, norm_ffn_pre, norm_ffn_post, w_in_even, w_out_even, mlstm_igate_b, mlstm_fgate_b, ret_decay_logit, w_in_odd, gla_gate_w2, gla_gate_b, w_out_odd, ffn_w_in, ffn_conv_w, ffn_conv_b, ffn_w_out):
    cond = jnp.concatenate([c_ctx[None, :], c, jnp.zeros((8 - 1 - N_LAT_SEQ, D), F32)], axis=0)
    mods = _mod_table(cond, ada_w, ada_b)
    norm_mix_pre, norm_mix_post, norm_ffn_pre, norm_ffn_post = (
        g.reshape(DEPTH, 1, D) for g in (norm_mix_pre, norm_mix_post, norm_ffn_pre, norm_ffn_post))
    ffn_conv_b = ffn_conv_b.reshape(DEPTH, 1, D_FF)

    x, xn = _prenorm(x_prompt.reshape(N_CTX_TOK, D), x_sample.reshape(N_LAT_SEQ * LAT_LEN, D),
                     norm_mix_pre, mods, 0)
    n_even, n_odd = (DEPTH + 1) // 2, DEPTH // 2
    even_finals = [jnp.zeros((N_CTX_SEQ, n_even, 2, HEADS, DK, DK), F32),
                   jnp.zeros((N_CTX_SEQ, n_even, 2, HEADS, 8, DK), F32),
                   jnp.zeros((N_CTX_SEQ, n_even, HEADS, 8, 128), F32),
                   jnp.zeros((N_CTX_SEQ, n_even, 2, HEADS, DK, DK), F32)]
    gla_finals = jnp.zeros((N_CTX_SEQ, n_odd, 2, HEADS, DK, DV_GLA), F32)
    for layer in range(DEPTH):
        idx = layer // 2
        if layer % 2 == 0:
            w_main, w_gate, w_o = _even_weights(w_in_even[idx], w_out_even[idx])
            proj = _matmul(xn, w_main, MM_ROW_TILE, 1024, "proj_even", BF16)
            gates = _gates_t(xn, w_gate)
            params = jnp.concatenate([mlstm_igate_b[idx].reshape(-1), mlstm_fgate_b[idx].reshape(-1),
                                      ret_decay_logit[idx].reshape(-1)])
            h_ctx, *even_finals = _even_mixer(proj, gates, params, 0, None, idx, even_finals)
            init = (state_mlstm_C, state_mlstm_n, state_mlstm_m, state_ret)
            (h_lat,) = _even_mixer(proj, gates, params, 1, init, idx)
        else:
            w_main, w_lr, wz, bz = _odd_weights(w_in_odd[idx], gla_gate_w2[idx], gla_gate_b[idx])
            w_o = w_out_odd[idx].astype(BF16)
            proj = _matmul(xn, w_main, MM_ROW_TILE, 1024, "proj_odd", BF16)
            lowrank = _matmul(xn, w_lr, 1024, 128, "proj_lowrank")
            h_ctx, gla_finals = _gla_mixer(proj, lowrank, wz, bz, 0, None, idx, gla_finals)
            (h_lat,) = _gla_mixer(proj, lowrank, wz, bz, 1, state_gla, idx)
        tail = functools.partial(_layer_tail, h_ctx, h_lat, w_o, ffn_w_in[layer].astype(BF16), ffn_conv_w, ffn_conv_b,
                                 ffn_w_out[layer].astype(BF16), x, mods, norm_mix_post, norm_ffn_pre,
                                 norm_ffn_post, norm_mix_pre, layer)
        if layer < DEPTH - 1:
            x, xn = tail()
        else:
            n_ctx_tiles = N_CTX_TOK // FFN_TILE
            (y_ctx,) = tail(0, n_ctx_tiles)
            (y_lat,) = tail(n_ctx_tiles, N_TOK // FFN_TILE - n_ctx_tiles)

    y_prompt = y_ctx.reshape(N_CTX_SEQ, CTX_LEN, D)
    y_sample = y_lat.reshape(N_LAT_SEQ, LAT_LEN, D)
    new_c, new_n, new_m, new_r = even_finals
    return (y_prompt, y_sample, new_c, new_n[..., 0, :], new_m[:, :, :, 0:2, 0].transpose(0, 1, 3, 2), new_r, gla_finals)
```

```python
import functools

import jax
import jax.numpy as jnp
from jax import lax
from jax.experimental import pallas as pl
from jax.experimental.pallas import tpu as pltpu

F32 = jnp.float32
BF16 = jnp.bfloat16

D = 1024
DEPTH = 4
N_CTX_SEQ, CTX_LEN = 16, 256
N_LAT_SEQ, LAT_LEN = 2, 2048
GRID_W = 64
N_CTX_TOK = N_CTX_SEQ * CTX_LEN
N_TOK = N_CTX_TOK + N_LAT_SEQ * LAT_LEN
HEADS = 4
DK = 128
DV_GLA = 256
GLA_RANK = 16
GLA_TAU = 16.0
D_FF = 2816
EPS = 1e-6
NEG = -1e30

ROW_TILE = 256
PROJ_ROW_TILE = 1024
PROJ_COL_TILE = 1024
EVEN_CHUNK = 256
EVEN_UNROLL = 2
EVEN_CTX_SEQS = 4
GLA_CHUNK = 64
GLA_UNROLL = 4
GLA_CTX_SEQS = 4
GLA_OUT_UNROLL = 8
FF_CHUNK = 256
FFN_TILE = 512
VMEM_LIMIT = 56 * 1024 * 1024


def _cparams(*sem):
    return pltpu.CompilerParams(dimension_semantics=sem, vmem_limit_bytes=VMEM_LIMIT)


def _mod_row(i, tile):
    n_ctx = N_CTX_TOK // tile
    return jnp.where(i < n_ctx, 0, 1 + (i - n_ctx) // (LAT_LEN // tile))


def _rms(x):
    return x * lax.rsqrt(jnp.mean(x * x, axis=-1, keepdims=True) + EPS)


def _log_sigmoid(x):
    return jnp.minimum(x, 0.0) - jnp.log1p(jnp.exp(-jnp.abs(x)))


def _dot(a, b):
    return jnp.dot(a, b, preferred_element_type=F32)


def _dot_nt(a, b):
    return lax.dot_general(a, b, (((1,), (1,)), ((), ())), preferred_element_type=F32)


def _dot_tn(a, b):
    return lax.dot_general(a, b, (((0,), (0,)), ((), ())), preferred_element_type=F32)


def _seg_scan(x, seg, pos, op, ident, reverse):
    n = x.shape[0]
    s = 1
    while s < seg:
        if reverse:
            x = op(x, jnp.where(pos < seg - s, pltpu.roll(x, n - s, 0), ident))
        else:
            x = op(x, jnp.where(pos >= s, pltpu.roll(x, s, 0), ident))
        s *= 2
    return x


def _mod_kernel(c_ref, w_ref, b_ref, o_ref):
    c = c_ref[...]
    s = (c * jax.nn.sigmoid(c)).astype(BF16)
    o_ref[0] = _dot(s, w_ref[0].astype(BF16)) + b_ref[0]


def _mod_table(cond, ada_w, ada_b):
    tn = 1024
    out = pl.pallas_call(
        _mod_kernel,
        grid=(DEPTH, 6 * D // tn),
        in_specs=[pl.BlockSpec((8, D), lambda l, j: (0, 0)),
                  pl.BlockSpec((1, D, tn), lambda l, j: (l, 0, j)),
                  pl.BlockSpec((1, 1, tn), lambda l, j: (l, 0, j))],
        out_specs=pl.BlockSpec((1, 8, tn), lambda l, j: (l, 0, j)),
        out_shape=jax.ShapeDtypeStruct((DEPTH, 8, 6 * D), F32),
        compiler_params=_cparams("parallel", "parallel"),
        name="mod_table",
    )(cond, ada_w, ada_b.reshape(DEPTH, 1, 6 * D))
    return out.reshape(DEPTH, 8, 6, D)


def _prenorm_kernel(xp_ref, xs_ref, g_ref, mod_ref, x_ref, xn_ref):
    i = pl.program_id(0)

    def run(src_ref):
        x = src_ref[...]
        m = mod_ref[...]
        x_ref[...] = x
        xn_ref[...] = (_rms(x) * g_ref[...] * (1.0 + m[1:2]) + m[0:1]).astype(BF16)

    pl.when(i < N_CTX_TOK // ROW_TILE)(lambda: run(xp_ref))
    pl.when(i >= N_CTX_TOK // ROW_TILE)(lambda: run(xs_ref))


def _prenorm(xp, xs, g, mods, layer):
    nc = N_CTX_TOK // ROW_TILE
    return pl.pallas_call(
        _prenorm_kernel,
        grid=(N_TOK // ROW_TILE,),
        in_specs=[pl.BlockSpec((ROW_TILE, D), lambda i: (jnp.minimum(i, nc - 1), 0)),
                  pl.BlockSpec((ROW_TILE, D), lambda i: (jnp.maximum(i - nc, 0), 0)),
                  pl.BlockSpec((None, 1, D), lambda i: (layer, 0, 0)),
                  pl.BlockSpec((None, None, 6, D), lambda i: (layer, _mod_row(i, ROW_TILE), 0, 0))],
        out_specs=[pl.BlockSpec((ROW_TILE, D), lambda i: (i, 0)),
                   pl.BlockSpec((ROW_TILE, D), lambda i: (i, 0))],
        out_shape=[jax.ShapeDtypeStruct((N_TOK, D), F32), jax.ShapeDtypeStruct((N_TOK, D), BF16)],
        compiler_params=_cparams("parallel"),
        name="prenorm",
    )(xp, xs, g, mods)


def _proj_kernel(small_transposed, x_ref, w_ref, ws_ref, o_ref, os_ref):
    x = x_ref[...]
    o_ref[...] = _dot(x, w_ref[...]).astype(o_ref.dtype)

    @pl.when(pl.program_id(1) == 0)
    def _():
        os_ref[...] = _dot_nt(ws_ref[...], x) if small_transposed else _dot(x, ws_ref[...])


def _proj(xn, w_main, w_small, small_transposed, name):
    tm, tn = PROJ_ROW_TILE, PROJ_COL_TILE
    m, k = xn.shape
    n = w_main.shape[1]
    if small_transposed:
        r = w_small.shape[0]
        small_spec = pl.BlockSpec((r, tm), lambda i, j: (0, i))
        small_shape = jax.ShapeDtypeStruct((r, m), F32)
    else:
        small_spec = pl.BlockSpec((tm, w_small.shape[1]), lambda i, j: (i, 0))
        small_shape = jax.ShapeDtypeStruct((m, w_small.shape[1]), F32)
    return pl.pallas_call(
        functools.partial(_proj_kernel, small_transposed),
        grid=(m // tm, n // tn),
        in_specs=[pl.BlockSpec((tm, k), lambda i, j: (i, 0)),
                  pl.BlockSpec((k, tn), lambda i, j: (0, j)),
                  pl.BlockSpec(w_small.shape, lambda i, j: (0, 0))],
        out_specs=[pl.BlockSpec((tm, tn), lambda i, j: (i, j)), small_spec],
        out_shape=[jax.ShapeDtypeStruct((m, n), BF16), small_shape],
        compiler_params=_cparams("parallel", "arbitrary"),
        name=name,
    )(xn, w_main, w_small)


def _even_kernel(seq_len, n_seqs, init_layer, want_final, n_carried, *refs):
    has_init = init_layer is not None
    layer_idx, n_layers = init_layer if has_init else (0, 1)
    chunk = EVEN_CHUNK
    n_chunks = seq_len // chunk
    total = n_seqs * n_chunks
    use_cross = has_init or n_chunks > 1
    unroll = min(EVEN_UNROLL, n_chunks)
    assert not has_init or n_seqs == 1
    it = iter(refs)
    par_ref = next(it)
    m0_ref = next(it) if has_init else None
    mq_ref, mk_ref, mv_ref, mo_ref, rq_ref, rk_ref, rv_ref, rg_ref, gate_ref = (next(it) for _ in range(9))
    if has_init:
        c0_ref, n0_ref, r0_ref = next(it), next(it), next(it)
    for _ in range(n_carried):
        next(it)
    out_ref = next(it)
    if want_final:
        c_out_ref, n_out_ref, m_out_ref, r_out_ref = next(it), next(it), next(it), next(it)
    grow_ref, rep_ref, edge_ref, cn_ref, s_ref = (next(it) for _ in range(5))
    if use_cross:
        cns_ref, ss_ref, ms_ref = next(it), next(it), next(it)

    b = pl.program_id(0)
    h = pl.program_id(1)
    ii = lax.broadcasted_iota(jnp.int32, (chunk, chunk), 0)
    jj = lax.broadcasted_iota(jnp.int32, (chunk, chunk), 1)
    pos_col = lax.broadcasted_iota(jnp.int32, (chunk, 1), 0).astype(F32)
    pos_int = lax.broadcasted_iota(jnp.int32, (chunk, 1), 0)
    ones_blk = jnp.ones((chunk, 128), BF16)
    ones_sum = jnp.ones((256, 128), BF16)

    sub = lax.broadcasted_iota(jnp.int32, (8, 1), 0)
    bias = jnp.where(sub == 0, par_ref[h], jnp.where(sub == 1, par_ref[4 + h],
                     jnp.where(sub == 2, par_ref[8 + h], jnp.where(sub == 3, par_ref[12 + h], 0.0))))
    lane = lax.broadcasted_iota(jnp.int32, (1, 128), 1)
    tri_pre = (ii <= jj).astype(BF16)
    tri_suf = (ii >= jj).astype(BF16)

    def lanes(c):
        if isinstance(c, int):
            return slice(c * chunk, (c + 1) * chunk)
        return pl.ds(pl.multiple_of(c * chunk, chunk), chunk)

    def rows(c):
        if isinstance(c, int):
            return pl.ds(c * chunk, chunk)
        return pl.ds(pl.multiple_of(c * chunk, chunk), chunk)

    def prepare_chunk(c):
        pre = gate_ref[:, lanes(c)] + bias
        log_f = pltpu.roll(_log_sigmoid(pre), 6, 0)
        hi = log_f.astype(BF16)
        rest = log_f - hi.astype(F32)
        mid = rest.astype(BF16)
        low = (rest - mid.astype(F32)).astype(BF16)
        parts = jnp.concatenate([hi, mid, low, jnp.zeros_like(hi)], axis=0)
        cum_pre = _dot(parts, tri_pre)
        cum_suf = _dot(parts, tri_suf)
        fcum = jnp.where(sub == 0, cum_pre[0:8] + cum_pre[8:16] + cum_pre[16:24],
                         cum_suf[0:8] + cum_suf[8:16] + cum_suf[16:24])
        gsc = pre - fcum
        packed = jnp.where(sub < 2, gsc, pltpu.roll(fcum, 2, 0))
        grow_ref[:, lanes(c)] = packed
        pcol = packed.T
        g_rep = [jnp.broadcast_to(pcol[:, k:k + 1], (chunk, 128)) for k in range(4)]
        g_rep.append(_seg_scan(g_rep[0], chunk, pos_int, jnp.maximum, NEG, False))
        g_rep.append(_seg_scan(g_rep[1], chunk, pos_int, jnp.maximum, NEG, True))
        rep_ref[rows(c), :] = jnp.concatenate(g_rep, axis=1)
        f_tot = jnp.sum(log_f, axis=1, keepdims=True)
        g_top = jnp.max(gsc, axis=1, keepdims=True)
        edge_ref[c] = jnp.where(lane == 0, f_tot, jnp.where(lane == 1, g_top, 0.0))

    def for_all_chunks(body):
        if total <= 2 * EVEN_UNROLL:
            for c in range(total):
                body(c)
        else:
            def group(g, carry):
                for u in range(EVEN_UNROLL):
                    body(g * EVEN_UNROLL + u)
                return carry
            lax.fori_loop(0, total // EVEN_UNROLL, group, 0)

    for_all_chunks(prepare_chunk)

    k_scale = DK ** -0.5
    lg_f = _log_sigmoid(jnp.full((1, 1), par_ref[16 + h], F32))
    lg_b = _log_sigmoid(jnp.full((1, 1), par_ref[20 + h], F32))
    diff = (ii - jj).astype(F32)
    ret_mask = k_scale * (jnp.where(diff >= 0, jnp.exp(lg_f * jnp.maximum(diff, 0.0)), 0.0)
                          + jnp.where(diff <= 0, jnp.exp(lg_b * jnp.maximum(-diff, 0.0)), 0.0))
    rq_dec = (jnp.exp(lg_f * (pos_col + 1.0)), jnp.exp(lg_b * (chunk - pos_col)))
    rk_dec = (k_scale * jnp.exp(lg_f * (chunk - 1.0 - pos_col)), k_scale * jnp.exp(lg_b * pos_col))
    r_chunk = (jnp.exp(lg_f * chunk), jnp.exp(lg_b * chunk))

    def load_kv(c):
        return mk_ref[rows(c), :], jnp.concatenate([mv_ref[rows(c), :], ones_blk], axis=1)

    def ret_kv(c):
        return rk_ref[rows(c), :], rv_ref[rows(c), :]

    def advance(c, d, m, slot):
        if use_cross:
            cns_ref[c, :, 256 * d:256 * d + 256] = cn_ref[slot].astype(BF16)
            ss_ref[c, 128 * d:128 * d + 128, :] = s_ref[slot].astype(BF16)
            ms_ref[c, d:d + 1, :] = jnp.broadcast_to(m, (1, 128))
        edge = edge_ref[c]
        mu = jnp.maximum(m, edge[d:d + 1, 1:2])
        k, vext = load_kv(c)
        kw = (k.astype(F32) * (k_scale * jnp.exp(rep_ref[rows(c), 128 * d:128 * d + 128] - mu))).astype(BF16)
        cn_ref[slot] = jnp.exp(m - mu) * cn_ref[slot] + _dot_tn(kw, vext)
        rk, rv = ret_kv(c)
        s_ref[slot] = r_chunk[d] * s_ref[slot] + _dot_tn((rk.astype(F32) * rk_dec[d]).astype(BF16), rv)
        return edge[d:d + 1, 0:1] + mu

    if has_init:
        base = (b * n_layers + layer_idx) * 8
        m_init = (jnp.full((1, 1), m0_ref[base + h], F32), jnp.full((1, 1), m0_ref[base + 4 + h], F32))
        for d in range(2):
            cn_ref[d] = jnp.concatenate([c0_ref[d], jnp.broadcast_to(n0_ref[d], (DK, 128))], axis=1)
            s_ref[d] = r0_ref[d]
    else:
        m_init = (jnp.zeros((1, 1), F32), jnp.zeros((1, 1), F32))
        cn_ref[...] = jnp.zeros((2 * n_seqs, DK, 256), F32)
        s_ref[...] = jnp.zeros((2 * n_seqs, DK, DK), F32)

    m_fin = []
    for seq in range(n_seqs):
        def state_group(g, carry, seq=seq):
            m_f, m_b = carry
            for u in range(unroll):
                c = g * unroll + u
                m_f = advance(seq * n_chunks + c, 0, m_f, 2 * seq)
                m_b = advance(seq * n_chunks + n_chunks - 1 - c, 1, m_b, 2 * seq + 1)
            return m_f, m_b

        if n_chunks // unroll > 1:
            m_fin.append(lax.fori_loop(0, n_chunks // unroll, state_group, m_init))
        elif use_cross or want_final:
            m_fin.append(state_group(0, m_init))

    def rms_rep(x):
        sq = x * x
        hi = sq.astype(BF16)
        low = (sq - hi.astype(F32)).astype(BF16)
        ssum = _dot(jnp.concatenate([hi, low], axis=1), ones_sum)
        return x * lax.rsqrt(ssum * (1.0 / 128) + EPS)

    def output_chunk(c):
        q = mq_ref[rows(c), :]
        k, vext = load_kv(c)
        scores = _dot_nt(q, k) * k_scale
        cols = rep_ref[rows(c), :]
        f_f, f_b, gm_f, gm_b = cols[:, 256:384], cols[:, 384:512], cols[:, 512:640], cols[:, 640:768]
        row = grow_ref[:, lanes(c)]
        if use_cross:
            m_f = ms_ref[c, 0:1, :][:, 0:1]
            m_b = ms_ref[c, 1:2, :][:, 0:1]
        else:
            m_f, m_b = m_init
        mu_f = jnp.maximum(m_f, gm_f)
        mu_b = jnp.maximum(m_b, gm_b)
        p_f = jnp.exp(jnp.where(jj <= ii, row[0:1, :], NEG) - jnp.concatenate([mu_f, mu_f], axis=1)) * scores
        p_b = jnp.exp(jnp.where(jj >= ii, row[1:2, :], NEG) - jnp.concatenate([mu_b, mu_b], axis=1)) * scores
        tot = _dot(jnp.concatenate([p_f, p_b], axis=0).astype(BF16), vext)
        num_f, den_f, num_b, den_b = tot[:chunk, :128], tot[:chunk, 128:], tot[chunk:, :128], tot[chunk:, 128:]
        if use_cross:
            cross = _dot(q, cns_ref[c])
            w_f = jnp.exp(m_f - mu_f)
            w_b = jnp.exp(m_b - mu_b)
            num_f, den_f = num_f + w_f * cross[:, 0:128], den_f + w_f * cross[:, 128:256]
            num_b, den_b = num_b + w_b * cross[:, 256:384], den_b + w_b * cross[:, 384:512]
        h_f = num_f / jnp.maximum(jnp.abs(den_f), jnp.exp(-(f_f + mu_f)))
        h_b = num_b / jnp.maximum(jnp.abs(den_b), jnp.exp(-(f_b + mu_b)))
        out_m = rms_rep(h_f + h_b) * jax.nn.sigmoid(mo_ref[rows(c), :].astype(F32))
        rq = rq_ref[rows(c), :]
        rk, rv = ret_kv(c)
        o_r = _dot((_dot_nt(rq, rk) * ret_mask).astype(BF16), rv)
        if use_cross:
            rq32 = rq.astype(F32)
            q_in = jnp.concatenate([rq32 * rq_dec[0], rq32 * rq_dec[1]], axis=1).astype(BF16)
            o_r = o_r + _dot(q_in, ss_ref[c])
        rg = rg_ref[rows(c), :].astype(F32)
        out_r = rms_rep(o_r) * (rg * jax.nn.sigmoid(rg))
        out_ref[rows(c), :] = jnp.concatenate([out_m, out_r], axis=1).astype(BF16)

    for_all_chunks(output_chunk)

    if want_final:
        for seq in range(n_seqs):
            for d in range(2):
                cn = cn_ref[2 * seq + d]
                c_out_ref[seq, d] = cn[:, :DK]
                n_out_ref[seq, d] = cn[:, DK:].T[0:8, :]
                r_out_ref[seq, d] = s_ref[2 * seq + d]
            m_out_ref[seq] = jnp.concatenate([jnp.broadcast_to(m_fin[seq][0], (1, 128)),
                                              jnp.broadcast_to(m_fin[seq][1], (1, 128)), jnp.zeros((6, 128), F32)], axis=0)


def _even_mixer(proj, gates_t, params, group, init, layer_idx=0, carried=None):
    ctx = group == 0
    n_seq, seq_len = (N_CTX_SEQ, CTX_LEN) if ctx else (N_LAT_SEQ, LAT_LEN)
    row_off = 0 if ctx else N_CTX_TOK // LAT_LEN
    has_init, want_final = not ctx, ctx
    n_seqs = EVEN_CTX_SEQS if ctx else 1
    rows = n_seqs * seq_len
    n_chunks = seq_len // EVEN_CHUNK
    total = n_seqs * n_chunks
    use_cross = has_init or n_chunks > 1

    def col(k):
        return pl.BlockSpec((rows, 128), lambda b, h: (row_off + b, 4 * k + h))

    smem = pl.BlockSpec(memory_space=pltpu.SMEM)
    in_specs = [smem]
    args = [params]
    if has_init:
        c0, n0, m0, r0 = init
        n_layers = c0.shape[1]
        in_specs.append(smem)
        args.append(m0.reshape(-1))
    in_specs += [col(k) for k in range(8)]
    in_specs.append(pl.BlockSpec((8, rows), lambda b, h: (h, row_off + b)))
    args += [proj] * 8 + [gates_t]
    if has_init:
        in_specs += [pl.BlockSpec((None, None, 2, None, DK, 128), lambda b, h: (b, layer_idx, 0, h, 0, 0)),
                     pl.BlockSpec((None, None, 2, None, DK, 1), lambda b, h: (b, layer_idx, 0, h, 0, 0)),
                     pl.BlockSpec((None, None, 2, None, DK, 128), lambda b, h: (b, layer_idx, 0, h, 0, 0))]
        args += [c0, n0.reshape(n0.shape + (1,)), r0]
    out_specs = [pl.BlockSpec((rows, 256), lambda b, h: (b, h))]
    out_shape = [jax.ShapeDtypeStruct((n_seq * seq_len, D), BF16)]
    aliases = {}
    if want_final:
        n_even = (DEPTH + 1) // 2
        out_specs += [pl.BlockSpec((n_seqs, None, 2, None, DK, DK), lambda b, h: (b, layer_idx, 0, h, 0, 0)),
                      pl.BlockSpec((n_seqs, None, 2, None, 8, DK), lambda b, h: (b, layer_idx, 0, h, 0, 0)),
                      pl.BlockSpec((n_seqs, None, None, 8, 128), lambda b, h: (b, layer_idx, h, 0, 0)),
                      pl.BlockSpec((n_seqs, None, 2, None, DK, DK), lambda b, h: (b, layer_idx, 0, h, 0, 0))]
        out_shape += [jax.ShapeDtypeStruct((n_seq, n_even, 2, HEADS, DK, DK), F32),
                      jax.ShapeDtypeStruct((n_seq, n_even, 2, HEADS, 8, DK), F32),
                      jax.ShapeDtypeStruct((n_seq, n_even, HEADS, 8, 128), F32),
                      jax.ShapeDtypeStruct((n_seq, n_even, 2, HEADS, DK, DK), F32)]
        if carried is not None:
            aliases = {len(args) + k: 1 + k for k in range(len(carried))}
            in_specs += [pl.BlockSpec(memory_space=pl.ANY)] * len(carried)
            args += list(carried)
    scratch = [pltpu.VMEM((8, rows), F32), pltpu.VMEM((rows, 768), F32), pltpu.VMEM((total, 8, 128), F32),
               pltpu.VMEM((2 * n_seqs, DK, 256), F32), pltpu.VMEM((2 * n_seqs, DK, DK), F32)]
    if use_cross:
        scratch += [pltpu.VMEM((total, DK, 512), BF16), pltpu.VMEM((total, 2 * DK, DK), BF16),
                    pltpu.VMEM((total, 8, 128), F32)]
    return pl.pallas_call(
        functools.partial(_even_kernel, seq_len, n_seqs, (layer_idx, n_layers) if has_init else None, want_final,
                          len(aliases)),
        grid=(n_seq // n_seqs, HEADS),
        in_specs=in_specs,
        out_specs=out_specs,
        out_shape=out_shape,
        input_output_aliases=aliases,
        scratch_shapes=scratch,
        compiler_params=_cparams("parallel", "parallel"),
        name="even_mixer_ctx" if ctx else "even_mixer_lat",
    )(*args)


def _gla_kernel(seq_len, n_seqs, has_init, want_final, n_carried, *refs):
    chunk = GLA_CHUNK
    n_chunks = seq_len // chunk
    total = n_seqs * n_chunks
    n_rows = n_seqs * seq_len
    assert not has_init or n_seqs == 1
    it = iter(refs)
    q_ref, k_ref, v_ref, gr_ref, lr_ref, wz_ref, bz_ref = (next(it) for _ in range(7))
    s0_ref = next(it) if has_init else None
    for _ in range(n_carried):
        next(it)
    out_ref = next(it)
    s_out_ref = next(it) if want_final else None
    qe_ref, ke_ref, cum_ref, sts_ref, st_ref = (next(it) for _ in range(5))
    unroll = min(GLA_UNROLL, n_chunks)

    z = jnp.dot(lr_ref[...], wz_ref[...], precision=lax.Precision.HIGHEST, preferred_element_type=F32) + bz_ref[...]
    la = _log_sigmoid(z) / GLA_TAU
    pos = lax.broadcasted_iota(jnp.int32, (n_rows, 1), 0) % chunk
    b_f = _seg_scan(la[:, :128], chunk, pos, jnp.add, 0.0, False)
    b_b = _seg_scan(la[:, 128:], chunk, pos, jnp.add, 0.0, True)
    q = q_ref[...].astype(F32) * (DK ** -0.5)
    k = k_ref[...].astype(F32)
    qe_ref[...] = jnp.concatenate([q * jnp.exp(b_f), q * jnp.exp(b_b)], axis=1).astype(BF16)
    ke_ref[...] = jnp.concatenate([k * jnp.exp(-b_f), k * jnp.exp(-b_b)], axis=1).astype(BF16)
    cum_ref[...] = jnp.concatenate([b_f, b_b], axis=1)

    ii = lax.broadcasted_iota(jnp.int32, (chunk, chunk), 0)
    jj = lax.broadcasted_iota(jnp.int32, (chunk, chunk), 1)

    def rows(c):
        if isinstance(c, int):
            return pl.ds(c * chunk, chunk)
        return pl.ds(pl.multiple_of(c * chunk, chunk), chunk)

    def advance(c, d, slot):
        lanes = slice(128 * d, 128 * d + 128)
        edge = c * chunk + (chunk - 1 if d == 0 else 0)
        decay = cum_ref[pl.ds(edge, 1), :][:, lanes]
        kw = (k_ref[rows(c), :].astype(F32) * jnp.exp(decay - cum_ref[rows(c), :][:, lanes])).astype(BF16)
        st = st_ref[slot]
        sts_ref[c, :, lanes] = st.astype(BF16)
        st_ref[slot] = st * jnp.exp(decay) + _dot_tn(v_ref[rows(c), :], kw)

    if has_init:
        st_ref[0] = s0_ref[0].T
        st_ref[1] = s0_ref[1].T
    else:
        st_ref[...] = jnp.zeros((2 * n_seqs, DV_GLA, DK), F32)

    for seq in range(n_seqs):
        def state_group(g, carry, seq=seq):
            for u in range(unroll):
                c = g * unroll + u
                advance(seq * n_chunks + c, 0, 2 * seq)
                advance(seq * n_chunks + n_chunks - 1 - c, 1, 2 * seq + 1)
            return carry

        if n_chunks // unroll == 1:
            state_group(0, 0)
        else:
            lax.fori_loop(0, n_chunks // unroll, state_group, 0)

    out_unroll = min(GLA_OUT_UNROLL, total)

    def output_group(g, carry):
        for u in range(out_unroll):
            c = g * out_unroll + u
            qe = qe_ref[rows(c), :]
            ke = ke_ref[rows(c), :]
            att = (jnp.where(jj <= ii, _dot_nt(qe[:, :128], ke[:, :128]), 0.0)
                   + jnp.where(jj >= ii, _dot_nt(qe[:, 128:], ke[:, 128:]), 0.0))
            o = _dot(att.astype(BF16), v_ref[rows(c), :]) + _dot_nt(qe, sts_ref[c])
            gr = gr_ref[rows(c), :].astype(F32)
            out_ref[rows(c), :] = (_rms(o) * (gr * jax.nn.sigmoid(gr))).astype(BF16)
        return carry

    if total // out_unroll == 1:
        output_group(0, 0)
    else:
        lax.fori_loop(0, total // out_unroll, output_group, 0)
    if want_final:
        for seq in range(n_seqs):
            s_out_ref[seq, 0] = st_ref[2 * seq].T
            s_out_ref[seq, 1] = st_ref[2 * seq + 1].T


def _gla_mixer(proj, lowrank, wz, bz, group, init, layer_idx=0, carried=None):
    ctx = group == 0
    n_seq, seq_len = (N_CTX_SEQ, CTX_LEN) if ctx else (N_LAT_SEQ, LAT_LEN)
    row_off = 0 if ctx else N_CTX_TOK // LAT_LEN
    has_init, want_final = not ctx, ctx
    n_seqs = GLA_CTX_SEQS if ctx else 1
    n_rows = n_seqs * seq_len
    n_chunks = seq_len // GLA_CHUNK
    total = n_seqs * n_chunks
    in_specs = [pl.BlockSpec((n_rows, 128), lambda b, h: (row_off + b, h)),
                pl.BlockSpec((n_rows, 128), lambda b, h: (row_off + b, 4 + h)),
                pl.BlockSpec((n_rows, 256), lambda b, h: (row_off + b, 4 + h)),
                pl.BlockSpec((n_rows, 256), lambda b, h: (row_off + b, 8 + h)),
                pl.BlockSpec((n_rows, 128), lambda b, h: (row_off + b, 0)),
                pl.BlockSpec((None, 128, 256), lambda b, h: (h, 0, 0)),
                pl.BlockSpec((None, 1, 256), lambda b, h: (h, 0, 0))]
    args = [proj, proj, proj, proj, lowrank, wz, bz]
    if has_init:
        in_specs.append(pl.BlockSpec((None, None, 2, None, DK, DV_GLA), lambda b, h: (b, layer_idx, 0, h, 0, 0)))
        args.append(init)
    out_specs = [pl.BlockSpec((n_rows, 256), lambda b, h: (b, h))]
    out_shape = [jax.ShapeDtypeStruct((n_seq * seq_len, D), BF16)]
    aliases = {}
    if want_final:
        out_specs.append(pl.BlockSpec((n_seqs, None, 2, None, DK, DV_GLA), lambda b, h: (b, layer_idx, 0, h, 0, 0)))
        out_shape.append(jax.ShapeDtypeStruct((n_seq, DEPTH // 2, 2, HEADS, DK, DV_GLA), F32))
        if carried is not None:
            aliases = {len(args): 1}
            in_specs.append(pl.BlockSpec(memory_space=pl.ANY))
            args.append(carried)
    scratch = [pltpu.VMEM((n_rows, 256), BF16)] * 2 + [pltpu.VMEM((n_rows, 256), F32),
                                                       pltpu.VMEM((total, DV_GLA, 2 * DK), BF16),
                                                       pltpu.VMEM((2 * n_seqs, DV_GLA, DK), F32)]
    return pl.pallas_call(
        functools.partial(_gla_kernel, seq_len, n_seqs, has_init, want_final, len(aliases)),
        grid=(n_seq // n_seqs, HEADS),
        in_specs=in_specs,
        out_specs=out_specs,
        out_shape=out_shape,
        input_output_aliases=aliases,
        scratch_shapes=scratch,
        compiler_params=_cparams("parallel", "parallel"),
        name="gla_mixer_ctx" if ctx else "gla_mixer_lat",
    )(*args)


def _tail_kernel(last, tile_off, n_tiles, n_carried, *refs):
    it = iter(refs)
    hc_ref, hl_ref, hup_ref, hdn_ref, x_ref, xup_ref, xdn_ref = (next(it) for _ in range(7))
    wo_ref, win_ref, cw_ref, cb_ref, w_ref, mod_ref, g_mix_ref, g_pre_ref, g_post_ref = (next(it) for _ in range(9))
    if not last:
        g_next_ref, mod_next_ref = next(it), next(it)
    for _ in range(n_carried):
        next(it)
    x_out_ref = next(it)
    xn_out_ref = None if last else next(it)
    hbuf_ref = next(it)
    i = pl.program_id(0) + tile_off
    n_ctx = N_CTX_TOK // FFN_TILE
    tiles_per_seq = LAT_LEN // FFN_TILE
    n_chunks = D_FF // FF_CHUNK

    def cols(f):
        return slice(f * FF_CHUNK, (f + 1) * FF_CHUNK), slice(D_FF + f * FF_CHUNK, D_FF + (f + 1) * FF_CHUNK)

    def act(conv, v, sl):
        return (jax.nn.gelu(conv + cb_ref[:, sl]) * v).astype(BF16)

    def mixer_out(h, x, keep=None):
        m = mod_ref[...]
        x_mid = x + m[2:3] * (_rms(_dot(h, wo_ref[...])) * g_mix_ref[...])
        xn = _rms(x_mid) * g_pre_ref[...] * (1.0 + m[4:5]) + m[3:4]
        if keep is not None:
            xn = jnp.where(keep, xn, 0.0)
        return x_mid, xn.astype(BF16)

    def finish(x_mid):
        x_new = x_mid + mod_ref[5:6, :] * (_rms(_dot(hbuf_ref[...], w_ref[...])) * g_post_ref[...])
        x_out_ref[...] = x_new
        if not last:
            mn = mod_next_ref[...]
            xn_out_ref[...] = (_rms(x_new) * g_next_ref[...] * (1.0 + mn[1:2]) + mn[0:1]).astype(BF16)

    def ctx_branch():
        x_mid, xn = mixer_out(hc_ref[...], x_ref[...])
        pos = lax.broadcasted_iota(jnp.int32, (FFN_TILE, 1), 0) % CTX_LEN
        for f in range(n_chunks):
            sl, sv = cols(f)
            a = _dot(xn, win_ref[:, sl])
            left = jnp.where(pos >= 1, pltpu.roll(a, 1, 0), 0.0)
            right = jnp.where(pos < CTX_LEN - 1, pltpu.roll(a, FFN_TILE - 1, 0), 0.0)
            conv = cw_ref[3:4, sl] * left + cw_ref[4:5, sl] * a + cw_ref[5:6, sl] * right
            hbuf_ref[:, sl] = act(conv, _dot(xn, win_ref[:, sv]), sl)
        finish(x_mid)

    def lat_branch():
        t = (i - n_ctx) % tiles_per_seq
        n_rows = FFN_TILE + 2 * GRID_W
        rowpos = lax.broadcasted_iota(jnp.int32, (n_rows, 1), 0)
        first = jnp.where(t > 0, 0, GRID_W)
        stop = jnp.where(t < tiles_per_seq - 1, n_rows, GRID_W + FFN_TILE)
        x_mid, xe = mixer_out(jnp.concatenate([hup_ref[...], hl_ref[...], hdn_ref[...]], axis=0),
                              jnp.concatenate([xup_ref[...], x_ref[...], xdn_ref[...]], axis=0),
                              (rowpos >= first) & (rowpos < stop))
        x_mid = x_mid[GRID_W:GRID_W + FFN_TILE]
        xn = xe[GRID_W:GRID_W + FFN_TILE]
        colpos = rowpos % GRID_W
        for f in range(n_chunks):
            sl, sv = cols(f)
            buf = _dot(xe, win_ref[:, sl])
            left = jnp.where(colpos >= 1, pltpu.roll(buf, 1, 0), 0.0)
            right = jnp.where(colpos < GRID_W - 1, pltpu.roll(buf, n_rows - 1, 0), 0.0)
            conv = None
            for dr in range(3):
                rs = slice(dr * GRID_W, dr * GRID_W + FFN_TILE)
                term = (cw_ref[3 * dr:3 * dr + 1, sl] * left[rs] + cw_ref[3 * dr + 1:3 * dr + 2, sl] * buf[rs]
                        + cw_ref[3 * dr + 2:3 * dr + 3, sl] * right[rs])
                conv = term if conv is None else conv + term
            hbuf_ref[:, sl] = act(conv, _dot(xn, win_ref[:, sv]), sl)
        finish(x_mid)

    if tile_off + n_tiles <= n_ctx:
        ctx_branch()
    elif tile_off >= n_ctx:
        lat_branch()
    else:
        pl.when(i < n_ctx)(ctx_branch)
        pl.when(i >= n_ctx)(lat_branch)


def _layer_tail(h_ctx, h_lat, w_o, w_in, conv_w, conv_b, w_out, x, mods, g_mix_post, g_ffn_pre, g_post, g_next,
                layer, tile_off, n_tiles, x_dst=None, xn_dst=None):
    last = layer == DEPTH - 1
    halo_per_tile = FFN_TILE // GRID_W
    n_halo = N_TOK // GRID_W
    n_ctx = N_CTX_TOK // FFN_TILE
    n_lat_halo = (N_TOK - N_CTX_TOK) // GRID_W
    row = pl.BlockSpec((FFN_TILE, D), lambda i: (i + tile_off, 0))
    resident = dict(pipeline_mode=pl.Buffered(1))
    layer_vec = pl.BlockSpec((None, 1, D), lambda i: (layer, 0, 0))

    def lat_halo(i, shift):
        return jnp.clip((i + tile_off - n_ctx + shift) * halo_per_tile - (1 - shift), 0, n_lat_halo - 1)

    in_specs = [pl.BlockSpec((FFN_TILE, D), lambda i: (jnp.minimum(i + tile_off, n_ctx - 1), 0)),
                pl.BlockSpec((FFN_TILE, D), lambda i: (jnp.maximum(i + tile_off - n_ctx, 0), 0)),
                pl.BlockSpec((GRID_W, D), lambda i: (lat_halo(i, 0), 0)),
                pl.BlockSpec((GRID_W, D), lambda i: (lat_halo(i, 1), 0)),
                row,
                pl.BlockSpec((GRID_W, D), lambda i: (jnp.maximum((i + tile_off) * halo_per_tile - 1, 0), 0)),
                pl.BlockSpec((GRID_W, D), lambda i: (jnp.minimum((i + tile_off + 1) * halo_per_tile, n_halo - 1), 0)),
                pl.BlockSpec((D, D), lambda i: (0, 0), **resident),
                pl.BlockSpec((D, 2 * D_FF), lambda i: (0, 0), **resident),
                pl.BlockSpec((None, 9, D_FF), lambda i: (layer, 0, 0)),
                pl.BlockSpec((None, 1, D_FF), lambda i: (layer, 0, 0)),
                pl.BlockSpec((D_FF, D), lambda i: (0, 0), **resident),
                pl.BlockSpec((None, None, 6, D), lambda i: (layer, _mod_row(i + tile_off, FFN_TILE), 0, 0)),
                layer_vec, layer_vec, layer_vec]
    args = [h_ctx, h_lat, h_lat, h_lat, x, x, x, w_o, w_in, conv_w.reshape(DEPTH, 9, D_FF), conv_b, w_out, mods,
            g_mix_post, g_ffn_pre, g_post]
    aliases = {}
    if last:
        out_specs = [pl.BlockSpec((FFN_TILE, D), lambda i: (i, 0))]
        out_shape = [jax.ShapeDtypeStruct((n_tiles * FFN_TILE, D), F32)]
    else:
        in_specs += [pl.BlockSpec((None, 1, D), lambda i: (layer + 1, 0, 0)),
                     pl.BlockSpec((None, None, 6, D), lambda i: (layer + 1, _mod_row(i + tile_off, FFN_TILE), 0, 0))]
        args += [g_next, mods]
        aliases = {len(args): 0, len(args) + 1: 1}
        in_specs += [pl.BlockSpec(memory_space=pl.ANY)] * 2
        args += [x_dst, xn_dst]
        out_specs = [row, row]
        out_shape = [jax.ShapeDtypeStruct((N_TOK, D), F32), jax.ShapeDtypeStruct((N_TOK, D), BF16)]
    return pl.pallas_call(
        functools.partial(_tail_kernel, last, tile_off, n_tiles, len(aliases)),
        grid=(n_tiles,),
        in_specs=in_specs,
        out_specs=out_specs,
        out_shape=out_shape,
        input_output_aliases=aliases,
        scratch_shapes=[pltpu.VMEM((FFN_TILE, D_FF), BF16)],
        compiler_params=_cparams("parallel"),
        name="layer_tail",
    )(*args)


def _even_weights(w_in, w_out):
    w_main = jnp.concatenate([w_in[:, :2048], w_in[:, 2064:]], axis=1).astype(BF16)
    src = [2048] * (8 * HEADS)
    used = [0.0] * (8 * HEADS)
    for h in range(HEADS):
        for k, col in enumerate((h, 8 + h, 4 + h, 12 + h)):
            src[8 * h + k] = 2048 + col
            used[8 * h + k] = 1.0
    w_gate = (w_in[:, jnp.array(src)] * jnp.array(used, F32)).T.astype(BF16)
    w_o = w_out.reshape(2, HEADS, 128, D).transpose(1, 0, 2, 3).reshape(D, D).astype(BF16)
    return w_main, w_gate, w_o


def _odd_weights(w_in, w2, b2):
    w_main = w_in[:, :3072].astype(BF16)
    w_lr = jnp.pad(w_in[:, 3072:3072 + 2 * GLA_RANK], ((0, 0), (0, 128 - 2 * GLA_RANK))).astype(BF16)
    w2h = w2.reshape(2, GLA_RANK, HEADS, DK).transpose(2, 0, 1, 3)
    wz = jnp.zeros((HEADS, 128, 256), F32)
    wz = wz.at[:, :GLA_RANK, :DK].set(w2h[:, 0]).at[:, GLA_RANK:2 * GLA_RANK, DK:].set(w2h[:, 1])
    bz = b2.reshape(2, HEADS, DK).transpose(1, 0, 2).reshape(HEADS, 1, 2 * DK)
    return w_main, w_lr, wz, bz


def kernel(x_prompt, x_sample, c, c_ctx, state_mlstm_C, state_mlstm_n, state_mlstm_m, state_ret, state_gla, ada_w, ada_b, norm_mix_pre, norm_mix_post, norm_ffn_pre, norm_ffn_post, w_in_even, w_out_even, mlstm_igate_b, mlstm_fgate_b, ret_decay_logit, w_in_odd, gla_gate_w2, gla_gate_b, w_out_odd, ffn_w_in, ffn_conv_w, ffn_conv_b, ffn_w_out):
    cond = jnp.concatenate([c_ctx[None, :], c, jnp.zeros((8 - 1 - N_LAT_SEQ, D), F32)], axis=0)
    mods = _mod_table(cond, ada_w, ada_b)
    norm_mix_pre, norm_mix_post, norm_ffn_pre, norm_ffn_post = (
        g.reshape(DEPTH, 1, D) for g in (norm_mix_pre, norm_mix_post, norm_ffn_pre, norm_ffn_post))
    ffn_conv_b = ffn_conv_b.reshape(DEPTH, 1, D_FF)

    x, xn = _prenorm(x_prompt.reshape(N_CTX_TOK, D), x_sample.reshape(N_LAT_SEQ * LAT_LEN, D),
                     norm_mix_pre, mods, 0)
    x_spare = jnp.zeros((N_TOK, D), F32)
    n_even, n_odd = (DEPTH + 1) // 2, DEPTH // 2
    even_finals = [jnp.zeros((N_CTX_SEQ, n_even, 2, HEADS, DK, DK), F32),
                   jnp.zeros((N_CTX_SEQ, n_even, 2, HEADS, 8, DK), F32),
                   jnp.zeros((N_CTX_SEQ, n_even, HEADS, 8, 128), F32),
                   jnp.zeros((N_CTX_SEQ, n_even, 2, HEADS, DK, DK), F32)]
    gla_finals = jnp.zeros((N_CTX_SEQ, n_odd, 2, HEADS, DK, DV_GLA), F32)
    for layer in range(DEPTH):
        idx = layer // 2
        if layer % 2 == 0:
            w_main, w_gate, w_o = _even_weights(w_in_even[idx], w_out_even[idx])
            proj, gates = _proj(xn, w_main, w_gate, True, "proj_even")
            params = jnp.concatenate([mlstm_igate_b[idx].reshape(-1), mlstm_fgate_b[idx].reshape(-1),
                                      ret_decay_logit[idx].reshape(-1)])
            h_ctx, *even_finals = _even_mixer(proj, gates, params, 0, None, idx, even_finals)
            init = (state_mlstm_C, state_mlstm_n, state_mlstm_m, state_ret)
            (h_lat,) = _even_mixer(proj, gates, params, 1, init, idx)
        else:
            w_main, w_lr, wz, bz = _odd_weights(w_in_odd[idx], gla_gate_w2[idx], gla_gate_b[idx])
            w_o = w_out_odd[idx].astype(BF16)
            proj, lowrank = _proj(xn, w_main, w_lr, False, "proj_odd")
            h_ctx, gla_finals = _gla_mixer(proj, lowrank, wz, bz, 0, None, idx, gla_finals)
            (h_lat,) = _gla_mixer(proj, lowrank, wz, bz, 1, state_gla, idx)
        tail = functools.partial(_layer_tail, h_ctx, h_lat, w_o, ffn_w_in[layer].astype(BF16), ffn_conv_w, ffn_conv_b,
                                 ffn_w_out[layer].astype(BF16), x, mods, norm_mix_post, norm_ffn_pre,
                                 norm_ffn_post, norm_mix_pre, layer)
        n_ctx_tiles = N_CTX_TOK // FFN_TILE
        n_lat_tiles = N_TOK // FFN_TILE - n_ctx_tiles
        if layer < DEPTH - 1:
            x_new, xn_new = tail(0, n_ctx_tiles, x_spare, xn)
            x_new, xn = tail(n_ctx_tiles, n_lat_tiles, x_new, xn_new)
            x, x_spare = x_new, x
        else:
            (y_ctx,) = tail(0, n_ctx_tiles)
            (y_lat,) = tail(n_ctx_tiles, n_lat_tiles)

    y_prompt = y_ctx.reshape(N_CTX_SEQ, CTX_LEN, D)
    y_sample = y_lat.reshape(N_LAT_SEQ, LAT_LEN, D)
    new_c, new_n, new_m, new_r = even_finals
    return (y_prompt, y_sample, new_c, new_n[..., 0, :], new_m[:, :, :, 0:2, 0].transpose(0, 1, 3, 2), new_r, gla_finals)
```

```python
import functools

import jax
import jax.numpy as jnp
from jax import lax
from jax.experimental import pallas as pl
from jax.experimental.pallas import tpu as pltpu

F32 = jnp.float32
BF16 = jnp.bfloat16

D = 1024
DEPTH = 4
N_CTX_SEQ, CTX_LEN = 16, 256
N_LAT_SEQ, LAT_LEN = 2, 2048
GRID_W = 64
N_CTX_TOK = N_CTX_SEQ * CTX_LEN
N_TOK = N_CTX_TOK + N_LAT_SEQ * LAT_LEN
HEADS = 4
DK = 128
DV_GLA = 256
GLA_RANK = 16
GLA_TAU = 16.0
D_FF = 2816
EPS = 1e-6
NEG = -1e30

ROW_TILE = 256
PROJ_ROW_TILE = 1024
PROJ_COL_TILE = 1024
EVEN_CHUNK = 256
EVEN_UNROLL = 2
EVEN_CTX_SEQS = 4
GLA_CHUNK = 64
GLA_UNROLL = 4
GLA_CTX_SEQS = 4
GLA_OUT_UNROLL = 8
FF_CHUNK = 256
TAIL_TILE_CTX = 512
TAIL_TILE_LAT = 512
VMEM_LIMIT = 56 * 1024 * 1024


def _cparams(*sem):
    return pltpu.CompilerParams(dimension_semantics=sem, vmem_limit_bytes=VMEM_LIMIT)


def _mod_row(i, tile):
    n_ctx = N_CTX_TOK // tile
    return jnp.where(i < n_ctx, 0, 1 + (i - n_ctx) // (LAT_LEN // tile))


def _rms(x):
    return x * lax.rsqrt(jnp.mean(x * x, axis=-1, keepdims=True) + EPS)


def _log_sigmoid(x):
    return jnp.minimum(x, 0.0) - jnp.log(1.0 + jnp.exp(-jnp.abs(x)))


def _dot(a, b):
    return jnp.dot(a, b, preferred_element_type=F32)


def _dot_nt(a, b):
    return lax.dot_general(a, b, (((1,), (1,)), ((), ())), preferred_element_type=F32)


def _dot_tn(a, b):
    return lax.dot_general(a, b, (((0,), (0,)), ((), ())), preferred_element_type=F32)


def _seg_scan(x, seg, pos, op, ident, reverse):
    n = x.shape[0]
    s = 1
    while s < seg:
        if reverse:
            x = op(x, jnp.where(pos < seg - s, pltpu.roll(x, n - s, 0), ident))
        else:
            x = op(x, jnp.where(pos >= s, pltpu.roll(x, s, 0), ident))
        s *= 2
    return x


def _mod_kernel(c_ref, w_ref, b_ref, o_ref):
    c = c_ref[...]
    s = (c * jax.nn.sigmoid(c)).astype(BF16)
    o_ref[0] = _dot(s, w_ref[0].astype(BF16)) + b_ref[0]


def _mod_table(cond, ada_w, ada_b):
    tn = 2048
    out = pl.pallas_call(
        _mod_kernel,
        grid=(DEPTH, 6 * D // tn),
        in_specs=[pl.BlockSpec((8, D), lambda l, j: (0, 0)),
                  pl.BlockSpec((1, D, tn), lambda l, j: (l, 0, j)),
                  pl.BlockSpec((1, 1, tn), lambda l, j: (l, 0, j))],
        out_specs=pl.BlockSpec((1, 8, tn), lambda l, j: (l, 0, j)),
        out_shape=jax.ShapeDtypeStruct((DEPTH, 8, 6 * D), F32),
        compiler_params=_cparams("parallel", "parallel"),
        name="mod_table",
    )(cond, ada_w, ada_b.reshape(DEPTH, 1, 6 * D))
    return out.reshape(DEPTH, 8, 6, D)


def _prenorm_kernel(xp_ref, xs_ref, g_ref, mod_ref, x_ref, xn_ref):
    i = pl.program_id(0)

    def run(src_ref):
        x = src_ref[...]
        m = mod_ref[...]
        x_ref[...] = x
        xn_ref[...] = (_rms(x) * g_ref[...] * (1.0 + m[1:2]) + m[0:1]).astype(BF16)

    pl.when(i < N_CTX_TOK // ROW_TILE)(lambda: run(xp_ref))
    pl.when(i >= N_CTX_TOK // ROW_TILE)(lambda: run(xs_ref))


def _prenorm(xp, xs, g, mods, layer):
    nc = N_CTX_TOK // ROW_TILE
    return pl.pallas_call(
        _prenorm_kernel,
        grid=(N_TOK // ROW_TILE,),
        in_specs=[pl.BlockSpec((ROW_TILE, D), lambda i: (jnp.minimum(i, nc - 1), 0)),
                  pl.BlockSpec((ROW_TILE, D), lambda i: (jnp.maximum(i - nc, 0), 0)),
                  pl.BlockSpec((None, 1, D), lambda i: (layer, 0, 0)),
                  pl.BlockSpec((None, None, 6, D), lambda i: (layer, _mod_row(i, ROW_TILE), 0, 0))],
        out_specs=[pl.BlockSpec((ROW_TILE, D), lambda i: (i, 0)),
                   pl.BlockSpec((ROW_TILE, D), lambda i: (i, 0))],
        out_shape=[jax.ShapeDtypeStruct((N_TOK, D), F32), jax.ShapeDtypeStruct((N_TOK, D), BF16)],
        compiler_params=_cparams("parallel"),
        name="prenorm",
    )(xp, xs, g, mods)


def _proj_kernel(small_transposed, x_ref, w_ref, ws_ref, o_ref, os_ref):
    x = x_ref[...]
    o_ref[...] = _dot(x, w_ref[...]).astype(o_ref.dtype)

    @pl.when(pl.program_id(1) == 0)
    def _():
        os_ref[...] = _dot_nt(ws_ref[...], x) if small_transposed else _dot(x, ws_ref[...])


def _proj(xn, w_main, w_small, small_transposed, name):
    tm, tn = PROJ_ROW_TILE, PROJ_COL_TILE
    m, k = xn.shape
    n = w_main.shape[1]
    if small_transposed:
        r = w_small.shape[0]
        small_spec = pl.BlockSpec((r, tm), lambda i, j: (0, i))
        small_shape = jax.ShapeDtypeStruct((r, m), F32)
    else:
        small_spec = pl.BlockSpec((tm, w_small.shape[1]), lambda i, j: (i, 0))
        small_shape = jax.ShapeDtypeStruct((m, w_small.shape[1]), F32)
    return pl.pallas_call(
        functools.partial(_proj_kernel, small_transposed),
        grid=(m // tm, n // tn),
        in_specs=[pl.BlockSpec((tm, k), lambda i, j: (i, 0)),
                  pl.BlockSpec((k, tn), lambda i, j: (0, j)),
                  pl.BlockSpec(w_small.shape, lambda i, j: (0, 0))],
        out_specs=[pl.BlockSpec((tm, tn), lambda i, j: (i, j)), small_spec],
        out_shape=[jax.ShapeDtypeStruct((m, n), BF16), small_shape],
        compiler_params=_cparams("parallel", "arbitrary"),
        name=name,
    )(xn, w_main, w_small)


def _even_kernel(seq_len, n_seqs, init_layer, want_final, n_carried, out_layer, *refs):
    has_init = init_layer is not None
    layer_idx, n_layers = init_layer if has_init else (0, 1)
    chunk = EVEN_CHUNK
    n_chunks = seq_len // chunk
    total = n_seqs * n_chunks
    use_cross = has_init or n_chunks > 1
    unroll = min(EVEN_UNROLL, n_chunks)
    assert not has_init or n_seqs == 1
    it = iter(refs)
    par_ref = next(it)
    m0_ref = next(it) if has_init else None
    mq_ref, mk_ref, mv_ref, mo_ref, rq_ref, rk_ref, rv_ref, rg_ref, gate_ref = (next(it) for _ in range(9))
    if has_init:
        c0_ref, n0_ref, r0_ref = next(it), next(it), next(it)
    for _ in range(n_carried):
        next(it)
    out_ref = next(it)
    if want_final:
        finals = [next(it) for _ in range(4)]
        if n_carried == 0:
            for ref in finals:
                ref[...] = jnp.zeros(ref.shape, F32)
            finals = [ref.at[:, out_layer] for ref in finals]
        c_out_ref, n_out_ref, m_out_ref, r_out_ref = finals
    grow_ref, rep_ref, edge_ref, cn_ref, s_ref = (next(it) for _ in range(5))
    if use_cross:
        cns_ref, ss_ref, ms_ref = next(it), next(it), next(it)

    b = pl.program_id(0)
    h = pl.program_id(1)
    ii = lax.broadcasted_iota(jnp.int32, (chunk, chunk), 0)
    jj = lax.broadcasted_iota(jnp.int32, (chunk, chunk), 1)
    pos_col = lax.broadcasted_iota(jnp.int32, (chunk, 1), 0).astype(F32)
    pos_int = lax.broadcasted_iota(jnp.int32, (chunk, 1), 0)
    ones_blk = jnp.ones((chunk, 128), BF16)
    ones_sum = jnp.ones((256, 128), BF16)

    sub = lax.broadcasted_iota(jnp.int32, (8, 1), 0)
    bias = jnp.where(sub == 0, par_ref[h], jnp.where(sub == 1, par_ref[4 + h],
                     jnp.where(sub == 2, par_ref[8 + h], jnp.where(sub == 3, par_ref[12 + h], 0.0))))
    lane = lax.broadcasted_iota(jnp.int32, (1, 128), 1)
    tri_pre = (ii <= jj).astype(BF16)
    tri_suf = (ii >= jj).astype(BF16)

    def lanes(c):
        if isinstance(c, int):
            return slice(c * chunk, (c + 1) * chunk)
        return pl.ds(pl.multiple_of(c * chunk, chunk), chunk)

    def rows(c):
        if isinstance(c, int):
            return pl.ds(c * chunk, chunk)
        return pl.ds(pl.multiple_of(c * chunk, chunk), chunk)

    def prepare_chunk(c):
        pre = gate_ref[:, lanes(c)] + bias
        log_f = pltpu.roll(_log_sigmoid(pre), 6, 0)
        hi = log_f.astype(BF16)
        rest = log_f - hi.astype(F32)
        mid = rest.astype(BF16)
        low = (rest - mid.astype(F32)).astype(BF16)
        parts = jnp.concatenate([hi, mid, low, jnp.zeros_like(hi)], axis=0)
        cum_pre = _dot(parts, tri_pre)
        cum_suf = _dot(parts, tri_suf)
        fcum = jnp.where(sub == 0, cum_pre[0:8] + cum_pre[8:16] + cum_pre[16:24],
                         cum_suf[0:8] + cum_suf[8:16] + cum_suf[16:24])
        gsc = pre - fcum
        packed = jnp.where(sub < 2, gsc, pltpu.roll(fcum, 2, 0))
        grow_ref[:, lanes(c)] = packed
        pcol = packed.T
        g_rep = [jnp.broadcast_to(pcol[:, k:k + 1], (chunk, 128)) for k in range(4)]
        g_rep.append(_seg_scan(g_rep[0], chunk, pos_int, jnp.maximum, NEG, False))
        g_rep.append(_seg_scan(g_rep[1], chunk, pos_int, jnp.maximum, NEG, True))
        rep_ref[rows(c), :] = jnp.concatenate(g_rep, axis=1)
        f_tot = jnp.sum(log_f, axis=1, keepdims=True)
        g_top = jnp.max(gsc, axis=1, keepdims=True)
        edge_ref[c] = jnp.where(lane == 0, f_tot, jnp.where(lane == 1, g_top, 0.0))

    def for_all_chunks(body):
        if total <= 2 * EVEN_UNROLL:
            for c in range(total):
                body(c)
        else:
            def group(g, carry):
                for u in range(EVEN_UNROLL):
                    body(g * EVEN_UNROLL + u)
                return carry
            lax.fori_loop(0, total // EVEN_UNROLL, group, 0)

    for_all_chunks(prepare_chunk)

    k_scale = DK ** -0.5
    lg_f = _log_sigmoid(jnp.full((1, 1), par_ref[16 + h], F32))
    lg_b = _log_sigmoid(jnp.full((1, 1), par_ref[20 + h], F32))
    diff = (ii - jj).astype(F32)
    ret_mask = k_scale * (jnp.where(diff >= 0, jnp.exp(lg_f * jnp.maximum(diff, 0.0)), 0.0)
                          + jnp.where(diff <= 0, jnp.exp(lg_b * jnp.maximum(-diff, 0.0)), 0.0))
    rq_dec = (jnp.exp(lg_f * (pos_col + 1.0)), jnp.exp(lg_b * (chunk - pos_col)))
    rk_dec = (k_scale * jnp.exp(lg_f * (chunk - 1.0 - pos_col)), k_scale * jnp.exp(lg_b * pos_col))
    r_chunk = (jnp.exp(lg_f * chunk), jnp.exp(lg_b * chunk))

    def load_kv(c):
        return mk_ref[rows(c), :], jnp.concatenate([mv_ref[rows(c), :], ones_blk], axis=1)

    def ret_kv(c):
        return rk_ref[rows(c), :], rv_ref[rows(c), :]

    def advance(c, d, m, slot):
        if use_cross:
            cns_ref[c, :, 256 * d:256 * d + 256] = cn_ref[slot].astype(BF16)
            ss_ref[c, 128 * d:128 * d + 128, :] = s_ref[slot].astype(BF16)
            ms_ref[c, d:d + 1, :] = jnp.broadcast_to(m, (1, 128))
        edge = edge_ref[c]
        mu = jnp.maximum(m, edge[d:d + 1, 1:2])
        k, vext = load_kv(c)
        kw = (k.astype(F32) * (k_scale * jnp.exp(rep_ref[rows(c), 128 * d:128 * d + 128] - mu))).astype(BF16)
        cn_ref[slot] = jnp.exp(m - mu) * cn_ref[slot] + _dot_tn(kw, vext)
        rk, rv = ret_kv(c)
        s_ref[slot] = r_chunk[d] * s_ref[slot] + _dot_tn((rk.astype(F32) * rk_dec[d]).astype(BF16), rv)
        return edge[d:d + 1, 0:1] + mu

    if has_init:
        base = (b * n_layers + layer_idx) * 8
        m_init = (jnp.full((1, 1), m0_ref[base + h], F32), jnp.full((1, 1), m0_ref[base + 4 + h], F32))
        for d in range(2):
            cn_ref[d] = jnp.concatenate([c0_ref[d], jnp.broadcast_to(n0_ref[d], (DK, 128))], axis=1)
            s_ref[d] = r0_ref[d]
    else:
        m_init = (jnp.zeros((1, 1), F32), jnp.zeros((1, 1), F32))
        cn_ref[...] = jnp.zeros((2 * n_seqs, DK, 256), F32)
        s_ref[...] = jnp.zeros((2 * n_seqs, DK, DK), F32)

    m_fin = []
    for seq in range(n_seqs):
        def state_group(g, carry, seq=seq):
            m_f, m_b = carry
            for u in range(unroll):
                c = g * unroll + u
                m_f = advance(seq * n_chunks + c, 0, m_f, 2 * seq)
                m_b = advance(seq * n_chunks + n_chunks - 1 - c, 1, m_b, 2 * seq + 1)
            return m_f, m_b

        if n_chunks // unroll > 1:
            m_fin.append(lax.fori_loop(0, n_chunks // unroll, state_group, m_init))
        elif use_cross or want_final:
            m_fin.append(state_group(0, m_init))

    def rms_rep(x):
        sq = x * x
        hi = sq.astype(BF16)
        low = (sq - hi.astype(F32)).astype(BF16)
        ssum = _dot(jnp.concatenate([hi, low], axis=1), ones_sum)
        return x * lax.rsqrt(ssum * (1.0 / 128) + EPS)

    def output_chunk(c):
        q = mq_ref[rows(c), :]
        k, vext = load_kv(c)
        scores = _dot_nt(q, k) * k_scale
        cols = rep_ref[rows(c), :]
        f_f, f_b, gm_f, gm_b = cols[:, 256:384], cols[:, 384:512], cols[:, 512:640], cols[:, 640:768]
        row = grow_ref[:, lanes(c)]
        if use_cross:
            m_f = ms_ref[c, 0:1, :][:, 0:1]
            m_b = ms_ref[c, 1:2, :][:, 0:1]
        else:
            m_f, m_b = m_init
        mu_f = jnp.maximum(m_f, gm_f)
        mu_b = jnp.maximum(m_b, gm_b)
        p_f = jnp.exp(jnp.where(jj <= ii, row[0:1, :], NEG) - jnp.concatenate([mu_f, mu_f], axis=1)) * scores
        p_b = jnp.exp(jnp.where(jj >= ii, row[1:2, :], NEG) - jnp.concatenate([mu_b, mu_b], axis=1)) * scores
        tot = _dot(jnp.concatenate([p_f, p_b], axis=0).astype(BF16), vext)
        num_f, den_f, num_b, den_b = tot[:chunk, :128], tot[:chunk, 128:], tot[chunk:, :128], tot[chunk:, 128:]
        if use_cross:
            cross = _dot(q, cns_ref[c])
            w_f = jnp.exp(m_f - mu_f)
            w_b = jnp.exp(m_b - mu_b)
            num_f, den_f = num_f + w_f * cross[:, 0:128], den_f + w_f * cross[:, 128:256]
            num_b, den_b = num_b + w_b * cross[:, 256:384], den_b + w_b * cross[:, 384:512]
        h_f = num_f / jnp.maximum(jnp.abs(den_f), jnp.exp(-(f_f + mu_f)))
        h_b = num_b / jnp.maximum(jnp.abs(den_b), jnp.exp(-(f_b + mu_b)))
        out_m = rms_rep(h_f + h_b) * jax.nn.sigmoid(mo_ref[rows(c), :].astype(F32))
        rq = rq_ref[rows(c), :]
        rk, rv = ret_kv(c)
        o_r = _dot((_dot_nt(rq, rk) * ret_mask).astype(BF16), rv)
        if use_cross:
            rq32 = rq.astype(F32)
            q_in = jnp.concatenate([rq32 * rq_dec[0], rq32 * rq_dec[1]], axis=1).astype(BF16)
            o_r = o_r + _dot(q_in, ss_ref[c])
        rg = rg_ref[rows(c), :].astype(F32)
        out_r = rms_rep(o_r) * (rg * jax.nn.sigmoid(rg))
        out_ref[rows(c), :] = jnp.concatenate([out_m, out_r], axis=1).astype(BF16)

    for_all_chunks(output_chunk)

    if want_final:
        for seq in range(n_seqs):
            for d in range(2):
                cn = cn_ref[2 * seq + d]
                c_out_ref[seq, d] = cn[:, :DK]
                n_out_ref[seq, d] = cn[:, DK:].T[0:8, :]
                r_out_ref[seq, d] = s_ref[2 * seq + d]
            m_out_ref[seq] = jnp.concatenate([jnp.broadcast_to(m_fin[seq][0], (1, 128)),
                                              jnp.broadcast_to(m_fin[seq][1], (1, 128)), jnp.zeros((6, 128), F32)], axis=0)


def _even_mixer(proj, gates_t, params, group, init, layer_idx=0, carried=None):
    ctx = group == 0
    n_seq, seq_len = (N_CTX_SEQ, CTX_LEN) if ctx else (N_LAT_SEQ, LAT_LEN)
    row_off = 0 if ctx else N_CTX_TOK // LAT_LEN
    has_init, want_final = not ctx, ctx
    n_seqs = EVEN_CTX_SEQS if ctx else 1
    rows = n_seqs * seq_len
    n_chunks = seq_len // EVEN_CHUNK
    total = n_seqs * n_chunks
    use_cross = has_init or n_chunks > 1

    def col(k):
        return pl.BlockSpec((rows, 128), lambda b, h: (row_off + b, 4 * k + h))

    smem = pl.BlockSpec(memory_space=pltpu.SMEM)
    in_specs = [smem]
    args = [params]
    if has_init:
        c0, n0, m0, r0 = init
        n_layers = c0.shape[1]
        in_specs.append(smem)
        args.append(m0.reshape(-1))
    in_specs += [col(k) for k in range(8)]
    in_specs.append(pl.BlockSpec((8, rows), lambda b, h: (h, row_off + b)))
    args += [proj] * 8 + [gates_t]
    if has_init:
        in_specs += [pl.BlockSpec((None, None, 2, None, DK, 128), lambda b, h: (b, layer_idx, 0, h, 0, 0)),
                     pl.BlockSpec((None, None, 2, None, DK, 1), lambda b, h: (b, layer_idx, 0, h, 0, 0)),
                     pl.BlockSpec((None, None, 2, None, DK, 128), lambda b, h: (b, layer_idx, 0, h, 0, 0))]
        args += [c0, n0.reshape(n0.shape + (1,)), r0]
    out_specs = [pl.BlockSpec((rows, 256), lambda b, h: (b, h))]
    out_shape = [jax.ShapeDtypeStruct((n_seq * seq_len, D), BF16)]
    aliases = {}
    if want_final:
        n_even = (DEPTH + 1) // 2
        if carried is None:
            lay, at = n_even, 0
        else:
            lay, at = None, layer_idx
        out_specs += [pl.BlockSpec((n_seqs, lay, 2, None, DK, DK), lambda b, h: (b, at, 0, h, 0, 0)),
                      pl.BlockSpec((n_seqs, lay, 2, None, 8, DK), lambda b, h: (b, at, 0, h, 0, 0)),
                      pl.BlockSpec((n_seqs, lay, None, 8, 128), lambda b, h: (b, at, h, 0, 0)),
                      pl.BlockSpec((n_seqs, lay, 2, None, DK, DK), lambda b, h: (b, at, 0, h, 0, 0))]
        out_shape += [jax.ShapeDtypeStruct((n_seq, n_even, 2, HEADS, DK, DK), F32),
                      jax.ShapeDtypeStruct((n_seq, n_even, 2, HEADS, 8, DK), F32),
                      jax.ShapeDtypeStruct((n_seq, n_even, HEADS, 8, 128), F32),
                      jax.ShapeDtypeStruct((n_seq, n_even, 2, HEADS, DK, DK), F32)]
        if carried is not None:
            aliases = {len(args) + k: 1 + k for k in range(len(carried))}
            in_specs += [pl.BlockSpec(memory_space=pl.ANY)] * len(carried)
            args += list(carried)
    scratch = [pltpu.VMEM((8, rows), F32), pltpu.VMEM((rows, 768), F32), pltpu.VMEM((total, 8, 128), F32),
               pltpu.VMEM((2 * n_seqs, DK, 256), F32), pltpu.VMEM((2 * n_seqs, DK, DK), F32)]
    if use_cross:
        scratch += [pltpu.VMEM((total, DK, 512), BF16), pltpu.VMEM((total, 2 * DK, DK), BF16),
                    pltpu.VMEM((total, 8, 128), F32)]
    return pl.pallas_call(
        functools.partial(_even_kernel, seq_len, n_seqs, (layer_idx, n_layers) if has_init else None, want_final,
                          len(aliases), layer_idx),
        grid=(n_seq // n_seqs, HEADS),
        in_specs=in_specs,
        out_specs=out_specs,
        out_shape=out_shape,
        input_output_aliases=aliases,
        scratch_shapes=scratch,
        compiler_params=_cparams("parallel", "parallel"),
        name="even_mixer_ctx" if ctx else "even_mixer_lat",
    )(*args)


def _gla_kernel(seq_len, n_seqs, has_init, want_final, n_carried, out_layer, *refs):
    chunk = GLA_CHUNK
    n_chunks = seq_len // chunk
    total = n_seqs * n_chunks
    n_rows = n_seqs * seq_len
    assert not has_init or n_seqs == 1
    it = iter(refs)
    q_ref, k_ref, v_ref, gr_ref, lr_ref, wz_ref, bz_ref = (next(it) for _ in range(7))
    s0_ref = next(it) if has_init else None
    for _ in range(n_carried):
        next(it)
    out_ref = next(it)
    s_out_ref = next(it) if want_final else None
    if want_final and n_carried == 0:
        s_out_ref[...] = jnp.zeros(s_out_ref.shape, F32)
        s_out_ref = s_out_ref.at[:, out_layer]
    qe_ref, ke_ref, cum_ref, sts_ref, st_ref = (next(it) for _ in range(5))
    unroll = min(GLA_UNROLL, n_chunks)

    x4 = lr_ref[...]
    hi = x4.astype(BF16).astype(F32)
    mid = (x4 - hi).astype(BF16).astype(F32)
    low = ((x4 - hi) - mid).astype(BF16).astype(F32)
    grp = lax.broadcasted_iota(jnp.int32, (1, 128), 1) // (2 * GLA_RANK)
    lhs = jnp.concatenate([jnp.where(grp == 1, mid, jnp.where(grp == 2, low, hi)),
                           jnp.where(grp == 0, mid, jnp.where(grp == 1, hi, 0.0))], axis=1).astype(BF16)
    z = _dot(lhs, wz_ref[...]) + bz_ref[...]
    la = _log_sigmoid(z) / GLA_TAU
    pos = lax.broadcasted_iota(jnp.int32, (n_rows, 1), 0) % chunk
    b_f = _seg_scan(la[:, :128], chunk, pos, jnp.add, 0.0, False)
    b_b = _seg_scan(la[:, 128:], chunk, pos, jnp.add, 0.0, True)
    q = q_ref[...].astype(F32) * (DK ** -0.5)
    k = k_ref[...].astype(F32)
    qe_ref[...] = jnp.concatenate([q * jnp.exp(b_f), q * jnp.exp(b_b)], axis=1).astype(BF16)
    ke_ref[...] = jnp.concatenate([k * jnp.exp(-b_f), k * jnp.exp(-b_b)], axis=1).astype(BF16)
    cum_ref[...] = jnp.concatenate([b_f, b_b], axis=1)

    ii = lax.broadcasted_iota(jnp.int32, (chunk, chunk), 0)
    jj = lax.broadcasted_iota(jnp.int32, (chunk, chunk), 1)

    def rows(c):
        if isinstance(c, int):
            return pl.ds(c * chunk, chunk)
        return pl.ds(pl.multiple_of(c * chunk, chunk), chunk)

    def advance(c, d, slot):
        lanes = slice(128 * d, 128 * d + 128)
        edge = c * chunk + (chunk - 1 if d == 0 else 0)
        decay = cum_ref[pl.ds(edge, 1), :][:, lanes]
        kw = (k_ref[rows(c), :].astype(F32) * jnp.exp(decay - cum_ref[rows(c), :][:, lanes])).astype(BF16)
        st = st_ref[slot]
        sts_ref[c, :, lanes] = st.astype(BF16)
        st_ref[slot] = st * jnp.exp(decay) + _dot_tn(v_ref[rows(c), :], kw)

    if has_init:
        st_ref[0] = s0_ref[0].T
        st_ref[1] = s0_ref[1].T
    else:
        st_ref[...] = jnp.zeros((2 * n_seqs, DV_GLA, DK), F32)

    for seq in range(n_seqs):
        def state_group(g, carry, seq=seq):
            for u in range(unroll):
                c = g * unroll + u
                advance(seq * n_chunks + c, 0, 2 * seq)
                advance(seq * n_chunks + n_chunks - 1 - c, 1, 2 * seq + 1)
            return carry

        if n_chunks // unroll == 1:
            state_group(0, 0)
        else:
            lax.fori_loop(0, n_chunks // unroll, state_group, 0)

    out_unroll = min(GLA_OUT_UNROLL, total)

    def output_group(g, carry):
        for u in range(out_unroll):
            c = g * out_unroll + u
            qe = qe_ref[rows(c), :]
            ke = ke_ref[rows(c), :]
            att = (jnp.where(jj <= ii, _dot_nt(qe[:, :128], ke[:, :128]), 0.0)
                   + jnp.where(jj >= ii, _dot_nt(qe[:, 128:], ke[:, 128:]), 0.0))
            o = _dot(att.astype(BF16), v_ref[rows(c), :]) + _dot_nt(qe, sts_ref[c])
            gr = gr_ref[rows(c), :].astype(F32)
            out_ref[rows(c), :] = (_rms(o) * (gr * jax.nn.sigmoid(gr))).astype(BF16)
        return carry

    if total // out_unroll == 1:
        output_group(0, 0)
    else:
        lax.fori_loop(0, total // out_unroll, output_group, 0)
    if want_final:
        for seq in range(n_seqs):
            s_out_ref[seq, 0] = st_ref[2 * seq].T
            s_out_ref[seq, 1] = st_ref[2 * seq + 1].T


def _gla_mixer(proj, lowrank, wz, bz, group, init, layer_idx=0, carried=None):
    ctx = group == 0
    n_seq, seq_len = (N_CTX_SEQ, CTX_LEN) if ctx else (N_LAT_SEQ, LAT_LEN)
    row_off = 0 if ctx else N_CTX_TOK // LAT_LEN
    has_init, want_final = not ctx, ctx
    n_seqs = GLA_CTX_SEQS if ctx else 1
    n_rows = n_seqs * seq_len
    n_chunks = seq_len // GLA_CHUNK
    total = n_seqs * n_chunks
    in_specs = [pl.BlockSpec((n_rows, 128), lambda b, h: (row_off + b, h)),
                pl.BlockSpec((n_rows, 128), lambda b, h: (row_off + b, 4 + h)),
                pl.BlockSpec((n_rows, 256), lambda b, h: (row_off + b, 4 + h)),
                pl.BlockSpec((n_rows, 256), lambda b, h: (row_off + b, 8 + h)),
                pl.BlockSpec((n_rows, 128), lambda b, h: (row_off + b, 0)),
                pl.BlockSpec((None, 256, 256), lambda b, h: (h, 0, 0)),
                pl.BlockSpec((None, 1, 256), lambda b, h: (h, 0, 0))]
    args = [proj, proj, proj, proj, lowrank, wz, bz]
    if has_init:
        in_specs.append(pl.BlockSpec((None, None, 2, None, DK, DV_GLA), lambda b, h: (b, layer_idx, 0, h, 0, 0)))
        args.append(init)
    out_specs = [pl.BlockSpec((n_rows, 256), lambda b, h: (b, h))]
    out_shape = [jax.ShapeDtypeStruct((n_seq * seq_len, D), BF16)]
    aliases = {}
    if want_final:
        lay, at = (DEPTH // 2, 0) if carried is None else (None, layer_idx)
        out_specs.append(pl.BlockSpec((n_seqs, lay, 2, None, DK, DV_GLA), lambda b, h: (b, at, 0, h, 0, 0)))
        out_shape.append(jax.ShapeDtypeStruct((n_seq, DEPTH // 2, 2, HEADS, DK, DV_GLA), F32))
        if carried is not None:
            aliases = {len(args): 1}
            in_specs.append(pl.BlockSpec(memory_space=pl.ANY))
            args.append(carried)
    scratch = [pltpu.VMEM((n_rows, 256), BF16)] * 2 + [pltpu.VMEM((n_rows, 256), F32),
                                                       pltpu.VMEM((total, DV_GLA, 2 * DK), BF16),
                                                       pltpu.VMEM((2 * n_seqs, DV_GLA, DK), F32)]
    return pl.pallas_call(
        functools.partial(_gla_kernel, seq_len, n_seqs, has_init, want_final, len(aliases), layer_idx),
        grid=(n_seq // n_seqs, HEADS),
        in_specs=in_specs,
        out_specs=out_specs,
        out_shape=out_shape,
        input_output_aliases=aliases,
        scratch_shapes=scratch,
        compiler_params=_cparams("parallel", "parallel"),
        name="gla_mixer_ctx" if ctx else "gla_mixer_lat",
    )(*args)


def _tail_kernel(last, tile, tile_off, n_tiles, n_carried, *refs):
    it = iter(refs)
    hc_ref, hl_ref, hup_ref, hdn_ref, x_ref, xup_ref, xdn_ref = (next(it) for _ in range(7))
    wo_ref, win_ref, cw_ref, cb_ref, w_ref, mod_ref, g_mix_ref, g_pre_ref, g_post_ref = (next(it) for _ in range(9))
    if not last:
        g_next_ref, mod_next_ref = next(it), next(it)
    for _ in range(n_carried):
        next(it)
    x_out_ref = next(it)
    xn_out_ref = None if last else next(it)
    hbuf_ref = next(it)
    i = pl.program_id(0) + tile_off
    n_ctx = N_CTX_TOK // tile
    tiles_per_seq = LAT_LEN // tile
    n_chunks = D_FF // FF_CHUNK

    def cols(f):
        return slice(f * FF_CHUNK, (f + 1) * FF_CHUNK), slice(D_FF + f * FF_CHUNK, D_FF + (f + 1) * FF_CHUNK)

    def act(conv, v, sl):
        return (jax.nn.gelu(conv + cb_ref[:, sl]) * v).astype(BF16)

    def mixer_out(h, x, keep=None):
        m = mod_ref[...]
        x_mid = x + m[2:3] * (_rms(_dot(h, wo_ref[...])) * g_mix_ref[...])
        xn = _rms(x_mid) * g_pre_ref[...] * (1.0 + m[4:5]) + m[3:4]
        if keep is not None:
            xn = jnp.where(keep, xn, 0.0)
        return x_mid, xn.astype(BF16)

    def finish(x_mid):
        x_new = x_mid + mod_ref[5:6, :] * (_rms(_dot(hbuf_ref[...], w_ref[...])) * g_post_ref[...])
        x_out_ref[...] = x_new
        if not last:
            mn = mod_next_ref[...]
            xn_out_ref[...] = (_rms(x_new) * g_next_ref[...] * (1.0 + mn[1:2]) + mn[0:1]).astype(BF16)

    def ctx_branch():
        x_mid, xn = mixer_out(hc_ref[...], x_ref[...])
        pos = lax.broadcasted_iota(jnp.int32, (tile, 1), 0) % CTX_LEN
        for f in range(n_chunks):
            sl, sv = cols(f)
            a = _dot(xn, win_ref[:, sl])
            left = jnp.where(pos >= 1, pltpu.roll(a, 1, 0), 0.0)
            right = jnp.where(pos < CTX_LEN - 1, pltpu.roll(a, tile - 1, 0), 0.0)
            conv = cw_ref[3:4, sl] * left + cw_ref[4:5, sl] * a + cw_ref[5:6, sl] * right
            hbuf_ref[:, sl] = act(conv, _dot(xn, win_ref[:, sv]), sl)
        finish(x_mid)

    def lat_branch():
        t = (i - n_ctx) % tiles_per_seq
        n_rows = tile + 2 * GRID_W
        rowpos = lax.broadcasted_iota(jnp.int32, (n_rows, 1), 0)
        first = jnp.where(t > 0, 0, GRID_W)
        stop = jnp.where(t < tiles_per_seq - 1, n_rows, GRID_W + tile)
        x_mid, xe = mixer_out(jnp.concatenate([hup_ref[...], hl_ref[...], hdn_ref[...]], axis=0),
                              jnp.concatenate([xup_ref[...], x_ref[...], xdn_ref[...]], axis=0),
                              (rowpos >= first) & (rowpos < stop))
        x_mid = x_mid[GRID_W:GRID_W + tile]
        xn = xe[GRID_W:GRID_W + tile]
        colpos = rowpos % GRID_W
        for f in range(n_chunks):
            sl, sv = cols(f)
            buf = _dot(xe, win_ref[:, sl])
            left = jnp.where(colpos >= 1, pltpu.roll(buf, 1, 0), 0.0)
            right = jnp.where(colpos < GRID_W - 1, pltpu.roll(buf, n_rows - 1, 0), 0.0)
            conv = None
            for dr in range(3):
                rs = slice(dr * GRID_W, dr * GRID_W + tile)
                term = (cw_ref[3 * dr:3 * dr + 1, sl] * left[rs] + cw_ref[3 * dr + 1:3 * dr + 2, sl] * buf[rs]
                        + cw_ref[3 * dr + 2:3 * dr + 3, sl] * right[rs])
                conv = term if conv is None else conv + term
            hbuf_ref[:, sl] = act(conv, _dot(xn, win_ref[:, sv]), sl)
        finish(x_mid)

    if tile_off + n_tiles <= n_ctx:
        ctx_branch()
    elif tile_off >= n_ctx:
        lat_branch()
    else:
        pl.when(i < n_ctx)(ctx_branch)
        pl.when(i >= n_ctx)(lat_branch)


def _layer_tail(h_ctx, h_lat, w_o, w_in, conv_w, conv_b, w_out, x, mods, g_mix_post, g_ffn_pre, g_post, g_next,
                layer, group, x_dst=None, xn_dst=None):
    last = layer == DEPTH - 1
    tile = TAIL_TILE_CTX if group == 0 else TAIL_TILE_LAT
    tile_off = 0 if group == 0 else N_CTX_TOK // tile
    n_tiles = (N_CTX_TOK if group == 0 else N_TOK - N_CTX_TOK) // tile
    halo_per_tile = tile // GRID_W
    n_halo = N_TOK // GRID_W
    n_ctx = N_CTX_TOK // tile
    n_lat_halo = (N_TOK - N_CTX_TOK) // GRID_W
    row = pl.BlockSpec((tile, D), lambda i: (i + tile_off, 0))
    resident = dict(pipeline_mode=pl.Buffered(1))
    layer_vec = pl.BlockSpec((None, 1, D), lambda i: (layer, 0, 0))

    def lat_halo(i, shift):
        return jnp.clip((i + tile_off - n_ctx + shift) * halo_per_tile - (1 - shift), 0, n_lat_halo - 1)

    h_ctx_spec = (pl.BlockSpec((tile, D), lambda i: (i, 0)) if group == 0
                  else pl.BlockSpec((GRID_W, D), lambda i: (0, 0)))
    h_lat_spec = (pl.BlockSpec((tile, D), lambda i: (i, 0)) if group == 1
                  else pl.BlockSpec((GRID_W, D), lambda i: (0, 0)))
    in_specs = [h_ctx_spec, h_lat_spec,
                pl.BlockSpec((GRID_W, D), lambda i: (lat_halo(i, 0), 0)),
                pl.BlockSpec((GRID_W, D), lambda i: (lat_halo(i, 1), 0)),
                row,
                pl.BlockSpec((GRID_W, D), lambda i: (jnp.maximum((i + tile_off) * halo_per_tile - 1, 0), 0)),
                pl.BlockSpec((GRID_W, D), lambda i: (jnp.minimum((i + tile_off + 1) * halo_per_tile, n_halo - 1), 0)),
                pl.BlockSpec((D, D), lambda i: (0, 0), **resident),
                pl.BlockSpec((D, 2 * D_FF), lambda i: (0, 0), **resident),
                pl.BlockSpec((None, 9, D_FF), lambda i: (layer, 0, 0)),
                pl.BlockSpec((None, 1, D_FF), lambda i: (layer, 0, 0)),
                pl.BlockSpec((D_FF, D), lambda i: (0, 0), **resident),
                pl.BlockSpec((None, None, 6, D), lambda i: (layer, _mod_row(i + tile_off, tile), 0, 0)),
                layer_vec, layer_vec, layer_vec]
    args = [h_ctx, h_lat, h_lat, h_lat, x, x, x, w_o, w_in, conv_w.reshape(DEPTH, 9, D_FF), conv_b, w_out, mods,
            g_mix_post, g_ffn_pre, g_post]
    aliases = {}
    if last:
        out_specs = [pl.BlockSpec((tile, D), lambda i: (i, 0))]
        out_shape = [jax.ShapeDtypeStruct((n_tiles * tile, D), F32)]
    else:
        in_specs += [pl.BlockSpec((None, 1, D), lambda i: (layer + 1, 0, 0)),
                     pl.BlockSpec((None, None, 6, D), lambda i: (layer + 1, _mod_row(i + tile_off, tile), 0, 0))]
        args += [g_next, mods]
        aliases = {len(args): 0, len(args) + 1: 1}
        in_specs += [pl.BlockSpec(memory_space=pl.ANY)] * 2
        args += [x_dst, xn_dst]
        out_specs = [row, row]
        out_shape = [jax.ShapeDtypeStruct((N_TOK, D), F32), jax.ShapeDtypeStruct((N_TOK, D), BF16)]
    return pl.pallas_call(
        functools.partial(_tail_kernel, last, tile, tile_off, n_tiles, len(aliases)),
        grid=(n_tiles,),
        in_specs=in_specs,
        out_specs=out_specs,
        out_shape=out_shape,
        input_output_aliases=aliases,
        scratch_shapes=[pltpu.VMEM((tile, D_FF), BF16)],
        compiler_params=_cparams("parallel"),
        name="layer_tail",
    )(*args)


def _even_weights(w_in, w_out):
    w_main = jnp.concatenate([w_in[:, :2048], w_in[:, 2064:]], axis=1).astype(BF16)
    src = [2048] * (8 * HEADS)
    used = [0.0] * (8 * HEADS)
    for h in range(HEADS):
        for k, col in enumerate((h, 8 + h, 4 + h, 12 + h)):
            src[8 * h + k] = 2048 + col
            used[8 * h + k] = 1.0
    w_gate = (w_in[:, jnp.array(src)] * jnp.array(used, F32)).T.astype(BF16)
    w_o = w_out.reshape(2, HEADS, 128, D).transpose(1, 0, 2, 3).reshape(D, D).astype(BF16)
    return w_main, w_gate, w_o


def _odd_weights(w_in, w2, b2):
    w_main = w_in[:, :3072].astype(BF16)
    w_lr = jnp.tile(w_in[:, 3072:3072 + 2 * GLA_RANK], (1, 128 // (2 * GLA_RANK))).astype(BF16)
    w2h = w2.reshape(2, GLA_RANK, HEADS, DK).transpose(2, 0, 1, 3)
    w32 = jnp.zeros((HEADS, 2 * GLA_RANK, 256), F32)
    w32 = w32.at[:, :GLA_RANK, :DK].set(w2h[:, 0]).at[:, GLA_RANK:, DK:].set(w2h[:, 1])
    hi = w32.astype(BF16)
    mid = (w32 - hi.astype(F32)).astype(BF16)
    low = (w32 - hi.astype(F32) - mid.astype(F32)).astype(BF16)
    none = jnp.zeros_like(hi)
    wz = jnp.concatenate([hi, hi, hi, mid, mid, low, none, none], axis=1)
    bz = b2.reshape(2, HEADS, DK).transpose(1, 0, 2).reshape(HEADS, 1, 2 * DK)
    return w_main, w_lr, wz, bz


def kernel(x_prompt, x_sample, c, c_ctx, state_mlstm_C, state_mlstm_n, state_mlstm_m, state_ret, state_gla, ada_w, ada_b, norm_mix_pre, norm_mix_post, norm_ffn_pre, norm_ffn_post, w_in_even, w_out_even, mlstm_igate_b, mlstm_fgate_b, ret_decay_logit, w_in_odd, gla_gate_w2, gla_gate_b, w_out_odd, ffn_w_in, ffn_conv_w, ffn_conv_b, ffn_w_out):
    cond = jnp.concatenate([c_ctx[None, :], c, jnp.zeros((8 - 1 - N_LAT_SEQ, D), F32)], axis=0)
    mods = _mod_table(cond, ada_w, ada_b)
    norm_mix_pre, norm_mix_post, norm_ffn_pre, norm_ffn_post = (
        g.reshape(DEPTH, 1, D) for g in (norm_mix_pre, norm_mix_post, norm_ffn_pre, norm_ffn_post))
    ffn_conv_b = ffn_conv_b.reshape(DEPTH, 1, D_FF)

    x, xn = _prenorm(x_prompt.reshape(N_CTX_TOK, D), x_sample.reshape(N_LAT_SEQ * LAT_LEN, D),
                     norm_mix_pre, mods, 0)
    x_spare = jnp.zeros((N_TOK, D), F32)
    even_finals, gla_finals = None, None
    for layer in range(DEPTH):
        idx = layer // 2
        if layer % 2 == 0:
            w_main, w_gate, w_o = _even_weights(w_in_even[idx], w_out_even[idx])
            proj, gates = _proj(xn, w_main, w_gate, True, "proj_even")
            params = jnp.concatenate([mlstm_igate_b[idx].reshape(-1), mlstm_fgate_b[idx].reshape(-1),
                                      ret_decay_logit[idx].reshape(-1)])
            h_ctx, *even_finals = _even_mixer(proj, gates, params, 0, None, idx, even_finals)
            init = (state_mlstm_C, state_mlstm_n, state_mlstm_m, state_ret)
            (h_lat,) = _even_mixer(proj, gates, params, 1, init, idx)
        else:
            w_main, w_lr, wz, bz = _odd_weights(w_in_odd[idx], gla_gate_w2[idx], gla_gate_b[idx])
            w_o = w_out_odd[idx].astype(BF16)
            proj, lowrank = _proj(xn, w_main, w_lr, False, "proj_odd")
            h_ctx, gla_finals = _gla_mixer(proj, lowrank, wz, bz, 0, None, idx, gla_finals)
            (h_lat,) = _gla_mixer(proj, lowrank, wz, bz, 1, state_gla, idx)
        tail = functools.partial(_layer_tail, h_ctx, h_lat, w_o, ffn_w_in[layer].astype(BF16), ffn_conv_w, ffn_conv_b,
                                 ffn_w_out[layer].astype(BF16), x, mods, norm_mix_post, norm_ffn_pre,
                                 norm_ffn_post, norm_mix_pre, layer)
        if layer < DEPTH - 1:
            x_new, xn_new = tail(0, x_spare, xn)
            x_new, xn = tail(1, x_new, xn_new)
            x, x_spare = x_new, x
        else:
            (y_ctx,) = tail(0)
            (y_lat,) = tail(1)

    y_prompt = y_ctx.reshape(N_CTX_SEQ, CTX_LEN, D)
    y_sample = y_lat.reshape(N_LAT_SEQ, LAT_LEN, D)
    new_c, new_n, new_m, new_r = even_finals
    return (y_prompt, y_sample, new_c, new_n[..., 0, :], new_m[:, :, :, 0:2, 0].transpose(0, 1, 3, 2), new_r, gla_finals)
```

```python
import functools

import jax
import jax.numpy as jnp
from jax import lax
from jax.experimental import pallas as pl
from jax.experimental.pallas import tpu as pltpu

F32 = jnp.float32
BF16 = jnp.bfloat16

D = 1024
DEPTH = 4
N_CTX_SEQ, CTX_LEN = 16, 256
N_LAT_SEQ, LAT_LEN = 2, 2048
GRID_W = 64
N_CTX_TOK = N_CTX_SEQ * CTX_LEN
N_TOK = N_CTX_TOK + N_LAT_SEQ * LAT_LEN
HEADS = 4
DK = 128
DV_GLA = 256
GLA_RANK = 16
GLA_TAU = 16.0
D_FF = 2816
EPS = 1e-6
NEG = -1e30

ROW_TILE = 256
PROJ_ROW_TILE = 1024
PROJ_COL_TILE = 1024
EVEN_CHUNK = 256
EVEN_UNROLL = 4
EVEN_CTX_SEQS = 4
GLA_CHUNK = 64
GLA_UNROLL = 4
GLA_CTX_SEQS = 4
GLA_OUT_UNROLL = 8
FF_CHUNK = 256
TAIL_TILE_CTX = 512
TAIL_TILE_LAT = 512
VMEM_LIMIT = 56 * 1024 * 1024


def _cparams(*sem):
    return pltpu.CompilerParams(dimension_semantics=sem, vmem_limit_bytes=VMEM_LIMIT)


def _mod_row(i, tile):
    n_ctx = N_CTX_TOK // tile
    return jnp.where(i < n_ctx, 0, 1 + (i - n_ctx) // (LAT_LEN // tile))


def _rms(x):
    return x * lax.rsqrt(jnp.mean(x * x, axis=-1, keepdims=True) + EPS)


def _log_sigmoid(x):
    return jnp.minimum(x, 0.0) - jnp.log(1.0 + jnp.exp(-jnp.abs(x)))


def _dot(a, b):
    return jnp.dot(a, b, preferred_element_type=F32)


def _dot_nt(a, b):
    return lax.dot_general(a, b, (((1,), (1,)), ((), ())), preferred_element_type=F32)


def _dot_tn(a, b):
    return lax.dot_general(a, b, (((0,), (0,)), ((), ())), preferred_element_type=F32)


def _seg_scan(x, seg, pos, op, ident, reverse):
    n = x.shape[0]
    s = 1
    while s < seg:
        if reverse:
            x = op(x, jnp.where(pos < seg - s, pltpu.roll(x, n - s, 0), ident))
        else:
            x = op(x, jnp.where(pos >= s, pltpu.roll(x, s, 0), ident))
        s *= 2
    return x


def _mod_kernel(c_ref, w_ref, b_ref, o_ref):
    c = c_ref[...]
    s = (c * jax.nn.sigmoid(c)).astype(BF16)
    o_ref[0] = _dot(s, w_ref[0].astype(BF16)) + b_ref[0]


def _mod_table(cond, ada_w, ada_b):
    tn = 2048
    out = pl.pallas_call(
        _mod_kernel,
        grid=(DEPTH, 6 * D // tn),
        in_specs=[pl.BlockSpec((8, D), lambda l, j: (0, 0)),
                  pl.BlockSpec((1, D, tn), lambda l, j: (l, 0, j)),
                  pl.BlockSpec((1, 1, tn), lambda l, j: (l, 0, j))],
        out_specs=pl.BlockSpec((1, 8, tn), lambda l, j: (l, 0, j)),
        out_shape=jax.ShapeDtypeStruct((DEPTH, 8, 6 * D), F32),
        compiler_params=_cparams("parallel", "parallel"),
        name="mod_table",
    )(cond, ada_w, ada_b.reshape(DEPTH, 1, 6 * D))
    return out.reshape(DEPTH, 8, 6, D)


def _prenorm_kernel(xp_ref, xs_ref, g_ref, mod_ref, x_ref, xn_ref):
    i = pl.program_id(0)

    def run(src_ref):
        x = src_ref[...]
        m = mod_ref[...]
        x_ref[...] = x
        xn_ref[...] = (_rms(x) * g_ref[...] * (1.0 + m[1:2]) + m[0:1]).astype(BF16)

    pl.when(i < N_CTX_TOK // ROW_TILE)(lambda: run(xp_ref))
    pl.when(i >= N_CTX_TOK // ROW_TILE)(lambda: run(xs_ref))


def _prenorm(xp, xs, g, mods, layer):
    nc = N_CTX_TOK // ROW_TILE
    return pl.pallas_call(
        _prenorm_kernel,
        grid=(N_TOK // ROW_TILE,),
        in_specs=[pl.BlockSpec((ROW_TILE, D), lambda i: (jnp.minimum(i, nc - 1), 0)),
                  pl.BlockSpec((ROW_TILE, D), lambda i: (jnp.maximum(i - nc, 0), 0)),
                  pl.BlockSpec((None, 1, D), lambda i: (layer, 0, 0)),
                  pl.BlockSpec((None, None, 6, D), lambda i: (layer, _mod_row(i, ROW_TILE), 0, 0))],
        out_specs=[pl.BlockSpec((ROW_TILE, D), lambda i: (i, 0)),
                   pl.BlockSpec((ROW_TILE, D), lambda i: (i, 0))],
        out_shape=[jax.ShapeDtypeStruct((N_TOK, D), F32), jax.ShapeDtypeStruct((N_TOK, D), BF16)],
        compiler_params=_cparams("parallel"),
        name="prenorm",
    )(xp, xs, g, mods)


def _proj_kernel(small_transposed, x_ref, w_ref, ws_ref, o_ref, os_ref):
    x = x_ref[...]
    o_ref[...] = _dot(x, w_ref[...]).astype(o_ref.dtype)

    @pl.when(pl.program_id(1) == 0)
    def _():
        os_ref[...] = _dot_nt(ws_ref[...], x) if small_transposed else _dot(x, ws_ref[...])


def _proj(xn, w_all, layer_idx, n, w_small, small_transposed, name):
    tm, tn = PROJ_ROW_TILE, PROJ_COL_TILE
    m, k = xn.shape
    if small_transposed:
        r = w_small.shape[0]
        small_spec = pl.BlockSpec((r, tm), lambda i, j: (0, i))
        small_shape = jax.ShapeDtypeStruct((r, m), F32)
    else:
        small_spec = pl.BlockSpec((tm, w_small.shape[1]), lambda i, j: (i, 0))
        small_shape = jax.ShapeDtypeStruct((m, w_small.shape[1]), F32)
    return pl.pallas_call(
        functools.partial(_proj_kernel, small_transposed),
        grid=(m // tm, n // tn),
        in_specs=[pl.BlockSpec((tm, k), lambda i, j: (i, 0)),
                  pl.BlockSpec((None, k, tn), lambda i, j: (layer_idx, 0, j)),
                  pl.BlockSpec(w_small.shape, lambda i, j: (0, 0))],
        out_specs=[pl.BlockSpec((tm, tn), lambda i, j: (i, j)), small_spec],
        out_shape=[jax.ShapeDtypeStruct((m, n), BF16), small_shape],
        compiler_params=_cparams("parallel", "arbitrary"),
        name=name,
    )(xn, w_all, w_small)


def _even_kernel(seq_len, n_seqs, init_layer, want_final, n_carried, out_layer, *refs):
    has_init = init_layer is not None
    layer_idx, n_layers = init_layer if has_init else (0, 1)
    chunk = EVEN_CHUNK
    n_chunks = seq_len // chunk
    total = n_seqs * n_chunks
    use_cross = has_init or n_chunks > 1
    unroll = min(EVEN_UNROLL, n_chunks)
    assert not has_init or n_seqs == 1
    it = iter(refs)
    par_ref = next(it)
    m0_ref = next(it) if has_init else None
    mq_ref, mk_ref, mv_ref, mo_ref, rq_ref, rk_ref, rv_ref, rg_ref, gate_ref = (next(it) for _ in range(9))
    if has_init:
        c0_ref, n0_ref, r0_ref = next(it), next(it), next(it)
    for _ in range(n_carried):
        next(it)
    out_ref = next(it)
    if want_final:
        finals = [next(it) for _ in range(4)]
        if n_carried == 0:
            for ref in finals:
                ref[...] = jnp.zeros(ref.shape, F32)
            finals = [ref.at[:, out_layer] for ref in finals]
        c_out_ref, n_out_ref, m_out_ref, r_out_ref = finals
    grow_ref, rep_ref, edge_ref, cn_ref, s_ref = (next(it) for _ in range(5))
    if use_cross:
        cns_ref, ss_ref, ms_ref = next(it), next(it), next(it)

    b = pl.program_id(0)
    h = pl.program_id(1)
    ii = lax.broadcasted_iota(jnp.int32, (chunk, chunk), 0)
    jj = lax.broadcasted_iota(jnp.int32, (chunk, chunk), 1)
    pos_col = lax.broadcasted_iota(jnp.int32, (chunk, 1), 0).astype(F32)
    pos_int = lax.broadcasted_iota(jnp.int32, (chunk, 1), 0)
    ones_blk = jnp.ones((chunk, 128), BF16)
    ones_sum = jnp.ones((256, 128), BF16)

    sub = lax.broadcasted_iota(jnp.int32, (8, 1), 0)
    bias = jnp.where(sub == 0, par_ref[h], jnp.where(sub == 1, par_ref[4 + h],
                     jnp.where(sub == 2, par_ref[8 + h], jnp.where(sub == 3, par_ref[12 + h], 0.0))))
    lane = lax.broadcasted_iota(jnp.int32, (1, 128), 1)
    tri_pre = (ii <= jj).astype(BF16)
    tri_suf = (ii >= jj).astype(BF16)
    def lanes(c):
        if isinstance(c, int):
            return slice(c * chunk, (c + 1) * chunk)
        return pl.ds(pl.multiple_of(c * chunk, chunk), chunk)

    def rows(c):
        if isinstance(c, int):
            return pl.ds(c * chunk, chunk)
        return pl.ds(pl.multiple_of(c * chunk, chunk), chunk)

    def prepare_chunk(c):
        pre = gate_ref[:, lanes(c)] + bias
        log_f = pltpu.roll(_log_sigmoid(pre), 6, 0)
        hi = log_f.astype(BF16)
        rest = log_f - hi.astype(F32)
        mid = rest.astype(BF16)
        low = (rest - mid.astype(F32)).astype(BF16)
        parts = jnp.concatenate([hi, mid, low, jnp.zeros_like(hi)], axis=0)
        cum_pre = _dot(parts, tri_pre)
        cum_suf = _dot(parts, tri_suf)
        fcum = jnp.where(sub == 0, cum_pre[0:8] + cum_pre[8:16] + cum_pre[16:24],
                         cum_suf[0:8] + cum_suf[8:16] + cum_suf[16:24])
        gsc = pre - fcum
        packed = jnp.where(sub < 2, gsc, pltpu.roll(fcum, 2, 0))
        grow_ref[:, lanes(c)] = packed
        pcol = packed.T
        g_rep = [jnp.broadcast_to(pcol[:, k:k + 1], (chunk, 128)) for k in range(4)]
        g_rep.append(_seg_scan(g_rep[0], chunk, pos_int, jnp.maximum, NEG, False))
        g_rep.append(_seg_scan(g_rep[1], chunk, pos_int, jnp.maximum, NEG, True))
        rep_ref[rows(c), :] = jnp.concatenate(g_rep, axis=1)
        f_tot = jnp.sum(log_f, axis=1, keepdims=True)
        g_top = jnp.max(gsc, axis=1, keepdims=True)
        edge_ref[c] = jnp.where(lane == 0, f_tot, jnp.where(lane == 1, g_top, 0.0))

    def for_all_chunks(body):
        if total <= 2 * EVEN_UNROLL:
            for c in range(total):
                body(c)
        else:
            def group(g, carry):
                for u in range(EVEN_UNROLL):
                    body(g * EVEN_UNROLL + u)
                return carry
            lax.fori_loop(0, total // EVEN_UNROLL, group, 0)

    for_all_chunks(prepare_chunk)

    k_scale = DK ** -0.5
    lg_f = _log_sigmoid(jnp.full((1, 1), par_ref[16 + h], F32))
    lg_b = _log_sigmoid(jnp.full((1, 1), par_ref[20 + h], F32))
    diff = (ii - jj).astype(F32)
    ret_mask = k_scale * (jnp.where(diff >= 0, jnp.exp(lg_f * jnp.maximum(diff, 0.0)), 0.0)
                          + jnp.where(diff <= 0, jnp.exp(lg_b * jnp.maximum(-diff, 0.0)), 0.0))
    rq_dec = (jnp.exp(lg_f * (pos_col + 1.0)), jnp.exp(lg_b * (chunk - pos_col)))
    rk_dec = (k_scale * jnp.exp(lg_f * (chunk - 1.0 - pos_col)), k_scale * jnp.exp(lg_b * pos_col))
    r_chunk = (jnp.exp(lg_f * chunk), jnp.exp(lg_b * chunk))

    def load_kv(c):
        return mk_ref[rows(c), :], jnp.concatenate([mv_ref[rows(c), :], ones_blk], axis=1)

    def ret_kv(c):
        return rk_ref[rows(c), :], rv_ref[rows(c), :]

    def advance(c, d, m, slot):
        if use_cross:
            cns_ref[c, :, 256 * d:256 * d + 256] = cn_ref[slot].astype(BF16)
            ss_ref[c, 128 * d:128 * d + 128, :] = s_ref[slot].astype(BF16)
            ms_ref[c, d:d + 1, :] = jnp.broadcast_to(m, (1, 128))
        edge = edge_ref[c]
        mu = jnp.maximum(m, edge[d:d + 1, 1:2])
        k, vext = load_kv(c)
        kw = (k.astype(F32) * (k_scale * jnp.exp(rep_ref[rows(c), 128 * d:128 * d + 128] - mu))).astype(BF16)
        cn_ref[slot] = jnp.exp(m - mu) * cn_ref[slot] + _dot_tn(kw, vext)
        rk, rv = ret_kv(c)
        s_ref[slot] = r_chunk[d] * s_ref[slot] + _dot_tn((rk.astype(F32) * rk_dec[d]).astype(BF16), rv)
        return edge[d:d + 1, 0:1] + mu

    if has_init:
        base = (b * n_layers + layer_idx) * 8
        m_init = (jnp.full((1, 1), m0_ref[base + h], F32), jnp.full((1, 1), m0_ref[base + 4 + h], F32))
        for d in range(2):
            cn_ref[d] = jnp.concatenate([c0_ref[d], jnp.broadcast_to(n0_ref[d], (DK, 128))], axis=1)
            s_ref[d] = r0_ref[d]
    else:
        m_init = (jnp.zeros((1, 1), F32), jnp.zeros((1, 1), F32))
        cn_ref[...] = jnp.zeros((2 * n_seqs, DK, 256), F32)
        s_ref[...] = jnp.zeros((2 * n_seqs, DK, DK), F32)

    m_fin = []
    for seq in range(n_seqs):
        def state_group(g, carry, seq=seq):
            m_f, m_b = carry
            for u in range(unroll):
                c = g * unroll + u
                m_f = advance(seq * n_chunks + c, 0, m_f, 2 * seq)
                m_b = advance(seq * n_chunks + n_chunks - 1 - c, 1, m_b, 2 * seq + 1)
            return m_f, m_b

        if n_chunks // unroll > 1:
            m_fin.append(lax.fori_loop(0, n_chunks // unroll, state_group, m_init))
        elif use_cross or want_final:
            m_fin.append(state_group(0, m_init))

    def rms_rep(x):
        sq = x * x
        hi = sq.astype(BF16)
        low = (sq - hi.astype(F32)).astype(BF16)
        ssum = _dot(jnp.concatenate([hi, low], axis=1), ones_sum)
        return x * lax.rsqrt(ssum * (1.0 / 128) + EPS)

    def output_chunk(c):
        q = mq_ref[rows(c), :]
        k, vext = load_kv(c)
        scores = _dot_nt(q, k) * k_scale
        cols = rep_ref[rows(c), :]
        f_f, f_b, gm_f, gm_b = cols[:, 256:384], cols[:, 384:512], cols[:, 512:640], cols[:, 640:768]
        row = grow_ref[:, lanes(c)]
        if use_cross:
            m_f = ms_ref[c, 0:1, :][:, 0:1]
            m_b = ms_ref[c, 1:2, :][:, 0:1]
        else:
            m_f, m_b = m_init
        mu_f = jnp.maximum(m_f, gm_f)
        mu_b = jnp.maximum(m_b, gm_b)
        p_f = jnp.exp(jnp.where(jj <= ii, row[0:1, :], NEG) - jnp.concatenate([mu_f, mu_f], axis=1)) * scores
        p_b = jnp.exp(jnp.where(jj >= ii, row[1:2, :], NEG) - jnp.concatenate([mu_b, mu_b], axis=1)) * scores
        tot = _dot(jnp.concatenate([p_f, p_b], axis=0).astype(BF16), vext)
        num_f, den_f, num_b, den_b = tot[:chunk, :128], tot[:chunk, 128:], tot[chunk:, :128], tot[chunk:, 128:]
        if use_cross:
            cross = _dot(q, cns_ref[c])
            w_f = jnp.exp(m_f - mu_f)
            w_b = jnp.exp(m_b - mu_b)
            num_f, den_f = num_f + w_f * cross[:, 0:128], den_f + w_f * cross[:, 128:256]
            num_b, den_b = num_b + w_b * cross[:, 256:384], den_b + w_b * cross[:, 384:512]
        h_f = num_f / jnp.maximum(jnp.abs(den_f), jnp.exp(-(f_f + mu_f)))
        h_b = num_b / jnp.maximum(jnp.abs(den_b), jnp.exp(-(f_b + mu_b)))
        out_m = rms_rep(h_f + h_b) * jax.nn.sigmoid(mo_ref[rows(c), :].astype(F32))
        rq = rq_ref[rows(c), :]
        rk, rv = ret_kv(c)
        o_r = _dot((_dot_nt(rq, rk) * ret_mask).astype(BF16), rv)
        if use_cross:
            rq32 = rq.astype(F32)
            q_in = jnp.concatenate([rq32 * rq_dec[0], rq32 * rq_dec[1]], axis=1).astype(BF16)
            o_r = o_r + _dot(q_in, ss_ref[c])
        rg = rg_ref[rows(c), :].astype(F32)
        out_r = rms_rep(o_r) * (rg * jax.nn.sigmoid(rg))
        out_ref[rows(c), :] = jnp.concatenate([out_m, out_r], axis=1).astype(BF16)

    for_all_chunks(output_chunk)

    if want_final:
        for seq in range(n_seqs):
            for d in range(2):
                cn = cn_ref[2 * seq + d]
                c_out_ref[seq, d] = cn[:, :DK]
                n_out_ref[seq, d] = cn[:, DK:].T[0:8, :]
                r_out_ref[seq, d] = s_ref[2 * seq + d]
            m_out_ref[seq] = jnp.concatenate([jnp.broadcast_to(m_fin[seq][0], (1, 128)),
                                              jnp.broadcast_to(m_fin[seq][1], (1, 128)), jnp.zeros((6, 128), F32)], axis=0)


def _even_mixer(proj, gates_t, params, group, init, layer_idx=0, carried=None):
    ctx = group == 0
    n_seq, seq_len = (N_CTX_SEQ, CTX_LEN) if ctx else (N_LAT_SEQ, LAT_LEN)
    row_off = 0 if ctx else N_CTX_TOK // LAT_LEN
    has_init, want_final = not ctx, ctx
    n_seqs = EVEN_CTX_SEQS if ctx else 1
    rows = n_seqs * seq_len
    n_chunks = seq_len // EVEN_CHUNK
    total = n_seqs * n_chunks
    use_cross = has_init or n_chunks > 1

    def col(k):
        return pl.BlockSpec((rows, 128), lambda b, h: (row_off + b, 4 * k + h))

    smem = pl.BlockSpec(memory_space=pltpu.SMEM)
    in_specs = [smem]
    args = [params]
    if has_init:
        c0, n0, m0, r0 = init
        n_layers = c0.shape[1]
        in_specs.append(smem)
        args.append(m0.reshape(-1))
    in_specs += [col(k) for k in range(8)]
    in_specs.append(pl.BlockSpec((8, rows), lambda b, h: (h, row_off + b)))
    args += [proj] * 8 + [gates_t]
    if has_init:
        in_specs += [pl.BlockSpec((None, None, 2, None, DK, 128), lambda b, h: (b, layer_idx, 0, h, 0, 0)),
                     pl.BlockSpec((None, None, 2, None, DK, 1), lambda b, h: (b, layer_idx, 0, h, 0, 0)),
                     pl.BlockSpec((None, None, 2, None, DK, 128), lambda b, h: (b, layer_idx, 0, h, 0, 0))]
        args += [c0, n0.reshape(n0.shape + (1,)), r0]
    out_specs = [pl.BlockSpec((rows, 256), lambda b, h: (b, h))]
    out_shape = [jax.ShapeDtypeStruct((n_seq * seq_len, D), BF16)]
    aliases = {}
    if want_final:
        n_even = (DEPTH + 1) // 2
        if carried is None:
            lay, at = n_even, 0
        else:
            lay, at = None, layer_idx
        out_specs += [pl.BlockSpec((n_seqs, lay, 2, None, DK, DK), lambda b, h: (b, at, 0, h, 0, 0)),
                      pl.BlockSpec((n_seqs, lay, 2, None, 8, DK), lambda b, h: (b, at, 0, h, 0, 0)),
                      pl.BlockSpec((n_seqs, lay, None, 8, 128), lambda b, h: (b, at, h, 0, 0)),
                      pl.BlockSpec((n_seqs, lay, 2, None, DK, DK), lambda b, h: (b, at, 0, h, 0, 0))]
        out_shape += [jax.ShapeDtypeStruct((n_seq, n_even, 2, HEADS, DK, DK), F32),
                      jax.ShapeDtypeStruct((n_seq, n_even, 2, HEADS, 8, DK), F32),
                      jax.ShapeDtypeStruct((n_seq, n_even, HEADS, 8, 128), F32),
                      jax.ShapeDtypeStruct((n_seq, n_even, 2, HEADS, DK, DK), F32)]
        if carried is not None:
            aliases = {len(args) + k: 1 + k for k in range(len(carried))}
            in_specs += [pl.BlockSpec(memory_space=pl.ANY)] * len(carried)
            args += list(carried)
    scratch = [pltpu.VMEM((8, rows), F32), pltpu.VMEM((rows, 768), F32), pltpu.VMEM((total, 8, 128), F32),
               pltpu.VMEM((2 * n_seqs, DK, 256), F32), pltpu.VMEM((2 * n_seqs, DK, DK), F32)]
    if use_cross:
        scratch += [pltpu.VMEM((total, DK, 512), BF16), pltpu.VMEM((total, 2 * DK, DK), BF16),
                    pltpu.VMEM((total, 8, 128), F32)]
    return pl.pallas_call(
        functools.partial(_even_kernel, seq_len, n_seqs, (layer_idx, n_layers) if has_init else None, want_final,
                          len(aliases), layer_idx),
        grid=(n_seq // n_seqs, HEADS),
        in_specs=in_specs,
        out_specs=out_specs,
        out_shape=out_shape,
        input_output_aliases=aliases,
        scratch_shapes=scratch,
        compiler_params=_cparams("parallel", "parallel"),
        name="even_mixer_ctx" if ctx else "even_mixer_lat",
    )(*args)


def _gla_kernel(seq_len, n_seqs, has_init, want_final, n_carried, out_layer, *refs):
    chunk = GLA_CHUNK
    n_chunks = seq_len // chunk
    total = n_seqs * n_chunks
    n_rows = n_seqs * seq_len
    assert not has_init or n_seqs == 1
    it = iter(refs)
    q_ref, k_ref, v_ref, gr_ref, lr_ref, wz_ref, bz_ref = (next(it) for _ in range(7))
    s0_ref = next(it) if has_init else None
    for _ in range(n_carried):
        next(it)
    out_ref = next(it)
    s_out_ref = next(it) if want_final else None
    if want_final and n_carried == 0:
        s_out_ref[...] = jnp.zeros(s_out_ref.shape, F32)
        s_out_ref = s_out_ref.at[:, out_layer]
    qe_ref, ke_ref, cum_ref, sts_ref, st_ref = (next(it) for _ in range(5))
    unroll = min(GLA_UNROLL, n_chunks)

    x4 = lr_ref[...]
    hi = x4.astype(BF16).astype(F32)
    mid = (x4 - hi).astype(BF16).astype(F32)
    low = ((x4 - hi) - mid).astype(BF16).astype(F32)
    grp = lax.broadcasted_iota(jnp.int32, (1, 128), 1) // (2 * GLA_RANK)
    lhs = jnp.concatenate([jnp.where(grp == 1, mid, jnp.where(grp == 2, low, hi)),
                           jnp.where(grp == 0, mid, jnp.where(grp == 1, hi, 0.0))], axis=1).astype(BF16)
    z = _dot(lhs, wz_ref[...]) + bz_ref[...]
    la = _log_sigmoid(z) / GLA_TAU
    pos = lax.broadcasted_iota(jnp.int32, (n_rows, 1), 0) % chunk
    b_f = _seg_scan(la[:, :128], chunk, pos, jnp.add, 0.0, False)
    b_b = _seg_scan(la[:, 128:], chunk, pos, jnp.add, 0.0, True)
    q = q_ref[...].astype(F32) * (DK ** -0.5)
    k = k_ref[...].astype(F32)
    qe_ref[...] = jnp.concatenate([q * jnp.exp(b_f), q * jnp.exp(b_b)], axis=1).astype(BF16)
    ke_ref[...] = jnp.concatenate([k * jnp.exp(-b_f), k * jnp.exp(-b_b)], axis=1).astype(BF16)
    cum_ref[...] = jnp.concatenate([b_f, b_b], axis=1)

    ii = lax.broadcasted_iota(jnp.int32, (chunk, chunk), 0)
    jj = lax.broadcasted_iota(jnp.int32, (chunk, chunk), 1)

    def rows(c):
        if isinstance(c, int):
            return pl.ds(c * chunk, chunk)
        return pl.ds(pl.multiple_of(c * chunk, chunk), chunk)

    def advance(c, d, slot):
        lanes = slice(128 * d, 128 * d + 128)
        edge = c * chunk + (chunk - 1 if d == 0 else 0)
        decay = cum_ref[pl.ds(edge, 1), :][:, lanes]
        kw = (k_ref[rows(c), :].astype(F32) * jnp.exp(decay - cum_ref[rows(c), :][:, lanes])).astype(BF16)
        st = st_ref[slot]
        sts_ref[c, :, lanes] = st.astype(BF16)
        st_ref[slot] = st * jnp.exp(decay) + _dot_tn(v_ref[rows(c), :], kw)

    if has_init:
        st_ref[0] = s0_ref[0].T
        st_ref[1] = s0_ref[1].T
    else:
        st_ref[...] = jnp.zeros((2 * n_seqs, DV_GLA, DK), F32)

    for seq in range(n_seqs):
        def state_group(g, carry, seq=seq):
            for u in range(unroll):
                c = g * unroll + u
                advance(seq * n_chunks + c, 0, 2 * seq)
                advance(seq * n_chunks + n_chunks - 1 - c, 1, 2 * seq + 1)
            return carry

        if n_chunks // unroll == 1:
            state_group(0, 0)
        else:
            lax.fori_loop(0, n_chunks // unroll, state_group, 0)

    out_unroll = min(GLA_OUT_UNROLL, total)

    def output_group(g, carry):
        for u in range(out_unroll):
            c = g * out_unroll + u
            qe = qe_ref[rows(c), :]
            ke = ke_ref[rows(c), :]
            att = (jnp.where(jj <= ii, _dot_nt(qe[:, :128], ke[:, :128]), 0.0)
                   + jnp.where(jj >= ii, _dot_nt(qe[:, 128:], ke[:, 128:]), 0.0))
            o = _dot(att.astype(BF16), v_ref[rows(c), :]) + _dot_nt(qe, sts_ref[c])
            gr = gr_ref[rows(c), :].astype(F32)
            out_ref[rows(c), :] = (_rms(o) * (gr * jax.nn.sigmoid(gr))).astype(BF16)
        return carry

    if total // out_unroll == 1:
        output_group(0, 0)
    else:
        lax.fori_loop(0, total // out_unroll, output_group, 0)
    if want_final:
        for seq in range(n_seqs):
            s_out_ref[seq, 0] = st_ref[2 * seq].T
            s_out_ref[seq, 1] = st_ref[2 * seq + 1].T


def _gla_mixer(proj, lowrank, wz, bz, group, init, layer_idx=0, carried=None):
    ctx = group == 0
    n_seq, seq_len = (N_CTX_SEQ, CTX_LEN) if ctx else (N_LAT_SEQ, LAT_LEN)
    row_off = 0 if ctx else N_CTX_TOK // LAT_LEN
    has_init, want_final = not ctx, ctx
    n_seqs = GLA_CTX_SEQS if ctx else 1
    n_rows = n_seqs * seq_len
    n_chunks = seq_len // GLA_CHUNK
    total = n_seqs * n_chunks
    in_specs = [pl.BlockSpec((n_rows, 128), lambda b, h: (row_off + b, h)),
                pl.BlockSpec((n_rows, 128), lambda b, h: (row_off + b, 4 + h)),
                pl.BlockSpec((n_rows, 256), lambda b, h: (row_off + b, 4 + h)),
                pl.BlockSpec((n_rows, 256), lambda b, h: (row_off + b, 8 + h)),
                pl.BlockSpec((n_rows, 128), lambda b, h: (row_off + b, 0)),
                pl.BlockSpec((None, 256, 256), lambda b, h: (h, 0, 0)),
                pl.BlockSpec((None, 1, 256), lambda b, h: (h, 0, 0))]
    args = [proj, proj, proj, proj, lowrank, wz, bz]
    if has_init:
        in_specs.append(pl.BlockSpec((None, None, 2, None, DK, DV_GLA), lambda b, h: (b, layer_idx, 0, h, 0, 0)))
        args.append(init)
    out_specs = [pl.BlockSpec((n_rows, 256), lambda b, h: (b, h))]
    out_shape = [jax.ShapeDtypeStruct((n_seq * seq_len, D), BF16)]
    aliases = {}
    if want_final:
        lay, at = (DEPTH // 2, 0) if carried is None else (None, layer_idx)
        out_specs.append(pl.BlockSpec((n_seqs, lay, 2, None, DK, DV_GLA), lambda b, h: (b, at, 0, h, 0, 0)))
        out_shape.append(jax.ShapeDtypeStruct((n_seq, DEPTH // 2, 2, HEADS, DK, DV_GLA), F32))
        if carried is not None:
            aliases = {len(args): 1}
            in_specs.append(pl.BlockSpec(memory_space=pl.ANY))
            args.append(carried)
    scratch = [pltpu.VMEM((n_rows, 256), BF16)] * 2 + [pltpu.VMEM((n_rows, 256), F32),
                                                       pltpu.VMEM((total, DV_GLA, 2 * DK), BF16),
                                                       pltpu.VMEM((2 * n_seqs, DV_GLA, DK), F32)]
    return pl.pallas_call(
        functools.partial(_gla_kernel, seq_len, n_seqs, has_init, want_final, len(aliases), layer_idx),
        grid=(n_seq // n_seqs, HEADS),
        in_specs=in_specs,
        out_specs=out_specs,
        out_shape=out_shape,
        input_output_aliases=aliases,
        scratch_shapes=scratch,
        compiler_params=_cparams("parallel", "parallel"),
        name="gla_mixer_ctx" if ctx else "gla_mixer_lat",
    )(*args)


def _tail_kernel(last, tile, tile_off, n_tiles, n_carried, *refs):
    it = iter(refs)
    hc_ref, hl_ref, hup_ref, hdn_ref, x_ref, xup_ref, xdn_ref = (next(it) for _ in range(7))
    wo_ref, win_ref, cw_ref, cb_ref, w_ref, mod_ref, g_mix_ref, g_pre_ref, g_post_ref = (next(it) for _ in range(9))
    if not last:
        g_next_ref, mod_next_ref = next(it), next(it)
    for _ in range(n_carried):
        next(it)
    x_out_ref = next(it)
    xn_out_ref = None if last else next(it)
    hbuf_ref = next(it)
    i = pl.program_id(0) + tile_off
    n_ctx = N_CTX_TOK // tile
    tiles_per_seq = LAT_LEN // tile
    n_chunks = D_FF // FF_CHUNK

    def cols(f):
        return slice(f * FF_CHUNK, (f + 1) * FF_CHUNK), slice(D_FF + f * FF_CHUNK, D_FF + (f + 1) * FF_CHUNK)

    def act(conv, v, sl):
        return (jax.nn.gelu(conv + cb_ref[:, sl]) * v).astype(BF16)

    def mixer_out(h, x, keep=None):
        m = mod_ref[...]
        x_mid = x + m[2:3] * (_rms(_dot(h, wo_ref[...])) * g_mix_ref[...])
        xn = _rms(x_mid) * g_pre_ref[...] * (1.0 + m[4:5]) + m[3:4]
        if keep is not None:
            xn = jnp.where(keep, xn, 0.0)
        return x_mid, xn.astype(BF16)

    def finish(x_mid):
        x_new = x_mid + mod_ref[5:6, :] * (_rms(_dot(hbuf_ref[...], w_ref[...])) * g_post_ref[...])
        x_out_ref[...] = x_new
        if not last:
            mn = mod_next_ref[...]
            xn_out_ref[...] = (_rms(x_new) * g_next_ref[...] * (1.0 + mn[1:2]) + mn[0:1]).astype(BF16)

    def ctx_branch():
        x_mid, xn = mixer_out(hc_ref[...], x_ref[...])
        pos = lax.broadcasted_iota(jnp.int32, (tile, 1), 0) % CTX_LEN
        for f in range(n_chunks):
            sl, sv = cols(f)
            a = _dot(xn, win_ref[:, sl])
            left = jnp.where(pos >= 1, pltpu.roll(a, 1, 0), 0.0)
            right = jnp.where(pos < CTX_LEN - 1, pltpu.roll(a, tile - 1, 0), 0.0)
            conv = cw_ref[3:4, sl] * left + cw_ref[4:5, sl] * a + cw_ref[5:6, sl] * right
            hbuf_ref[:, sl] = act(conv, _dot(xn, win_ref[:, sv]), sl)
        finish(x_mid)

    def lat_branch():
        t = (i - n_ctx) % tiles_per_seq
        n_rows = tile + 2 * GRID_W
        rowpos = lax.broadcasted_iota(jnp.int32, (n_rows, 1), 0)
        first = jnp.where(t > 0, 0, GRID_W)
        stop = jnp.where(t < tiles_per_seq - 1, n_rows, GRID_W + tile)
        x_mid, xe = mixer_out(jnp.concatenate([hup_ref[...], hl_ref[...], hdn_ref[...]], axis=0),
                              jnp.concatenate([xup_ref[...], x_ref[...], xdn_ref[...]], axis=0),
                              (rowpos >= first) & (rowpos < stop))
        x_mid = x_mid[GRID_W:GRID_W + tile]
        xn = xe[GRID_W:GRID_W + tile]
        colpos = rowpos % GRID_W
        for f in range(n_chunks):
            sl, sv = cols(f)
            buf = _dot(xe, win_ref[:, sl])
            left = jnp.where(colpos >= 1, pltpu.roll(buf, 1, 0), 0.0)
            right = jnp.where(colpos < GRID_W - 1, pltpu.roll(buf, n_rows - 1, 0), 0.0)
            conv = None
            for dr in range(3):
                rs = slice(dr * GRID_W, dr * GRID_W + tile)
                term = (cw_ref[3 * dr:3 * dr + 1, sl] * left[rs] + cw_ref[3 * dr + 1:3 * dr + 2, sl] * buf[rs]
                        + cw_ref[3 * dr + 2:3 * dr + 3, sl] * right[rs])
                conv = term if conv is None else conv + term
            hbuf_ref[:, sl] = act(conv, _dot(xn, win_ref[:, sv]), sl)
        finish(x_mid)

    if tile_off + n_tiles <= n_ctx:
        ctx_branch()
    elif tile_off >= n_ctx:
        lat_branch()
    else:
        pl.when(i < n_ctx)(ctx_branch)
        pl.when(i >= n_ctx)(lat_branch)


def _layer_tail(h_ctx, h_lat, w_o, w_in, conv_w, conv_b, w_out, x, mods, g_mix_post, g_ffn_pre, g_post, g_next,
                layer, group, x_dst=None, xn_dst=None):
    last = layer == DEPTH - 1
    tile = TAIL_TILE_CTX if group == 0 else TAIL_TILE_LAT
    tile_off = 0 if group == 0 else N_CTX_TOK // tile
    n_tiles = (N_CTX_TOK if group == 0 else N_TOK - N_CTX_TOK) // tile
    halo_per_tile = tile // GRID_W
    n_halo = N_TOK // GRID_W
    n_ctx = N_CTX_TOK // tile
    n_lat_halo = (N_TOK - N_CTX_TOK) // GRID_W
    row = pl.BlockSpec((tile, D), lambda i: (i + tile_off, 0))
    resident = dict(pipeline_mode=pl.Buffered(1))
    layer_vec = pl.BlockSpec((None, 1, D), lambda i: (layer, 0, 0))

    def lat_halo(i, shift):
        return jnp.clip((i + tile_off - n_ctx + shift) * halo_per_tile - (1 - shift), 0, n_lat_halo - 1)

    h_ctx_spec = (pl.BlockSpec((tile, D), lambda i: (i, 0)) if group == 0
                  else pl.BlockSpec((GRID_W, D), lambda i: (0, 0)))
    h_lat_spec = (pl.BlockSpec((tile, D), lambda i: (i, 0)) if group == 1
                  else pl.BlockSpec((GRID_W, D), lambda i: (0, 0)))
    in_specs = [h_ctx_spec, h_lat_spec,
                pl.BlockSpec((GRID_W, D), lambda i: (lat_halo(i, 0), 0)),
                pl.BlockSpec((GRID_W, D), lambda i: (lat_halo(i, 1), 0)),
                row,
                pl.BlockSpec((GRID_W, D), lambda i: (jnp.maximum((i + tile_off) * halo_per_tile - 1, 0), 0)),
                pl.BlockSpec((GRID_W, D), lambda i: (jnp.minimum((i + tile_off + 1) * halo_per_tile, n_halo - 1), 0)),
                pl.BlockSpec((D, D), lambda i: (0, 0), **resident),
                pl.BlockSpec((None, D, 2 * D_FF), lambda i: (layer, 0, 0), **resident),
                pl.BlockSpec((None, 9, D_FF), lambda i: (layer, 0, 0)),
                pl.BlockSpec((None, 1, D_FF), lambda i: (layer, 0, 0)),
                pl.BlockSpec((None, D_FF, D), lambda i: (layer, 0, 0), **resident),
                pl.BlockSpec((None, None, 6, D), lambda i: (layer, _mod_row(i + tile_off, tile), 0, 0)),
                layer_vec, layer_vec, layer_vec]
    args = [h_ctx, h_lat, h_lat, h_lat, x, x, x, w_o, w_in, conv_w.reshape(DEPTH, 9, D_FF), conv_b, w_out, mods,
            g_mix_post, g_ffn_pre, g_post]
    aliases = {}
    if last:
        out_specs = [pl.BlockSpec((tile, D), lambda i: (i, 0))]
        out_shape = [jax.ShapeDtypeStruct((n_tiles * tile, D), F32)]
    else:
        in_specs += [pl.BlockSpec((None, 1, D), lambda i: (layer + 1, 0, 0)),
                     pl.BlockSpec((None, None, 6, D), lambda i: (layer + 1, _mod_row(i + tile_off, tile), 0, 0))]
        args += [g_next, mods]
        aliases = {len(args): 0, len(args) + 1: 1}
        in_specs += [pl.BlockSpec(memory_space=pl.ANY)] * 2
        args += [x_dst, xn_dst]
        out_specs = [row, row]
        out_shape = [jax.ShapeDtypeStruct((N_TOK, D), F32), jax.ShapeDtypeStruct((N_TOK, D), BF16)]
    return pl.pallas_call(
        functools.partial(_tail_kernel, last, tile, tile_off, n_tiles, len(aliases)),
        grid=(n_tiles,),
        in_specs=in_specs,
        out_specs=out_specs,
        out_shape=out_shape,
        input_output_aliases=aliases,
        scratch_shapes=[pltpu.VMEM((tile, D_FF), BF16)],
        compiler_params=_cparams("parallel"),
        name="layer_tail",
    )(*args)


def _even_weights(w_in, w_out):
    src = [0] * (8 * HEADS)
    used = [0.0] * (8 * HEADS)
    for h in range(HEADS):
        for k, col in enumerate((h, 8 + h, 4 + h, 12 + h)):
            src[8 * h + k] = col
            used[8 * h + k] = 1.0
    w_gate = (w_in[:, 2048:2064][:, jnp.array(src)] * jnp.array(used, F32)).T.astype(BF16)
    w_o = w_out.reshape(2, HEADS, 128, D).transpose(1, 0, 2, 3).reshape(D, D).astype(BF16)
    return w_gate, w_o


def _odd_weights(w_in, w2, b2):
    w_lr = jnp.tile(w_in[:, 3072:3072 + 2 * GLA_RANK], (1, 128 // (2 * GLA_RANK))).astype(BF16)
    w2h = w2.reshape(2, GLA_RANK, HEADS, DK).transpose(2, 0, 1, 3)
    w32 = jnp.zeros((HEADS, 2 * GLA_RANK, 256), F32)
    w32 = w32.at[:, :GLA_RANK, :DK].set(w2h[:, 0]).at[:, GLA_RANK:, DK:].set(w2h[:, 1])
    hi = w32.astype(BF16)
    mid = (w32 - hi.astype(F32)).astype(BF16)
    low = (w32 - hi.astype(F32) - mid.astype(F32)).astype(BF16)
    none = jnp.zeros_like(hi)
    wz = jnp.concatenate([hi, hi, hi, mid, mid, low, none, none], axis=1)
    bz = b2.reshape(2, HEADS, DK).transpose(1, 0, 2).reshape(HEADS, 1, 2 * DK)
    return w_lr, wz, bz


def kernel(x_prompt, x_sample, c, c_ctx, state_mlstm_C, state_mlstm_n, state_mlstm_m, state_ret, state_gla, ada_w, ada_b, norm_mix_pre, norm_mix_post, norm_ffn_pre, norm_ffn_post, w_in_even, w_out_even, mlstm_igate_b, mlstm_fgate_b, ret_decay_logit, w_in_odd, gla_gate_w2, gla_gate_b, w_out_odd, ffn_w_in, ffn_conv_w, ffn_conv_b, ffn_w_out):
    cond = jnp.concatenate([c_ctx[None, :], c, jnp.zeros((8 - 1 - N_LAT_SEQ, D), F32)], axis=0)
    mods = _mod_table(cond, ada_w, ada_b)
    norm_mix_pre, norm_mix_post, norm_ffn_pre, norm_ffn_post = (
        g.reshape(DEPTH, 1, D) for g in (norm_mix_pre, norm_mix_post, norm_ffn_pre, norm_ffn_post))
    ffn_conv_b = ffn_conv_b.reshape(DEPTH, 1, D_FF)
    ffn_w_in_bf, ffn_w_out_bf = ffn_w_in.astype(BF16), ffn_w_out.astype(BF16)
    w_even_bf = jnp.concatenate([w_in_even[:, :, :2048], w_in_even[:, :, 2064:]], axis=2).astype(BF16)
    w_odd_bf = w_in_odd.astype(BF16)

    x, xn = _prenorm(x_prompt.reshape(N_CTX_TOK, D), x_sample.reshape(N_LAT_SEQ * LAT_LEN, D),
                     norm_mix_pre, mods, 0)
    x_spare = jnp.zeros((N_TOK, D), F32)
    even_finals, gla_finals = None, None
    for layer in range(DEPTH):
        idx = layer // 2
        if layer % 2 == 0:
            w_gate, w_o = _even_weights(w_in_even[idx], w_out_even[idx])
            proj, gates = _proj(xn, w_even_bf, idx, 4096, w_gate, True, "proj_even")
            params = jnp.concatenate([mlstm_igate_b[idx].reshape(-1), mlstm_fgate_b[idx].reshape(-1),
                                      ret_decay_logit[idx].reshape(-1)])
            h_ctx, *even_finals = _even_mixer(proj, gates, params, 0, None, idx, even_finals)
            init = (state_mlstm_C, state_mlstm_n, state_mlstm_m, state_ret)
            (h_lat,) = _even_mixer(proj, gates, params, 1, init, idx)
        else:
            w_lr, wz, bz = _odd_weights(w_in_odd[idx], gla_gate_w2[idx], gla_gate_b[idx])
            w_o = w_out_odd[idx].astype(BF16)
            proj, lowrank = _proj(xn, w_odd_bf, idx, 3072, w_lr, False, "proj_odd")
            h_ctx, gla_finals = _gla_mixer(proj, lowrank, wz, bz, 0, None, idx, gla_finals)
            (h_lat,) = _gla_mixer(proj, lowrank, wz, bz, 1, state_gla, idx)
        tail = functools.partial(_layer_tail, h_ctx, h_lat, w_o, ffn_w_in_bf, ffn_conv_w, ffn_conv_b, ffn_w_out_bf,
                                 x, mods, norm_mix_post, norm_ffn_pre, norm_ffn_post, norm_mix_pre, layer)
        if layer < DEPTH - 1:
            x_new, xn_new = tail(0, x_spare, xn)
            x_new, xn = tail(1, x_new, xn_new)
            x, x_spare = x_new, x
        else:
            (y_ctx,) = tail(0)
            (y_lat,) = tail(1)

    y_prompt = y_ctx.reshape(N_CTX_SEQ, CTX_LEN, D)
    y_sample = y_lat.reshape(N_LAT_SEQ, LAT_LEN, D)
    new_c, new_n, new_m, new_r = even_finals
    return (y_prompt, y_sample, new_c, new_n[..., 0, :], new_m[:, :, :, 0:2, 0].transpose(0, 1, 3, 2), new_r, gla_finals)
```

```python
import functools

import jax
import jax.numpy as jnp
from jax import lax
from jax.experimental import pallas as pl
from jax.experimental.pallas import tpu as pltpu

F32 = jnp.float32
BF16 = jnp.bfloat16

D = 1024
DEPTH = 4
N_CTX_SEQ, CTX_LEN = 16, 256
N_LAT_SEQ, LAT_LEN = 2, 2048
GRID_W = 64
N_CTX_TOK = N_CTX_SEQ * CTX_LEN
N_TOK = N_CTX_TOK + N_LAT_SEQ * LAT_LEN
HEADS = 4
DK = 128
DV_GLA = 256
GLA_RANK = 16
GLA_TAU = 16.0
D_FF = 2816
EPS = 1e-6
NEG = -1e30

ROW_TILE = 256
PROJ_ROW_TILE = 1024
PROJ_COL_TILE = 1024
EVEN_CHUNK = 256
EVEN_UNROLL = 4
EVEN_CTX_SEQS = 8
GLA_CHUNK = 64
GLA_UNROLL = 4
GLA_CTX_SEQS = 8
GLA_OUT_UNROLL = 8
FF_CHUNK = 256
TAIL_TILE_CTX = 512
TAIL_TILE_LAT = 512
VMEM_LIMIT = 56 * 1024 * 1024


def _cparams(*sem):
    return pltpu.CompilerParams(dimension_semantics=sem, vmem_limit_bytes=VMEM_LIMIT)


def _mod_row(i, tile):
    n_ctx = N_CTX_TOK // tile
    return jnp.where(i < n_ctx, 0, 1 + (i - n_ctx) // (LAT_LEN // tile))


def _rms(x):
    return x * lax.rsqrt(jnp.mean(x * x, axis=-1, keepdims=True) + EPS)


def _log_sigmoid(x):
    return jnp.minimum(x, 0.0) - jnp.log(1.0 + jnp.exp(-jnp.abs(x)))


def _dot(a, b):
    return jnp.dot(a, b, preferred_element_type=F32)


def _dot_nt(a, b):
    return lax.dot_general(a, b, (((1,), (1,)), ((), ())), preferred_element_type=F32)


def _dot_tn(a, b):
    return lax.dot_general(a, b, (((0,), (0,)), ((), ())), preferred_element_type=F32)


def _seg_scan(x, seg, pos, op, ident, reverse):
    n = x.shape[0]
    s = 1
    while s < seg:
        if reverse:
            x = op(x, jnp.where(pos < seg - s, pltpu.roll(x, n - s, 0), ident))
        else:
            x = op(x, jnp.where(pos >= s, pltpu.roll(x, s, 0), ident))
        s *= 2
    return x


def _mod_kernel(c_ref, w_ref, b_ref, o_ref):
    c = c_ref[...]
    s = (c * jax.nn.sigmoid(c)).astype(BF16)
    o_ref[0] = _dot(s, w_ref[0].astype(BF16)) + b_ref[0]


def _mod_table(cond, ada_w, ada_b):
    tn = 2048
    out = pl.pallas_call(
        _mod_kernel,
        grid=(DEPTH, 6 * D // tn),
        in_specs=[pl.BlockSpec((8, D), lambda l, j: (0, 0)),
                  pl.BlockSpec((1, D, tn), lambda l, j: (l, 0, j)),
                  pl.BlockSpec((1, 1, tn), lambda l, j: (l, 0, j))],
        out_specs=pl.BlockSpec((1, 8, tn), lambda l, j: (l, 0, j)),
        out_shape=jax.ShapeDtypeStruct((DEPTH, 8, 6 * D), F32),
        compiler_params=_cparams("parallel", "parallel"),
        name="mod_table",
    )(cond, ada_w, ada_b.reshape(DEPTH, 1, 6 * D))
    return out.reshape(DEPTH, 8, 6, D)


def _prenorm_kernel(xp_ref, xs_ref, g_ref, mod_ref, x_ref, xn_ref):
    i = pl.program_id(0)

    def run(src_ref):
        x = src_ref[...]
        m = mod_ref[...]
        x_ref[...] = x
        xn_ref[...] = (_rms(x) * g_ref[...] * (1.0 + m[1:2]) + m[0:1]).astype(BF16)

    pl.when(i < N_CTX_TOK // ROW_TILE)(lambda: run(xp_ref))
    pl.when(i >= N_CTX_TOK // ROW_TILE)(lambda: run(xs_ref))


def _prenorm(xp, xs, g, mods, layer):
    nc = N_CTX_TOK // ROW_TILE
    return pl.pallas_call(
        _prenorm_kernel,
        grid=(N_TOK // ROW_TILE,),
        in_specs=[pl.BlockSpec((ROW_TILE, D), lambda i: (jnp.minimum(i, nc - 1), 0)),
                  pl.BlockSpec((ROW_TILE, D), lambda i: (jnp.maximum(i - nc, 0), 0)),
                  pl.BlockSpec((None, 1, D), lambda i: (layer, 0, 0)),
                  pl.BlockSpec((None, None, 6, D), lambda i: (layer, _mod_row(i, ROW_TILE), 0, 0))],
        out_specs=[pl.BlockSpec((ROW_TILE, D), lambda i: (i, 0)),
                   pl.BlockSpec((ROW_TILE, D), lambda i: (i, 0))],
        out_shape=[jax.ShapeDtypeStruct((N_TOK, D), F32), jax.ShapeDtypeStruct((N_TOK, D), BF16)],
        compiler_params=_cparams("parallel"),
        name="prenorm",
    )(xp, xs, g, mods)


def _proj_kernel(small_transposed, x_ref, w_ref, ws_ref, o_ref, os_ref):
    x = x_ref[...]
    o_ref[...] = _dot(x, w_ref[...]).astype(o_ref.dtype)

    @pl.when(pl.program_id(1) == 0)
    def _():
        os_ref[...] = _dot_nt(ws_ref[...], x) if small_transposed else _dot(x, ws_ref[...])


def _proj(xn, w_all, layer_idx, n, w_small, small_transposed, name):
    tm, tn = PROJ_ROW_TILE, PROJ_COL_TILE
    m, k = xn.shape
    if small_transposed:
        r = w_small.shape[0]
        small_spec = pl.BlockSpec((r, tm), lambda i, j: (0, i))
        small_shape = jax.ShapeDtypeStruct((r, m), F32)
    else:
        small_spec = pl.BlockSpec((tm, w_small.shape[1]), lambda i, j: (i, 0))
        small_shape = jax.ShapeDtypeStruct((m, w_small.shape[1]), F32)
    return pl.pallas_call(
        functools.partial(_proj_kernel, small_transposed),
        grid=(m // tm, n // tn),
        in_specs=[pl.BlockSpec((tm, k), lambda i, j: (i, 0)),
                  pl.BlockSpec((None, k, tn), lambda i, j: (layer_idx, 0, j)),
                  pl.BlockSpec(w_small.shape, lambda i, j: (0, 0))],
        out_specs=[pl.BlockSpec((tm, tn), lambda i, j: (i, j)), small_spec],
        out_shape=[jax.ShapeDtypeStruct((m, n), BF16), small_shape],
        compiler_params=_cparams("parallel", "arbitrary"),
        name=name,
    )(xn, w_all, w_small)


def _even_kernel(seq_len, n_seqs, init_layer, want_final, n_carried, out_layer, *refs):
    has_init = init_layer is not None
    layer_idx, n_layers = init_layer if has_init else (0, 1)
    chunk = EVEN_CHUNK
    n_chunks = seq_len // chunk
    total = n_seqs * n_chunks
    use_cross = has_init or n_chunks > 1
    unroll = min(EVEN_UNROLL, n_chunks)
    assert not has_init or n_seqs == 1
    it = iter(refs)
    par_ref = next(it)
    m0_ref = next(it) if has_init else None
    mq_ref, mk_ref, mv_ref, mo_ref, rq_ref, rk_ref, rv_ref, rg_ref, gate_ref = (next(it) for _ in range(9))
    if has_init:
        c0_ref, n0_ref, r0_ref = next(it), next(it), next(it)
    for _ in range(n_carried):
        next(it)
    out_ref = next(it)
    if want_final:
        finals = [next(it) for _ in range(4)]
        if n_carried == 0:
            for ref in finals:
                ref[...] = jnp.zeros(ref.shape, F32)
            finals = [ref.at[:, out_layer] for ref in finals]
        c_out_ref, n_out_ref, m_out_ref, r_out_ref = finals
    grow_ref, rep_ref, edge_ref, cn_ref, s_ref = (next(it) for _ in range(5))
    if use_cross:
        cns_ref, ss_ref, ms_ref = next(it), next(it), next(it)

    b = pl.program_id(0)
    h = pl.program_id(1)
    ii = lax.broadcasted_iota(jnp.int32, (chunk, chunk), 0)
    jj = lax.broadcasted_iota(jnp.int32, (chunk, chunk), 1)
    pos_col = lax.broadcasted_iota(jnp.int32, (chunk, 1), 0).astype(F32)
    pos_int = lax.broadcasted_iota(jnp.int32, (chunk, 1), 0)
    ones_blk = jnp.ones((chunk, 128), BF16)
    ones_sum = jnp.ones((256, 128), BF16)

    sub = lax.broadcasted_iota(jnp.int32, (8, 1), 0)
    bias = jnp.where(sub == 0, par_ref[h], jnp.where(sub == 1, par_ref[4 + h],
                     jnp.where(sub == 2, par_ref[8 + h], jnp.where(sub == 3, par_ref[12 + h], 0.0))))
    lane = lax.broadcasted_iota(jnp.int32, (1, 128), 1)
    tri_pre = (ii <= jj).astype(BF16)
    tri_suf = (ii >= jj).astype(BF16)
    def lanes(c):
        if isinstance(c, int):
            return slice(c * chunk, (c + 1) * chunk)
        return pl.ds(pl.multiple_of(c * chunk, chunk), chunk)

    def rows(c):
        if isinstance(c, int):
            return pl.ds(c * chunk, chunk)
        return pl.ds(pl.multiple_of(c * chunk, chunk), chunk)

    def prepare_chunk(c):
        pre = gate_ref[:, lanes(c)] + bias
        log_f = pltpu.roll(_log_sigmoid(pre), 6, 0)
        hi = log_f.astype(BF16)
        rest = log_f - hi.astype(F32)
        mid = rest.astype(BF16)
        low = (rest - mid.astype(F32)).astype(BF16)
        parts = jnp.concatenate([hi, mid, low, jnp.zeros_like(hi)], axis=0)
        cum_pre = _dot(parts, tri_pre)
        cum_suf = _dot(parts, tri_suf)
        fcum = jnp.where(sub == 0, cum_pre[0:8] + cum_pre[8:16] + cum_pre[16:24],
                         cum_suf[0:8] + cum_suf[8:16] + cum_suf[16:24])
        gsc = pre - fcum
        packed = jnp.where(sub < 2, gsc, pltpu.roll(fcum, 2, 0))
        grow_ref[:, lanes(c)] = packed
        pcol = packed.T
        g_rep = [jnp.broadcast_to(pcol[:, k:k + 1], (chunk, 128)) for k in range(4)]
        g_rep.append(_seg_scan(g_rep[0], chunk, pos_int, jnp.maximum, NEG, False))
        g_rep.append(_seg_scan(g_rep[1], chunk, pos_int, jnp.maximum, NEG, True))
        rep_ref[rows(c), :] = jnp.concatenate(g_rep, axis=1)
        f_tot = jnp.sum(log_f, axis=1, keepdims=True)
        g_top = jnp.max(gsc, axis=1, keepdims=True)
        edge_ref[c] = jnp.where(lane == 0, f_tot, jnp.where(lane == 1, g_top, 0.0))

    def for_all_chunks(body):
        if total <= 2 * EVEN_UNROLL:
            for c in range(total):
                body(c)
        else:
            def group(g, carry):
                for u in range(EVEN_UNROLL):
                    body(g * EVEN_UNROLL + u)
                return carry
            lax.fori_loop(0, total // EVEN_UNROLL, group, 0)

    for_all_chunks(prepare_chunk)

    k_scale = DK ** -0.5
    lg_f = _log_sigmoid(jnp.full((1, 1), par_ref[16 + h], F32))
    lg_b = _log_sigmoid(jnp.full((1, 1), par_ref[20 + h], F32))
    diff = (ii - jj).astype(F32)
    ret_mask = k_scale * (jnp.where(diff >= 0, jnp.exp(lg_f * jnp.maximum(diff, 0.0)), 0.0)
                          + jnp.where(diff <= 0, jnp.exp(lg_b * jnp.maximum(-diff, 0.0)), 0.0))
    rq_dec = (jnp.exp(lg_f * (pos_col + 1.0)), jnp.exp(lg_b * (chunk - pos_col)))
    rk_dec = (k_scale * jnp.exp(lg_f * (chunk - 1.0 - pos_col)), k_scale * jnp.exp(lg_b * pos_col))
    r_chunk = (jnp.exp(lg_f * chunk), jnp.exp(lg_b * chunk))

    def load_kv(c):
        return mk_ref[rows(c), :], jnp.concatenate([mv_ref[rows(c), :], ones_blk], axis=1)

    def ret_kv(c):
        return rk_ref[rows(c), :], rv_ref[rows(c), :]

    def advance(c, d, m, slot):
        if use_cross:
            cns_ref[c, :, 256 * d:256 * d + 256] = cn_ref[slot].astype(BF16)
            ss_ref[c, 128 * d:128 * d + 128, :] = s_ref[slot].astype(BF16)
            ms_ref[c, d:d + 1, :] = jnp.broadcast_to(m, (1, 128))
        edge = edge_ref[c]
        mu = jnp.maximum(m, edge[d:d + 1, 1:2])
        k, vext = load_kv(c)
        kw = (k.astype(F32) * (k_scale * jnp.exp(rep_ref[rows(c), 128 * d:128 * d + 128] - mu))).astype(BF16)
        cn_ref[slot] = jnp.exp(m - mu) * cn_ref[slot] + _dot_tn(kw, vext)
        rk, rv = ret_kv(c)
        s_ref[slot] = r_chunk[d] * s_ref[slot] + _dot_tn((rk.astype(F32) * rk_dec[d]).astype(BF16), rv)
        return edge[d:d + 1, 0:1] + mu

    if has_init:
        base = (b * n_layers + layer_idx) * 8
        m_init = (jnp.full((1, 1), m0_ref[base + h], F32), jnp.full((1, 1), m0_ref[base + 4 + h], F32))
        for d in range(2):
            cn_ref[d] = jnp.concatenate([c0_ref[d], jnp.broadcast_to(n0_ref[d], (DK, 128))], axis=1)
            s_ref[d] = r0_ref[d]
    else:
        m_init = (jnp.zeros((1, 1), F32), jnp.zeros((1, 1), F32))
        cn_ref[...] = jnp.zeros((2 * n_seqs, DK, 256), F32)
        s_ref[...] = jnp.zeros((2 * n_seqs, DK, DK), F32)

    m_fin = []
    for seq in range(n_seqs):
        def state_group(g, carry, seq=seq):
            m_f, m_b = carry
            for u in range(unroll):
                c = g * unroll + u
                m_f = advance(seq * n_chunks + c, 0, m_f, 2 * seq)
                m_b = advance(seq * n_chunks + n_chunks - 1 - c, 1, m_b, 2 * seq + 1)
            return m_f, m_b

        if n_chunks // unroll > 1:
            m_fin.append(lax.fori_loop(0, n_chunks // unroll, state_group, m_init))
        elif use_cross or want_final:
            m_fin.append(state_group(0, m_init))

    def rms_rep(x):
        sq = x * x
        hi = sq.astype(BF16)
        low = (sq - hi.astype(F32)).astype(BF16)
        ssum = _dot(jnp.concatenate([hi, low], axis=1), ones_sum)
        return x * lax.rsqrt(ssum * (1.0 / 128) + EPS)

    def output_chunk(c):
        q = mq_ref[rows(c), :]
        k, vext = load_kv(c)
        scores = _dot_nt(q, k) * k_scale
        cols = rep_ref[rows(c), :]
        f_f, f_b, gm_f, gm_b = cols[:, 256:384], cols[:, 384:512], cols[:, 512:640], cols[:, 640:768]
        row = grow_ref[:, lanes(c)]
        if use_cross:
            m_f = ms_ref[c, 0:1, :][:, 0:1]
            m_b = ms_ref[c, 1:2, :][:, 0:1]
        else:
            m_f, m_b = m_init
        mu_f = jnp.maximum(m_f, gm_f)
        mu_b = jnp.maximum(m_b, gm_b)
        p_f = jnp.exp(jnp.where(jj <= ii, row[0:1, :], NEG) - jnp.concatenate([mu_f, mu_f], axis=1)) * scores
        p_b = jnp.exp(jnp.where(jj >= ii, row[1:2, :], NEG) - jnp.concatenate([mu_b, mu_b], axis=1)) * scores
        tot = _dot(jnp.concatenate([p_f, p_b], axis=0).astype(BF16), vext)
        num_f, den_f, num_b, den_b = tot[:chunk, :128], tot[:chunk, 128:], tot[chunk:, :128], tot[chunk:, 128:]
        if use_cross:
            cross = _dot(q, cns_ref[c])
            w_f = jnp.exp(m_f - mu_f)
            w_b = jnp.exp(m_b - mu_b)
            num_f, den_f = num_f + w_f * cross[:, 0:128], den_f + w_f * cross[:, 128:256]
            num_b, den_b = num_b + w_b * cross[:, 256:384], den_b + w_b * cross[:, 384:512]
        h_f = num_f / jnp.maximum(jnp.abs(den_f), jnp.exp(-(f_f + mu_f)))
        h_b = num_b / jnp.maximum(jnp.abs(den_b), jnp.exp(-(f_b + mu_b)))
        out_m = rms_rep(h_f + h_b) * jax.nn.sigmoid(mo_ref[rows(c), :].astype(F32))
        rq = rq_ref[rows(c), :]
        rk, rv = ret_kv(c)
        o_r = _dot((_dot_nt(rq, rk) * ret_mask).astype(BF16), rv)
        if use_cross:
            rq32 = rq.astype(F32)
            q_in = jnp.concatenate([rq32 * rq_dec[0], rq32 * rq_dec[1]], axis=1).astype(BF16)
            o_r = o_r + _dot(q_in, ss_ref[c])
        rg = rg_ref[rows(c), :].astype(F32)
        out_r = rms_rep(o_r) * (rg * jax.nn.sigmoid(rg))
        out_ref[rows(c), :] = jnp.concatenate([out_m, out_r], axis=1).astype(BF16)

    for_all_chunks(output_chunk)

    if want_final:
        for seq in range(n_seqs):
            for d in range(2):
                cn = cn_ref[2 * seq + d]
                c_out_ref[seq, d] = cn[:, :DK]
                n_out_ref[seq, d] = cn[:, DK:].T[0:8, :]
                r_out_ref[seq, d] = s_ref[2 * seq + d]
            m_out_ref[seq] = jnp.concatenate([jnp.broadcast_to(m_fin[seq][0], (1, 128)),
                                              jnp.broadcast_to(m_fin[seq][1], (1, 128)), jnp.zeros((6, 128), F32)], axis=0)


def _even_mixer(proj, gates_t, params, group, init, layer_idx=0, carried=None):
    ctx = group == 0
    n_seq, seq_len = (N_CTX_SEQ, CTX_LEN) if ctx else (N_LAT_SEQ, LAT_LEN)
    row_off = 0 if ctx else N_CTX_TOK // LAT_LEN
    has_init, want_final = not ctx, ctx
    n_seqs = EVEN_CTX_SEQS if ctx else 1
    rows = n_seqs * seq_len
    n_chunks = seq_len // EVEN_CHUNK
    total = n_seqs * n_chunks
    use_cross = has_init or n_chunks > 1

    def col(k):
        return pl.BlockSpec((rows, 128), lambda b, h: (row_off + b, 4 * k + h))

    smem = pl.BlockSpec(memory_space=pltpu.SMEM)
    in_specs = [smem]
    args = [params]
    if has_init:
        c0, n0, m0, r0 = init
        n_layers = c0.shape[1]
        in_specs.append(smem)
        args.append(m0.reshape(-1))
    in_specs += [col(k) for k in range(8)]
    in_specs.append(pl.BlockSpec((8, rows), lambda b, h: (h, row_off + b)))
    args += [proj] * 8 + [gates_t]
    if has_init:
        in_specs += [pl.BlockSpec((None, None, 2, None, DK, 128), lambda b, h: (b, layer_idx, 0, h, 0, 0)),
                     pl.BlockSpec((None, None, 2, None, DK, 1), lambda b, h: (b, layer_idx, 0, h, 0, 0)),
                     pl.BlockSpec((None, None, 2, None, DK, 128), lambda b, h: (b, layer_idx, 0, h, 0, 0))]
        args += [c0, n0.reshape(n0.shape + (1,)), r0]
    out_specs = [pl.BlockSpec((rows, 256), lambda b, h: (b, h))]
    out_shape = [jax.ShapeDtypeStruct((n_seq * seq_len, D), BF16)]
    aliases = {}
    if want_final:
        n_even = (DEPTH + 1) // 2
        if carried is None:
            lay, at = n_even, 0
        else:
            lay, at = None, layer_idx
        out_specs += [pl.BlockSpec((n_seqs, lay, 2, None, DK, DK), lambda b, h: (b, at, 0, h, 0, 0)),
                      pl.BlockSpec((n_seqs, lay, 2, None, 8, DK), lambda b, h: (b, at, 0, h, 0, 0)),
                      pl.BlockSpec((n_seqs, lay, None, 8, 128), lambda b, h: (b, at, h, 0, 0)),
                      pl.BlockSpec((n_seqs, lay, 2, None, DK, DK), lambda b, h: (b, at, 0, h, 0, 0))]
        out_shape += [jax.ShapeDtypeStruct((n_seq, n_even, 2, HEADS, DK, DK), F32),
                      jax.ShapeDtypeStruct((n_seq, n_even, 2, HEADS, 8, DK), F32),
                      jax.ShapeDtypeStruct((n_seq, n_even, HEADS, 8, 128), F32),
                      jax.ShapeDtypeStruct((n_seq, n_even, 2, HEADS, DK, DK), F32)]
        if carried is not None:
            aliases = {len(args) + k: 1 + k for k in range(len(carried))}
            in_specs += [pl.BlockSpec(memory_space=pl.ANY)] * len(carried)
            args += list(carried)
    scratch = [pltpu.VMEM((8, rows), F32), pltpu.VMEM((rows, 768), F32), pltpu.VMEM((total, 8, 128), F32),
               pltpu.VMEM((2 * n_seqs, DK, 256), F32), pltpu.VMEM((2 * n_seqs, DK, DK), F32)]
    if use_cross:
        scratch += [pltpu.VMEM((total, DK, 512), BF16), pltpu.VMEM((total, 2 * DK, DK), BF16),
                    pltpu.VMEM((total, 8, 128), F32)]
    return pl.pallas_call(
        functools.partial(_even_kernel, seq_len, n_seqs, (layer_idx, n_layers) if has_init else None, want_final,
                          len(aliases), layer_idx),
        grid=(n_seq // n_seqs, HEADS),
        in_specs=in_specs,
        out_specs=out_specs,
        out_shape=out_shape,
        input_output_aliases=aliases,
        scratch_shapes=scratch,
        compiler_params=_cparams("parallel", "parallel"),
        name="even_mixer_ctx" if ctx else "even_mixer_lat",
    )(*args)


def _gla_kernel(seq_len, n_seqs, has_init, want_final, n_carried, out_layer, *refs):
    chunk = GLA_CHUNK
    n_chunks = seq_len // chunk
    total = n_seqs * n_chunks
    n_rows = n_seqs * seq_len
    assert not has_init or n_seqs == 1
    it = iter(refs)
    q_ref, k_ref, v_ref, gr_ref, lr_ref, wz_ref, bz_ref = (next(it) for _ in range(7))
    s0_ref = next(it) if has_init else None
    for _ in range(n_carried):
        next(it)
    out_ref = next(it)
    s_out_ref = next(it) if want_final else None
    if want_final and n_carried == 0:
        s_out_ref[...] = jnp.zeros(s_out_ref.shape, F32)
        s_out_ref = s_out_ref.at[:, out_layer]
    qe_ref, ke_ref, cum_ref, sts_ref, st_ref = (next(it) for _ in range(5))
    unroll = min(GLA_UNROLL, n_chunks)

    x4 = lr_ref[...]
    hi = x4.astype(BF16).astype(F32)
    mid = (x4 - hi).astype(BF16).astype(F32)
    low = ((x4 - hi) - mid).astype(BF16).astype(F32)
    grp = lax.broadcasted_iota(jnp.int32, (1, 128), 1) // (2 * GLA_RANK)
    lhs = jnp.concatenate([jnp.where(grp == 1, mid, jnp.where(grp == 2, low, hi)),
                           jnp.where(grp == 0, mid, jnp.where(grp == 1, hi, 0.0))], axis=1).astype(BF16)
    z = _dot(lhs, wz_ref[...]) + bz_ref[...]
    la = _log_sigmoid(z) / GLA_TAU
    pos = lax.broadcasted_iota(jnp.int32, (n_rows, 1), 0) % chunk
    b_f = _seg_scan(la[:, :128], chunk, pos, jnp.add, 0.0, False)
    b_b = _seg_scan(la[:, 128:], chunk, pos, jnp.add, 0.0, True)
    q = q_ref[...].astype(F32) * (DK ** -0.5)
    k = k_ref[...].astype(F32)
    qe_ref[...] = jnp.concatenate([q * jnp.exp(b_f), q * jnp.exp(b_b)], axis=1).astype(BF16)
    ke_ref[...] = jnp.concatenate([k * jnp.exp(-b_f), k * jnp.exp(-b_b)], axis=1).astype(BF16)
    cum_ref[...] = jnp.concatenate([b_f, b_b], axis=1)

    ii = lax.broadcasted_iota(jnp.int32, (chunk, chunk), 0)
    jj = lax.broadcasted_iota(jnp.int32, (chunk, chunk), 1)

    def rows(c):
        if isinstance(c, int):
            return pl.ds(c * chunk, chunk)
        return pl.ds(pl.multiple_of(c * chunk, chunk), chunk)

    def advance(c, d, slot):
        lanes = slice(128 * d, 128 * d + 128)
        edge = c * chunk + (chunk - 1 if d == 0 else 0)
        decay = cum_ref[pl.ds(edge, 1), :][:, lanes]
        kw = (k_ref[rows(c), :].astype(F32) * jnp.exp(decay - cum_ref[rows(c), :][:, lanes])).astype(BF16)
        st = st_ref[slot]
        sts_ref[c, :, lanes] = st.astype(BF16)
        st_ref[slot] = st * jnp.exp(decay) + _dot_tn(v_ref[rows(c), :], kw)

    if has_init:
        st_ref[0] = s0_ref[0].T
        st_ref[1] = s0_ref[1].T
    else:
        st_ref[...] = jnp.zeros((2 * n_seqs, DV_GLA, DK), F32)

    for seq in range(n_seqs):
        def state_group(g, carry, seq=seq):
            for u in range(unroll):
                c = g * unroll + u
                advance(seq * n_chunks + c, 0, 2 * seq)
                advance(seq * n_chunks + n_chunks - 1 - c, 1, 2 * seq + 1)
            return carry

        if n_chunks // unroll == 1:
            state_group(0, 0)
        else:
            lax.fori_loop(0, n_chunks // unroll, state_group, 0)

    out_unroll = min(GLA_OUT_UNROLL, total)

    def output_group(g, carry):
        for u in range(out_unroll):
            c = g * out_unroll + u
            qe = qe_ref[rows(c), :]
            ke = ke_ref[rows(c), :]
            att = (jnp.where(jj <= ii, _dot_nt(qe[:, :128], ke[:, :128]), 0.0)
                   + jnp.where(jj >= ii, _dot_nt(qe[:, 128:], ke[:, 128:]), 0.0))
            o = _dot(att.astype(BF16), v_ref[rows(c), :]) + _dot_nt(qe, sts_ref[c])
            gr = gr_ref[rows(c), :].astype(F32)
            out_ref[rows(c), :] = (_rms(o) * (gr * jax.nn.sigmoid(gr))).astype(BF16)
        return carry

    if total // out_unroll == 1:
        output_group(0, 0)
    else:
        lax.fori_loop(0, total // out_unroll, output_group, 0)
    if want_final:
        for seq in range(n_seqs):
            s_out_ref[seq, 0] = st_ref[2 * seq].T
            s_out_ref[seq, 1] = st_ref[2 * seq + 1].T


def _gla_mixer(proj, lowrank, wz, bz, group, init, layer_idx=0, carried=None):
    ctx = group == 0
    n_seq, seq_len = (N_CTX_SEQ, CTX_LEN) if ctx else (N_LAT_SEQ, LAT_LEN)
    row_off = 0 if ctx else N_CTX_TOK // LAT_LEN
    has_init, want_final = not ctx, ctx
    n_seqs = GLA_CTX_SEQS if ctx else 1
    n_rows = n_seqs * seq_len
    n_chunks = seq_len // GLA_CHUNK
    total = n_seqs * n_chunks
    in_specs = [pl.BlockSpec((n_rows, 128), lambda b, h: (row_off + b, h)),
                pl.BlockSpec((n_rows, 128), lambda b, h: (row_off + b, 4 + h)),
                pl.BlockSpec((n_rows, 256), lambda b, h: (row_off + b, 4 + h)),
                pl.BlockSpec((n_rows, 256), lambda b, h: (row_off + b, 8 + h)),
                pl.BlockSpec((n_rows, 128), lambda b, h: (row_off + b, 0)),
                pl.BlockSpec((None, 256, 256), lambda b, h: (h, 0, 0)),
                pl.BlockSpec((None, 1, 256), lambda b, h: (h, 0, 0))]
    args = [proj, proj, proj, proj, lowrank, wz, bz]
    if has_init:
        in_specs.append(pl.BlockSpec((None, None, 2, None, DK, DV_GLA), lambda b, h: (b, layer_idx, 0, h, 0, 0)))
        args.append(init)
    out_specs = [pl.BlockSpec((n_rows, 256), lambda b, h: (b, h))]
    out_shape = [jax.ShapeDtypeStruct((n_seq * seq_len, D), BF16)]
    aliases = {}
    if want_final:
        lay, at = (DEPTH // 2, 0) if carried is None else (None, layer_idx)
        out_specs.append(pl.BlockSpec((n_seqs, lay, 2, None, DK, DV_GLA), lambda b, h: (b, at, 0, h, 0, 0)))
        out_shape.append(jax.ShapeDtypeStruct((n_seq, DEPTH // 2, 2, HEADS, DK, DV_GLA), F32))
        if carried is not None:
            aliases = {len(args): 1}
            in_specs.append(pl.BlockSpec(memory_space=pl.ANY))
            args.append(carried)
    scratch = [pltpu.VMEM((n_rows, 256), BF16)] * 2 + [pltpu.VMEM((n_rows, 256), F32),
                                                       pltpu.VMEM((total, DV_GLA, 2 * DK), BF16),
                                                       pltpu.VMEM((2 * n_seqs, DV_GLA, DK), F32)]
    return pl.pallas_call(
        functools.partial(_gla_kernel, seq_len, n_seqs, has_init, want_final, len(aliases), layer_idx),
        grid=(n_seq // n_seqs, HEADS),
        in_specs=in_specs,
        out_specs=out_specs,
        out_shape=out_shape,
        input_output_aliases=aliases,
        scratch_shapes=scratch,
        compiler_params=_cparams("parallel", "parallel"),
        name="gla_mixer_ctx" if ctx else "gla_mixer_lat",
    )(*args)


def _tail_kernel(last, tile, tile_off, n_tiles, n_carried, *refs):
    it = iter(refs)
    hc_ref, hl_ref, hup_ref, hdn_ref, x_ref, xup_ref, xdn_ref = (next(it) for _ in range(7))
    wo_ref, win_ref, cw_ref, cb_ref, w_ref, mod_ref, g_mix_ref, g_pre_ref, g_post_ref = (next(it) for _ in range(9))
    if not last:
        g_next_ref, mod_next_ref = next(it), next(it)
    for _ in range(n_carried):
        next(it)
    x_out_ref = next(it)
    xn_out_ref = None if last else next(it)
    hbuf_ref = next(it)
    i = pl.program_id(0) + tile_off
    n_ctx = N_CTX_TOK // tile
    tiles_per_seq = LAT_LEN // tile
    n_chunks = D_FF // FF_CHUNK

    def cols(f):
        return slice(f * FF_CHUNK, (f + 1) * FF_CHUNK), slice(D_FF + f * FF_CHUNK, D_FF + (f + 1) * FF_CHUNK)

    def act(conv, v, sl):
        return (jax.nn.gelu(conv + cb_ref[:, sl]) * v).astype(BF16)

    def mixer_out(h, x, keep=None):
        m = mod_ref[...]
        x_mid = x + m[2:3] * (_rms(_dot(h, wo_ref[...])) * g_mix_ref[...])
        xn = _rms(x_mid) * g_pre_ref[...] * (1.0 + m[4:5]) + m[3:4]
        if keep is not None:
            xn = jnp.where(keep, xn, 0.0)
        return x_mid, xn.astype(BF16)

    def finish(x_mid):
        x_new = x_mid + mod_ref[5:6, :] * (_rms(_dot(hbuf_ref[...], w_ref[...])) * g_post_ref[...])
        x_out_ref[...] = x_new
        if not last:
            mn = mod_next_ref[...]
            xn_out_ref[...] = (_rms(x_new) * g_next_ref[...] * (1.0 + mn[1:2]) + mn[0:1]).astype(BF16)

    def ctx_branch():
        x_mid, xn = mixer_out(hc_ref[...], x_ref[...])
        pos = lax.broadcasted_iota(jnp.int32, (tile, 1), 0) % CTX_LEN
        for f in range(n_chunks):
            sl, sv = cols(f)
            a = _dot(xn, win_ref[:, sl])
            left = jnp.where(pos >= 1, pltpu.roll(a, 1, 0), 0.0)
            right = jnp.where(pos < CTX_LEN - 1, pltpu.roll(a, tile - 1, 0), 0.0)
            conv = cw_ref[3:4, sl] * left + cw_ref[4:5, sl] * a + cw_ref[5:6, sl] * right
            hbuf_ref[:, sl] = act(conv, _dot(xn, win_ref[:, sv]), sl)
        finish(x_mid)

    def lat_branch():
        t = (i - n_ctx) % tiles_per_seq
        n_rows = tile + 2 * GRID_W
        rowpos = lax.broadcasted_iota(jnp.int32, (n_rows, 1), 0)
        first = jnp.where(t > 0, 0, GRID_W)
        stop = jnp.where(t < tiles_per_seq - 1, n_rows, GRID_W + tile)
        x_mid, xe = mixer_out(jnp.concatenate([hup_ref[...], hl_ref[...], hdn_ref[...]], axis=0),
                              jnp.concatenate([xup_ref[...], x_ref[...], xdn_ref[...]], axis=0),
                              (rowpos >= first) & (rowpos < stop))
        x_mid = x_mid[GRID_W:GRID_W + tile]
        xn = xe[GRID_W:GRID_W + tile]
        colpos = rowpos % GRID_W
        for f in range(n_chunks):
            sl, sv = cols(f)
            buf = _dot(xe, win_ref[:, sl])
            left = jnp.where(colpos >= 1, pltpu.roll(buf, 1, 0), 0.0)
            right = jnp.where(colpos < GRID_W - 1, pltpu.roll(buf, n_rows - 1, 0), 0.0)
            conv = None
            for dr in range(3):
                rs = slice(dr * GRID_W, dr * GRID_W + tile)
                term = (cw_ref[3 * dr:3 * dr + 1, sl] * left[rs] + cw_ref[3 * dr + 1:3 * dr + 2, sl] * buf[rs]
                        + cw_ref[3 * dr + 2:3 * dr + 3, sl] * right[rs])
                conv = term if conv is None else conv + term
            hbuf_ref[:, sl] = act(conv, _dot(xn, win_ref[:, sv]), sl)
        finish(x_mid)

    if tile_off + n_tiles <= n_ctx:
        ctx_branch()
    elif tile_off >= n_ctx:
        lat_branch()
    else:
        pl.when(i < n_ctx)(ctx_branch)
        pl.when(i >= n_ctx)(lat_branch)


def _layer_tail(h_ctx, h_lat, w_o, w_in, conv_w, conv_b, w_out, x, mods, g_mix_post, g_ffn_pre, g_post, g_next,
                layer, group, x_dst=None, xn_dst=None):
    last = layer == DEPTH - 1
    tile = TAIL_TILE_CTX if group == 0 else TAIL_TILE_LAT
    tile_off = 0 if group == 0 else N_CTX_TOK // tile
    n_tiles = (N_CTX_TOK if group == 0 else N_TOK - N_CTX_TOK) // tile
    halo_per_tile = tile // GRID_W
    n_halo = N_TOK // GRID_W
    n_ctx = N_CTX_TOK // tile
    n_lat_halo = (N_TOK - N_CTX_TOK) // GRID_W
    row = pl.BlockSpec((tile, D), lambda i: (i + tile_off, 0))
    resident = dict(pipeline_mode=pl.Buffered(1))
    layer_vec = pl.BlockSpec((None, 1, D), lambda i: (layer, 0, 0))

    def lat_halo(i, shift):
        return jnp.clip((i + tile_off - n_ctx + shift) * halo_per_tile - (1 - shift), 0, n_lat_halo - 1)

    h_ctx_spec = (pl.BlockSpec((tile, D), lambda i: (i, 0)) if group == 0
                  else pl.BlockSpec((GRID_W, D), lambda i: (0, 0)))
    h_lat_spec = (pl.BlockSpec((tile, D), lambda i: (i, 0)) if group == 1
                  else pl.BlockSpec((GRID_W, D), lambda i: (0, 0)))
    in_specs = [h_ctx_spec, h_lat_spec,
                pl.BlockSpec((GRID_W, D), lambda i: (lat_halo(i, 0), 0)),
                pl.BlockSpec((GRID_W, D), lambda i: (lat_halo(i, 1), 0)),
                row,
                pl.BlockSpec((GRID_W, D), lambda i: (jnp.maximum((i + tile_off) * halo_per_tile - 1, 0), 0)),
                pl.BlockSpec((GRID_W, D), lambda i: (jnp.minimum((i + tile_off + 1) * halo_per_tile, n_halo - 1), 0)),
                pl.BlockSpec((D, D), lambda i: (0, 0), **resident),
                pl.BlockSpec((None, D, 2 * D_FF), lambda i: (layer, 0, 0), **resident),
                pl.BlockSpec((None, 9, D_FF), lambda i: (layer, 0, 0)),
                pl.BlockSpec((None, 1, D_FF), lambda i: (layer, 0, 0)),
                pl.BlockSpec((None, D_FF, D), lambda i: (layer, 0, 0), **resident),
                pl.BlockSpec((None, None, 6, D), lambda i: (layer, _mod_row(i + tile_off, tile), 0, 0)),
                layer_vec, layer_vec, layer_vec]
    args = [h_ctx, h_lat, h_lat, h_lat, x, x, x, w_o, w_in, conv_w.reshape(DEPTH, 9, D_FF), conv_b, w_out, mods,
            g_mix_post, g_ffn_pre, g_post]
    aliases = {}
    if last:
        out_specs = [pl.BlockSpec((tile, D), lambda i: (i, 0))]
        out_shape = [jax.ShapeDtypeStruct((n_tiles * tile, D), F32)]
    else:
        in_specs += [pl.BlockSpec((None, 1, D), lambda i: (layer + 1, 0, 0)),
                     pl.BlockSpec((None, None, 6, D), lambda i: (layer + 1, _mod_row(i + tile_off, tile), 0, 0))]
        args += [g_next, mods]
        aliases = {len(args): 0, len(args) + 1: 1}
        in_specs += [pl.BlockSpec(memory_space=pl.ANY)] * 2
        args += [x_dst, xn_dst]
        out_specs = [row, row]
        out_shape = [jax.ShapeDtypeStruct((N_TOK, D), F32), jax.ShapeDtypeStruct((N_TOK, D), BF16)]
    return pl.pallas_call(
        functools.partial(_tail_kernel, last, tile, tile_off, n_tiles, len(aliases)),
        grid=(n_tiles,),
        in_specs=in_specs,
        out_specs=out_specs,
        out_shape=out_shape,
        input_output_aliases=aliases,
        scratch_shapes=[pltpu.VMEM((tile, D_FF), BF16)],
        compiler_params=_cparams("parallel"),
        name="layer_tail",
    )(*args)


def _even_weights(w_gate_cols, w_out):
    src = [0] * (8 * HEADS)
    used = [0.0] * (8 * HEADS)
    for h in range(HEADS):
        for k, col in enumerate((h, 8 + h, 4 + h, 12 + h)):
            src[8 * h + k] = col
            used[8 * h + k] = 1.0
    w_gate = (w_gate_cols[:, jnp.array(src)] * jnp.array(used, F32)).T.astype(BF16)
    w_o = w_out.reshape(2, HEADS, 128, D).transpose(1, 0, 2, 3).reshape(D, D).astype(BF16)
    return w_gate, w_o


def _odd_weights(w_lr_cols, w2, b2):
    w_lr = jnp.tile(w_lr_cols, (1, 128 // (2 * GLA_RANK))).astype(BF16)
    w2h = w2.reshape(2, GLA_RANK, HEADS, DK).transpose(2, 0, 1, 3)
    w32 = jnp.zeros((HEADS, 2 * GLA_RANK, 256), F32)
    w32 = w32.at[:, :GLA_RANK, :DK].set(w2h[:, 0]).at[:, GLA_RANK:, DK:].set(w2h[:, 1])
    hi = w32.astype(BF16)
    mid = (w32 - hi.astype(F32)).astype(BF16)
    low = (w32 - hi.astype(F32) - mid.astype(F32)).astype(BF16)
    none = jnp.zeros_like(hi)
    wz = jnp.concatenate([hi, hi, hi, mid, mid, low, none, none], axis=1)
    bz = b2.reshape(2, HEADS, DK).transpose(1, 0, 2).reshape(HEADS, 1, 2 * DK)
    return w_lr, wz, bz


def kernel(x_prompt, x_sample, c, c_ctx, state_mlstm_C, state_mlstm_n, state_mlstm_m, state_ret, state_gla, ada_w, ada_b, norm_mix_pre, norm_mix_post, norm_ffn_pre, norm_ffn_post, w_in_even, w_out_even, mlstm_igate_b, mlstm_fgate_b, ret_decay_logit, w_in_odd, gla_gate_w2, gla_gate_b, w_out_odd, ffn_w_in, ffn_conv_w, ffn_conv_b, ffn_w_out):
    cond = jnp.concatenate([c_ctx[None, :], c, jnp.zeros((8 - 1 - N_LAT_SEQ, D), F32)], axis=0)
    mods = _mod_table(cond, ada_w, ada_b)
    norm_mix_pre, norm_mix_post, norm_ffn_pre, norm_ffn_post = (
        g.reshape(DEPTH, 1, D) for g in (norm_mix_pre, norm_mix_post, norm_ffn_pre, norm_ffn_post))
    ffn_conv_b = ffn_conv_b.reshape(DEPTH, 1, D_FF)
    ffn_w_in_bf, ffn_w_out_bf = ffn_w_in.astype(BF16), ffn_w_out.astype(BF16)
    w_even_bf = jnp.concatenate([w_in_even[:, :, :2048], w_in_even[:, :, 2064:]], axis=2).astype(BF16)
    w_odd_bf = w_in_odd.astype(BF16)
    gate_cols = w_in_even[:, :, 2048:2064]
    lr_cols = w_in_odd[:, :, 3072:3072 + 2 * GLA_RANK]

    x, xn = _prenorm(x_prompt.reshape(N_CTX_TOK, D), x_sample.reshape(N_LAT_SEQ * LAT_LEN, D),
                     norm_mix_pre, mods, 0)
    x_spare = jnp.zeros((N_TOK, D), F32)
    even_finals, gla_finals = None, None
    for layer in range(DEPTH):
        idx = layer // 2
        if layer % 2 == 0:
            w_gate, w_o = _even_weights(gate_cols[idx], w_out_even[idx])
            proj, gates = _proj(xn, w_even_bf, idx, 4096, w_gate, True, "proj_even")
            params = jnp.concatenate([mlstm_igate_b[idx].reshape(-1), mlstm_fgate_b[idx].reshape(-1),
                                      ret_decay_logit[idx].reshape(-1)])
            h_ctx, *even_finals = _even_mixer(proj, gates, params, 0, None, idx, even_finals)
            init = (state_mlstm_C, state_mlstm_n, state_mlstm_m, state_ret)
            (h_lat,) = _even_mixer(proj, gates, params, 1, init, idx)
        else:
            w_lr, wz, bz = _odd_weights(lr_cols[idx], gla_gate_w2[idx], gla_gate_b[idx])
            w_o = w_out_odd[idx].astype(BF16)
            proj, lowrank = _proj(xn, w_odd_bf, idx, 3072, w_lr, False, "proj_odd")
            h_ctx, gla_finals = _gla_mixer(proj, lowrank, wz, bz, 0, None, idx, gla_finals)
            (h_lat,) = _gla_mixer(proj, lowrank, wz, bz, 1, state_gla, idx)
        tail = functools.partial(_layer_tail, h_ctx, h_lat, w_o, ffn_w_in_bf, ffn_conv_w, ffn_conv_b, ffn_w_out_bf,
                                 x, mods, norm_mix_post, norm_ffn_pre, norm_ffn_post, norm_mix_pre, layer)
        if layer < DEPTH - 1:
            x_new, xn_new = tail(0, x_spare, xn)
            x_new, xn = tail(1, x_new, xn_new)
            x, x_spare = x_new, x
        else:
            (y_ctx,) = tail(0)
            (y_lat,) = tail(1)

    y_prompt = y_ctx.reshape(N_CTX_SEQ, CTX_LEN, D)
    y_sample = y_lat.reshape(N_LAT_SEQ, LAT_LEN, D)
    new_c, new_n, new_m, new_r = even_finals
    return (y_prompt, y_sample, new_c, new_n[..., 0, :], new_m[:, :, :, 0:2, 0].transpose(0, 1, 3, 2), new_r, gla_finals)
```

```python
import functools
import math

import jax
import jax.numpy as jnp
from jax import lax
from jax.experimental import pallas as pl
from jax.experimental.pallas import tpu as pltpu

F32 = jnp.float32
BF16 = jnp.bfloat16

D = 1024
DEPTH = 4
N_CTX_SEQ, CTX_LEN = 16, 256
N_LAT_SEQ, LAT_LEN = 2, 2048
GRID_W = 64
N_CTX_TOK = N_CTX_SEQ * CTX_LEN
N_TOK = N_CTX_TOK + N_LAT_SEQ * LAT_LEN
HEADS = 4
DK = 128
DV_GLA = 256
GLA_RANK = 16
GLA_TAU = 16.0
D_FF = 2816
EPS = 1e-6
NEG = -1e30

ROW_TILE = 1024
PROJ_ROW_TILE = 1024
PROJ_COL_TILE = 1024
EVEN_CHUNK = 256
EVEN_UNROLL = 4
EVEN_CTX_SEQS = 8
GLA_CHUNK = 64
GLA_UNROLL = 8
GLA_CTX_SEQS = 8
GLA_OUT_UNROLL = 16
FF_CHUNK = 256
TAIL_TILE_CTX = 512
TAIL_TILE_LAT = 512
VMEM_LIMIT = 56 * 1024 * 1024


def _cparams(*sem):
    return pltpu.CompilerParams(dimension_semantics=sem, vmem_limit_bytes=VMEM_LIMIT)


def _mod_row(i, tile):
    n_ctx = N_CTX_TOK // tile
    return jnp.where(i < n_ctx, 0, 1 + (i - n_ctx) // (LAT_LEN // tile))


def _rms(x):
    return x * lax.rsqrt(jnp.mean(x * x, axis=-1, keepdims=True) + EPS)


def _log_sigmoid(x):
    return jnp.minimum(x, 0.0) - jnp.log(1.0 + jnp.exp(-jnp.abs(x)))


def _dot(a, b):
    return jnp.dot(a, b, preferred_element_type=F32)


def _dot_nt(a, b):
    return lax.dot_general(a, b, (((1,), (1,)), ((), ())), preferred_element_type=F32)


def _dot_tn(a, b):
    return lax.dot_general(a, b, (((0,), (0,)), ((), ())), preferred_element_type=F32)


def _seg_scan(x, seg, pos, op, ident, reverse):
    n = x.shape[0]
    s = 1
    while s < seg:
        if reverse:
            x = op(x, jnp.where(pos < seg - s, pltpu.roll(x, n - s, 0), ident))
        else:
            x = op(x, jnp.where(pos >= s, pltpu.roll(x, s, 0), ident))
        s *= 2
    return x


def _mod_kernel(c_ref, w_ref, b_ref, o_ref):
    c = c_ref[...]
    s = (c * jax.nn.sigmoid(c)).astype(BF16)
    o_ref[0] = _dot(s, w_ref[0].astype(BF16)) + b_ref[0]


def _mod_table(cond, ada_w, ada_b):
    tn = 2048
    out = pl.pallas_call(
        _mod_kernel,
        grid=(DEPTH, 6 * D // tn),
        in_specs=[pl.BlockSpec((8, D), lambda l, j: (0, 0)),
                  pl.BlockSpec((1, D, tn), lambda l, j: (l, 0, j)),
                  pl.BlockSpec((1, 1, tn), lambda l, j: (l, 0, j))],
        out_specs=pl.BlockSpec((1, 8, tn), lambda l, j: (l, 0, j)),
        out_shape=jax.ShapeDtypeStruct((DEPTH, 8, 6 * D), F32),
        compiler_params=_cparams("parallel", "parallel"),
        name="mod_table",
    )(cond, ada_w, ada_b.reshape(DEPTH, 1, 6 * D))
    return out.reshape(DEPTH, 8, 6, D)


def _prenorm_kernel(xp_ref, xs_ref, g_ref, mod_ref, x_ref, xn_ref):
    i = pl.program_id(0)

    def run(src_ref):
        x = src_ref[...]
        m = mod_ref[...]
        x_ref[...] = x
        xn_ref[...] = (_rms(x) * g_ref[...] * (1.0 + m[1:2]) + m[0:1]).astype(BF16)

    pl.when(i < N_CTX_TOK // ROW_TILE)(lambda: run(xp_ref))
    pl.when(i >= N_CTX_TOK // ROW_TILE)(lambda: run(xs_ref))


def _prenorm(xp, xs, g, mods, layer):
    nc = N_CTX_TOK // ROW_TILE
    return pl.pallas_call(
        _prenorm_kernel,
        grid=(N_TOK // ROW_TILE,),
        in_specs=[pl.BlockSpec((ROW_TILE, D), lambda i: (jnp.minimum(i, nc - 1), 0)),
                  pl.BlockSpec((ROW_TILE, D), lambda i: (jnp.maximum(i - nc, 0), 0)),
                  pl.BlockSpec((None, 1, D), lambda i: (layer, 0, 0)),
                  pl.BlockSpec((None, None, 6, D), lambda i: (layer, _mod_row(i, ROW_TILE), 0, 0))],
        out_specs=[pl.BlockSpec((ROW_TILE, D), lambda i: (i, 0)),
                   pl.BlockSpec((ROW_TILE, D), lambda i: (i, 0))],
        out_shape=[jax.ShapeDtypeStruct((N_TOK, D), F32), jax.ShapeDtypeStruct((N_TOK, D), BF16)],
        compiler_params=_cparams("parallel"),
        name="prenorm",
    )(xp, xs, g, mods)


def _proj_kernel(small_transposed, x_ref, w_ref, ws_ref, o_ref, os_ref):
    x = x_ref[...]
    o_ref[...] = _dot(x, w_ref[...]).astype(o_ref.dtype)

    @pl.when(pl.program_id(1) == 0)
    def _():
        os_ref[...] = _dot_nt(ws_ref[...], x) if small_transposed else _dot(x, ws_ref[...])


def _proj(xn, w_all, layer_idx, n, w_small, small_transposed, name):
    tm, tn = PROJ_ROW_TILE, PROJ_COL_TILE
    m, k = xn.shape
    if small_transposed:
        r = w_small.shape[0]
        small_spec = pl.BlockSpec((r, tm), lambda i, j: (0, i))
        small_shape = jax.ShapeDtypeStruct((r, m), F32)
    else:
        small_spec = pl.BlockSpec((tm, w_small.shape[1]), lambda i, j: (i, 0))
        small_shape = jax.ShapeDtypeStruct((m, w_small.shape[1]), F32)
    return pl.pallas_call(
        functools.partial(_proj_kernel, small_transposed),
        grid=(m // tm, n // tn),
        in_specs=[pl.BlockSpec((tm, k), lambda i, j: (i, 0)),
                  pl.BlockSpec((None, k, tn), lambda i, j: (layer_idx, 0, j)),
                  pl.BlockSpec(w_small.shape, lambda i, j: (0, 0))],
        out_specs=[pl.BlockSpec((tm, tn), lambda i, j: (i, j)), small_spec],
        out_shape=[jax.ShapeDtypeStruct((m, n), BF16), small_shape],
        compiler_params=_cparams("parallel", "arbitrary"),
        name=name,
    )(xn, w_all, w_small)


def _even_kernel(seq_len, n_seqs, init_layer, want_final, n_carried, out_layer, *refs):
    has_init = init_layer is not None
    layer_idx, n_layers = init_layer if has_init else (0, 1)
    chunk = EVEN_CHUNK
    n_chunks = seq_len // chunk
    total = n_seqs * n_chunks
    use_cross = has_init or n_chunks > 1
    unroll = min(EVEN_UNROLL, n_chunks)
    assert not has_init or n_seqs == 1
    it = iter(refs)
    par_ref = next(it)
    m0_ref = next(it) if has_init else None
    mq_ref, mk_ref, mv_ref, mo_ref, rq_ref, rk_ref, rv_ref, rg_ref, gate_ref = (next(it) for _ in range(9))
    if has_init:
        c0_ref, n0_ref, r0_ref = next(it), next(it), next(it)
    for _ in range(n_carried):
        next(it)
    out_ref = next(it)
    if want_final:
        finals = [next(it) for _ in range(4)]
        if n_carried == 0:
            for ref in finals:
                ref[...] = jnp.zeros(ref.shape, F32)
            finals = [ref.at[:, out_layer] for ref in finals]
        c_out_ref, n_out_ref, m_out_ref, r_out_ref = finals
    grow_ref, rep_ref, edge_ref, cn_ref, s_ref = (next(it) for _ in range(5))
    if use_cross:
        cns_ref, ss_ref, ms_ref = next(it), next(it), next(it)

    b = pl.program_id(0)
    h = pl.program_id(1)
    ii = lax.broadcasted_iota(jnp.int32, (chunk, chunk), 0)
    jj = lax.broadcasted_iota(jnp.int32, (chunk, chunk), 1)
    pos_col = lax.broadcasted_iota(jnp.int32, (chunk, 1), 0).astype(F32)
    pos_int = lax.broadcasted_iota(jnp.int32, (chunk, 1), 0)
    ones_blk = jnp.ones((chunk, 128), BF16)
    ones_sum = jnp.ones((256, 128), BF16)

    sub = lax.broadcasted_iota(jnp.int32, (8, 1), 0)
    bias = jnp.where(sub == 0, par_ref[h], jnp.where(sub == 1, par_ref[4 + h],
                     jnp.where(sub == 2, par_ref[8 + h], jnp.where(sub == 3, par_ref[12 + h], 0.0))))
    lane = lax.broadcasted_iota(jnp.int32, (1, 128), 1)
    tri_pre = (ii <= jj).astype(BF16)
    tri_suf = (ii >= jj).astype(BF16)
    def lanes(c):
        if isinstance(c, int):
            return slice(c * chunk, (c + 1) * chunk)
        return pl.ds(pl.multiple_of(c * chunk, chunk), chunk)

    def rows(c):
        if isinstance(c, int):
            return pl.ds(c * chunk, chunk)
        return pl.ds(pl.multiple_of(c * chunk, chunk), chunk)

    def prepare_chunk(c):
        pre = gate_ref[:, lanes(c)] + bias
        log_f = pltpu.roll(_log_sigmoid(pre), 6, 0)
        hi = log_f.astype(BF16)
        rest = log_f - hi.astype(F32)
        mid = rest.astype(BF16)
        low = (rest - mid.astype(F32)).astype(BF16)
        parts = jnp.concatenate([hi, mid, low, jnp.zeros_like(hi)], axis=0)
        cum_pre = _dot(parts, tri_pre)
        cum_suf = _dot(parts, tri_suf)
        fcum = jnp.where(sub == 0, cum_pre[0:8] + cum_pre[8:16] + cum_pre[16:24],
                         cum_suf[0:8] + cum_suf[8:16] + cum_suf[16:24])
        gsc = pre - fcum
        packed = jnp.where(sub < 2, gsc, pltpu.roll(fcum, 2, 0))
        grow_ref[:, lanes(c)] = packed
        pcol = packed.T
        g_rep = [jnp.broadcast_to(pcol[:, k:k + 1], (chunk, 128)) for k in range(4)]
        g_rep.append(_seg_scan(g_rep[0], chunk, pos_int, jnp.maximum, NEG, False))
        g_rep.append(_seg_scan(g_rep[1], chunk, pos_int, jnp.maximum, NEG, True))
        rep_ref[rows(c), :] = jnp.concatenate(g_rep, axis=1)
        f_tot = jnp.sum(log_f, axis=1, keepdims=True)
        g_top = jnp.max(gsc, axis=1, keepdims=True)
        edge_ref[c] = jnp.where(lane == 0, f_tot, jnp.where(lane == 1, g_top, 0.0))

    def for_all_chunks(body):
        if total <= 2 * EVEN_UNROLL:
            for c in range(total):
                body(c)
        else:
            def group(g, carry):
                for u in range(EVEN_UNROLL):
                    body(g * EVEN_UNROLL + u)
                return carry
            lax.fori_loop(0, total // EVEN_UNROLL, group, 0)

    for_all_chunks(prepare_chunk)

    k_scale = DK ** -0.5
    lg_f = _log_sigmoid(jnp.full((1, 1), par_ref[16 + h], F32))
    lg_b = _log_sigmoid(jnp.full((1, 1), par_ref[20 + h], F32))
    diff = (ii - jj).astype(F32)
    ret_mask = k_scale * (jnp.where(diff >= 0, jnp.exp(lg_f * jnp.maximum(diff, 0.0)), 0.0)
                          + jnp.where(diff <= 0, jnp.exp(lg_b * jnp.maximum(-diff, 0.0)), 0.0))
    rq_dec = (jnp.exp(lg_f * (pos_col + 1.0)), jnp.exp(lg_b * (chunk - pos_col)))
    rk_dec = (k_scale * jnp.exp(lg_f * (chunk - 1.0 - pos_col)), k_scale * jnp.exp(lg_b * pos_col))
    r_chunk = (jnp.exp(lg_f * chunk), jnp.exp(lg_b * chunk))

    def load_kv(c):
        return mk_ref[rows(c), :], jnp.concatenate([mv_ref[rows(c), :], ones_blk], axis=1)

    def ret_kv(c):
        return rk_ref[rows(c), :], rv_ref[rows(c), :]

    def advance(c, d, m, slot):
        if use_cross:
            cns_ref[c, :, 256 * d:256 * d + 256] = cn_ref[slot].astype(BF16)
            ss_ref[c, 128 * d:128 * d + 128, :] = s_ref[slot].astype(BF16)
            ms_ref[c, d:d + 1, :] = jnp.broadcast_to(m, (1, 128))
        edge = edge_ref[c]
        mu = jnp.maximum(m, edge[d:d + 1, 1:2])
        k, vext = load_kv(c)
        kw = (k.astype(F32) * (k_scale * jnp.exp(rep_ref[rows(c), 128 * d:128 * d + 128] - mu))).astype(BF16)
        cn_ref[slot] = jnp.exp(m - mu) * cn_ref[slot] + _dot_tn(kw, vext)
        rk, rv = ret_kv(c)
        s_ref[slot] = r_chunk[d] * s_ref[slot] + _dot_tn((rk.astype(F32) * rk_dec[d]).astype(BF16), rv)
        return edge[d:d + 1, 0:1] + mu

    if has_init:
        base = (b * n_layers + layer_idx) * 8
        m_init = (jnp.full((1, 1), m0_ref[base + h], F32), jnp.full((1, 1), m0_ref[base + 4 + h], F32))
        for d in range(2):
            cn_ref[d] = jnp.concatenate([c0_ref[d], jnp.broadcast_to(n0_ref[d], (DK, 128))], axis=1)
            s_ref[d] = r0_ref[d]
    else:
        m_init = (jnp.zeros((1, 1), F32), jnp.zeros((1, 1), F32))
        cn_ref[...] = jnp.zeros((2 * n_seqs, DK, 256), F32)
        s_ref[...] = jnp.zeros((2 * n_seqs, DK, DK), F32)

    m_fin = []
    for seq in range(n_seqs):
        def state_group(g, carry, seq=seq):
            m_f, m_b = carry
            for u in range(unroll):
                c = g * unroll + u
                m_f = advance(seq * n_chunks + c, 0, m_f, 2 * seq)
                m_b = advance(seq * n_chunks + n_chunks - 1 - c, 1, m_b, 2 * seq + 1)
            return m_f, m_b

        if n_chunks // unroll > 1:
            m_fin.append(lax.fori_loop(0, n_chunks // unroll, state_group, m_init))
        elif use_cross or want_final:
            m_fin.append(state_group(0, m_init))

    def rms_rep(x):
        sq = x * x
        hi = sq.astype(BF16)
        low = (sq - hi.astype(F32)).astype(BF16)
        ssum = _dot(jnp.concatenate([hi, low], axis=1), ones_sum)
        return x * lax.rsqrt(ssum * (1.0 / 128) + EPS)

    def output_chunk(c):
        q = mq_ref[rows(c), :]
        k, vext = load_kv(c)
        scores = _dot_nt(q, k)
        cols = rep_ref[rows(c), :]
        f_f, f_b, gm_f, gm_b = cols[:, 256:384], cols[:, 384:512], cols[:, 512:640], cols[:, 640:768]
        row = grow_ref[:, lanes(c)] + math.log(k_scale)
        if use_cross:
            m_f = ms_ref[c, 0:1, :][:, 0:1]
            m_b = ms_ref[c, 1:2, :][:, 0:1]
        else:
            m_f, m_b = m_init
        mu_f = jnp.maximum(m_f, gm_f)
        mu_b = jnp.maximum(m_b, gm_b)
        p_f = jnp.exp(jnp.where(jj <= ii, row[0:1, :], NEG) - jnp.concatenate([mu_f, mu_f], axis=1)) * scores
        p_b = jnp.exp(jnp.where(jj >= ii, row[1:2, :], NEG) - jnp.concatenate([mu_b, mu_b], axis=1)) * scores
        tot = _dot(jnp.concatenate([p_f, p_b], axis=0).astype(BF16), vext)
        num_f, den_f, num_b, den_b = tot[:chunk, :128], tot[:chunk, 128:], tot[chunk:, :128], tot[chunk:, 128:]
        if use_cross:
            cross = _dot(q, cns_ref[c])
            w_f = jnp.exp(m_f - mu_f)
            w_b = jnp.exp(m_b - mu_b)
            num_f, den_f = num_f + w_f * cross[:, 0:128], den_f + w_f * cross[:, 128:256]
            num_b, den_b = num_b + w_b * cross[:, 256:384], den_b + w_b * cross[:, 384:512]
        h_f = num_f / jnp.maximum(jnp.abs(den_f), jnp.exp(-(f_f + mu_f)))
        h_b = num_b / jnp.maximum(jnp.abs(den_b), jnp.exp(-(f_b + mu_b)))
        out_m = rms_rep(h_f + h_b) * jax.nn.sigmoid(mo_ref[rows(c), :].astype(F32))
        rq = rq_ref[rows(c), :]
        rk, rv = ret_kv(c)
        o_r = _dot((_dot_nt(rq, rk) * ret_mask).astype(BF16), rv)
        if use_cross:
            rq32 = rq.astype(F32)
            q_in = jnp.concatenate([rq32 * rq_dec[0], rq32 * rq_dec[1]], axis=1).astype(BF16)
            o_r = o_r + _dot(q_in, ss_ref[c])
        rg = rg_ref[rows(c), :].astype(F32)
        out_r = rms_rep(o_r) * (rg * jax.nn.sigmoid(rg))
        out_ref[rows(c), :] = jnp.concatenate([out_m, out_r], axis=1).astype(BF16)

    for_all_chunks(output_chunk)

    if want_final:
        for seq in range(n_seqs):
            for d in range(2):
                cn = cn_ref[2 * seq + d]
                c_out_ref[seq, d] = cn[:, :DK]
                n_out_ref[seq, d] = cn[:, DK:].T[0:8, :]
                r_out_ref[seq, d] = s_ref[2 * seq + d]
            m_out_ref[seq] = jnp.concatenate([jnp.broadcast_to(m_fin[seq][0], (1, 128)),
                                              jnp.broadcast_to(m_fin[seq][1], (1, 128)), jnp.zeros((6, 128), F32)], axis=0)


def _even_mixer(proj, gates_t, params, group, init, layer_idx=0, carried=None):
    ctx = group == 0
    n_seq, seq_len = (N_CTX_SEQ, CTX_LEN) if ctx else (N_LAT_SEQ, LAT_LEN)
    row_off = 0 if ctx else N_CTX_TOK // LAT_LEN
    has_init, want_final = not ctx, ctx
    n_seqs = EVEN_CTX_SEQS if ctx else 1
    rows = n_seqs * seq_len
    n_chunks = seq_len // EVEN_CHUNK
    total = n_seqs * n_chunks
    use_cross = has_init or n_chunks > 1

    def col(k):
        return pl.BlockSpec((rows, 128), lambda b, h: (row_off + b, 4 * k + h))

    smem = pl.BlockSpec(memory_space=pltpu.SMEM)
    in_specs = [smem]
    args = [params]
    if has_init:
        c0, n0, m0, r0 = init
        n_layers = c0.shape[1]
        in_specs.append(smem)
        args.append(m0.reshape(-1))
    in_specs += [col(k) for k in range(8)]
    in_specs.append(pl.BlockSpec((8, rows), lambda b, h: (h, row_off + b)))
    args += [proj] * 8 + [gates_t]
    if has_init:
        in_specs += [pl.BlockSpec((None, None, 2, None, DK, 128), lambda b, h: (b, layer_idx, 0, h, 0, 0)),
                     pl.BlockSpec((None, None, 2, None, DK, 1), lambda b, h: (b, layer_idx, 0, h, 0, 0)),
                     pl.BlockSpec((None, None, 2, None, DK, 128), lambda b, h: (b, layer_idx, 0, h, 0, 0))]
        args += [c0, n0.reshape(n0.shape + (1,)), r0]
    out_specs = [pl.BlockSpec((rows, 256), lambda b, h: (b, h))]
    out_shape = [jax.ShapeDtypeStruct((n_seq * seq_len, D), BF16)]
    aliases = {}
    if want_final:
        n_even = (DEPTH + 1) // 2
        if carried is None:
            lay, at = n_even, 0
        else:
            lay, at = None, layer_idx
        out_specs += [pl.BlockSpec((n_seqs, lay, 2, None, DK, DK), lambda b, h: (b, at, 0, h, 0, 0)),
                      pl.BlockSpec((n_seqs, lay, 2, None, 8, DK), lambda b, h: (b, at, 0, h, 0, 0)),
                      pl.BlockSpec((n_seqs, lay, None, 8, 128), lambda b, h: (b, at, h, 0, 0)),
                      pl.BlockSpec((n_seqs, lay, 2, None, DK, DK), lambda b, h: (b, at, 0, h, 0, 0))]
        out_shape += [jax.ShapeDtypeStruct((n_seq, n_even, 2, HEADS, DK, DK), F32),
                      jax.ShapeDtypeStruct((n_seq, n_even, 2, HEADS, 8, DK), F32),
                      jax.ShapeDtypeStruct((n_seq, n_even, HEADS, 8, 128), F32),
                      jax.ShapeDtypeStruct((n_seq, n_even, 2, HEADS, DK, DK), F32)]
        if carried is not None:
            aliases = {len(args) + k: 1 + k for k in range(len(carried))}
            in_specs += [pl.BlockSpec(memory_space=pl.ANY)] * len(carried)
            args += list(carried)
    scratch = [pltpu.VMEM((8, rows), F32), pltpu.VMEM((rows, 768), F32), pltpu.VMEM((total, 8, 128), F32),
               pltpu.VMEM((2 * n_seqs, DK, 256), F32), pltpu.VMEM((2 * n_seqs, DK, DK), F32)]
    if use_cross:
        scratch += [pltpu.VMEM((total, DK, 512), BF16), pltpu.VMEM((total, 2 * DK, DK), BF16),
                    pltpu.VMEM((total, 8, 128), F32)]
    return pl.pallas_call(
        functools.partial(_even_kernel, seq_len, n_seqs, (layer_idx, n_layers) if has_init else None, want_final,
                          len(aliases), layer_idx),
        grid=(n_seq // n_seqs, HEADS),
        in_specs=in_specs,
        out_specs=out_specs,
        out_shape=out_shape,
        input_output_aliases=aliases,
        scratch_shapes=scratch,
        compiler_params=_cparams("parallel", "parallel"),
        name="even_mixer_ctx" if ctx else "even_mixer_lat",
    )(*args)


def _gla_kernel(seq_len, n_seqs, has_init, want_final, n_carried, out_layer, *refs):
    chunk = GLA_CHUNK
    n_chunks = seq_len // chunk
    total = n_seqs * n_chunks
    n_rows = n_seqs * seq_len
    assert not has_init or n_seqs == 1
    it = iter(refs)
    q_ref, k_ref, v_ref, gr_ref, lr_ref, wz_ref, bz_ref = (next(it) for _ in range(7))
    s0_ref = next(it) if has_init else None
    for _ in range(n_carried):
        next(it)
    out_ref = next(it)
    s_out_ref = next(it) if want_final else None
    if want_final and n_carried == 0:
        s_out_ref[...] = jnp.zeros(s_out_ref.shape, F32)
        s_out_ref = s_out_ref.at[:, out_layer]
    qe_ref, ke_ref, cum_ref, sts_ref, st_ref = (next(it) for _ in range(5))
    unroll = min(GLA_UNROLL, n_chunks)

    x4 = lr_ref[...]
    hi = x4.astype(BF16).astype(F32)
    mid = (x4 - hi).astype(BF16).astype(F32)
    low = ((x4 - hi) - mid).astype(BF16).astype(F32)
    grp = lax.broadcasted_iota(jnp.int32, (1, 128), 1) // (2 * GLA_RANK)
    lhs = jnp.concatenate([jnp.where(grp == 1, mid, jnp.where(grp == 2, low, hi)),
                           jnp.where(grp == 0, mid, jnp.where(grp == 1, hi, 0.0))], axis=1).astype(BF16)
    z = _dot(lhs, wz_ref[...]) + bz_ref[...]
    la = _log_sigmoid(z) / GLA_TAU
    pos = lax.broadcasted_iota(jnp.int32, (n_rows, 1), 0) % chunk
    b_f = _seg_scan(la[:, :128], chunk, pos, jnp.add, 0.0, False)
    b_b = _seg_scan(la[:, 128:], chunk, pos, jnp.add, 0.0, True)
    q = q_ref[...].astype(F32) * (DK ** -0.5)
    k = k_ref[...].astype(F32)
    qe_ref[...] = jnp.concatenate([q * jnp.exp(b_f), q * jnp.exp(b_b)], axis=1).astype(BF16)
    ke_ref[...] = jnp.concatenate([k * jnp.exp(-b_f), k * jnp.exp(-b_b)], axis=1).astype(BF16)
    cum_ref[...] = jnp.concatenate([b_f, b_b], axis=1)

    ii = lax.broadcasted_iota(jnp.int32, (chunk, chunk), 0)
    jj = lax.broadcasted_iota(jnp.int32, (chunk, chunk), 1)

    def rows(c):
        if isinstance(c, int):
            return pl.ds(c * chunk, chunk)
        return pl.ds(pl.multiple_of(c * chunk, chunk), chunk)

    def advance(c, d, slot):
        lanes = slice(128 * d, 128 * d + 128)
        edge = c * chunk + (chunk - 1 if d == 0 else 0)
        decay = cum_ref[pl.ds(edge, 1), :][:, lanes]
        kw = (k_ref[rows(c), :].astype(F32) * jnp.exp(decay - cum_ref[rows(c), :][:, lanes])).astype(BF16)
        st = st_ref[slot]
        sts_ref[c, :, lanes] = st.astype(BF16)
        st_ref[slot] = st * jnp.exp(decay) + _dot_tn(v_ref[rows(c), :], kw)

    if has_init:
        st_ref[0] = s0_ref[0].T
        st_ref[1] = s0_ref[1].T
    else:
        st_ref[...] = jnp.zeros((2 * n_seqs, DV_GLA, DK), F32)

    for seq in range(n_seqs):
        def state_group(g, carry, seq=seq):
            for u in range(unroll):
                c = g * unroll + u
                advance(seq * n_chunks + c, 0, 2 * seq)
                advance(seq * n_chunks + n_chunks - 1 - c, 1, 2 * seq + 1)
            return carry

        if n_chunks // unroll == 1:
            state_group(0, 0)
        else:
            lax.fori_loop(0, n_chunks // unroll, state_group, 0)

    out_unroll = min(GLA_OUT_UNROLL, total)

    def output_group(g, carry):
        for u in range(out_unroll):
            c = g * out_unroll + u
            qe = qe_ref[rows(c), :]
            ke = ke_ref[rows(c), :]
            att = (jnp.where(jj <= ii, _dot_nt(qe[:, :128], ke[:, :128]), 0.0)
                   + jnp.where(jj >= ii, _dot_nt(qe[:, 128:], ke[:, 128:]), 0.0))
            o = _dot(att.astype(BF16), v_ref[rows(c), :]) + _dot_nt(qe, sts_ref[c])
            gr = gr_ref[rows(c), :].astype(F32)
            out_ref[rows(c), :] = (_rms(o) * (gr * jax.nn.sigmoid(gr))).astype(BF16)
        return carry

    if total // out_unroll == 1:
        output_group(0, 0)
    else:
        lax.fori_loop(0, total // out_unroll, output_group, 0)
    if want_final:
        for seq in range(n_seqs):
            s_out_ref[seq, 0] = st_ref[2 * seq].T
            s_out_ref[seq, 1] = st_ref[2 * seq + 1].T


def _gla_mixer(proj, lowrank, wz, bz, group, init, layer_idx=0, carried=None):
    ctx = group == 0
    n_seq, seq_len = (N_CTX_SEQ, CTX_LEN) if ctx else (N_LAT_SEQ, LAT_LEN)
    row_off = 0 if ctx else N_CTX_TOK // LAT_LEN
    has_init, want_final = not ctx, ctx
    n_seqs = GLA_CTX_SEQS if ctx else 1
    n_rows = n_seqs * seq_len
    n_chunks = seq_len // GLA_CHUNK
    total = n_seqs * n_chunks
    in_specs = [pl.BlockSpec((n_rows, 128), lambda b, h: (row_off + b, h)),
                pl.BlockSpec((n_rows, 128), lambda b, h: (row_off + b, 4 + h)),
                pl.BlockSpec((n_rows, 256), lambda b, h: (row_off + b, 4 + h)),
                pl.BlockSpec((n_rows, 256), lambda b, h: (row_off + b, 8 + h)),
                pl.BlockSpec((n_rows, 128), lambda b, h: (row_off + b, 0)),
                pl.BlockSpec((None, 256, 256), lambda b, h: (h, 0, 0)),
                pl.BlockSpec((None, 1, 256), lambda b, h: (h, 0, 0))]
    args = [proj, proj, proj, proj, lowrank, wz, bz]
    if has_init:
        in_specs.append(pl.BlockSpec((None, None, 2, None, DK, DV_GLA), lambda b, h: (b, layer_idx, 0, h, 0, 0)))
        args.append(init)
    out_specs = [pl.BlockSpec((n_rows, 256), lambda b, h: (b, h))]
    out_shape = [jax.ShapeDtypeStruct((n_seq * seq_len, D), BF16)]
    aliases = {}
    if want_final:
        lay, at = (DEPTH // 2, 0) if carried is None else (None, layer_idx)
        out_specs.append(pl.BlockSpec((n_seqs, lay, 2, None, DK, DV_GLA), lambda b, h: (b, at, 0, h, 0, 0)))
        out_shape.append(jax.ShapeDtypeStruct((n_seq, DEPTH // 2, 2, HEADS, DK, DV_GLA), F32))
        if carried is not None:
            aliases = {len(args): 1}
            in_specs.append(pl.BlockSpec(memory_space=pl.ANY))
            args.append(carried)
    scratch = [pltpu.VMEM((n_rows, 256), BF16)] * 2 + [pltpu.VMEM((n_rows, 256), F32),
                                                       pltpu.VMEM((total, DV_GLA, 2 * DK), BF16),
                                                       pltpu.VMEM((2 * n_seqs, DV_GLA, DK), F32)]
    return pl.pallas_call(
        functools.partial(_gla_kernel, seq_len, n_seqs, has_init, want_final, len(aliases), layer_idx),
        grid=(n_seq // n_seqs, HEADS),
        in_specs=in_specs,
        out_specs=out_specs,
        out_shape=out_shape,
        input_output_aliases=aliases,
        scratch_shapes=scratch,
        compiler_params=_cparams("parallel", "parallel"),
        name="gla_mixer_ctx" if ctx else "gla_mixer_lat",
    )(*args)


def _tail_kernel(layer, tile, tile_off, n_tiles, n_carried, *refs):
    last = layer == DEPTH - 1
    it = iter(refs)
    hc_ref, hl_ref, hup_ref, hdn_ref, x_ref, xup_ref, xdn_ref = (next(it) for _ in range(7))
    wo_ref, win_ref, cw_ref, cb_ref, w_ref, mod_ref, g_mix_ref, g_pre_ref, g_post_ref = (next(it) for _ in range(9))
    if not last:
        g_next_ref, mod_next_ref = next(it), next(it)
    for _ in range(n_carried):
        next(it)
    x_out_ref = next(it)
    xn_out_ref = None if last else next(it)
    hbuf_ref = next(it)
    i = pl.program_id(0) + tile_off
    n_ctx = N_CTX_TOK // tile
    tiles_per_seq = LAT_LEN // tile
    n_chunks = D_FF // FF_CHUNK

    def cols(f):
        return slice(f * FF_CHUNK, (f + 1) * FF_CHUNK), slice(D_FF + f * FF_CHUNK, D_FF + (f + 1) * FF_CHUNK)

    def act(conv, v, sl):
        return (jax.nn.gelu(conv + cb_ref[:, sl]) * v).astype(BF16)

    def mixer_out(h, x, keep=None):
        m = mod_ref[...]
        x_mid = x + m[2:3] * (_rms(_dot(h, wo_ref[...])) * g_mix_ref[...])
        xn = _rms(x_mid) * g_pre_ref[...] * (1.0 + m[4:5]) + m[3:4]
        if keep is not None:
            xn = jnp.where(keep, xn, 0.0)
        return x_mid, xn.astype(BF16)

    def finish(x_mid):
        x_new = x_mid + mod_ref[5:6, :] * (_rms(_dot(hbuf_ref[...], w_ref[...])) * g_post_ref[...])
        x_out_ref[...] = x_new
        if not last:
            mn = mod_next_ref[...]
            xn_out_ref[...] = (_rms(x_new) * g_next_ref[...] * (1.0 + mn[1:2]) + mn[0:1]).astype(BF16)

    def ctx_branch():
        x_mid, xn = mixer_out(hc_ref[...], x_ref[...])
        pos = lax.broadcasted_iota(jnp.int32, (tile, 1), 0) % CTX_LEN
        for f in range(n_chunks):
            sl, sv = cols(f)
            a = _dot(xn, win_ref[:, sl])
            left = jnp.where(pos >= 1, pltpu.roll(a, 1, 0), 0.0)
            right = jnp.where(pos < CTX_LEN - 1, pltpu.roll(a, tile - 1, 0), 0.0)
            conv = cw_ref[3:4, sl] * left + cw_ref[4:5, sl] * a + cw_ref[5:6, sl] * right
            hbuf_ref[:, sl] = act(conv, _dot(xn, win_ref[:, sv]), sl)
        finish(x_mid)

    def lat_branch():
        t = (i - n_ctx) % tiles_per_seq
        n_rows = tile + 2 * GRID_W
        rowpos = lax.broadcasted_iota(jnp.int32, (n_rows, 1), 0)
        first = jnp.where(t > 0, 0, GRID_W)
        stop = jnp.where(t < tiles_per_seq - 1, n_rows, GRID_W + tile)
        x_mid, xe = mixer_out(jnp.concatenate([hup_ref[...], hl_ref[...], hdn_ref[...]], axis=0),
                              jnp.concatenate([xup_ref[...], x_ref[...], xdn_ref[...]], axis=0),
                              (rowpos >= first) & (rowpos < stop))
        x_mid = x_mid[GRID_W:GRID_W + tile]
        xn = xe[GRID_W:GRID_W + tile]
        colpos = rowpos % GRID_W
        for f in range(n_chunks):
            sl, sv = cols(f)
            buf = _dot(xe, win_ref[:, sl])
            left = jnp.where(colpos >= 1, pltpu.roll(buf, 1, 0), 0.0)
            right = jnp.where(colpos < GRID_W - 1, pltpu.roll(buf, n_rows - 1, 0), 0.0)
            conv = None
            for dr in range(3):
                rs = slice(dr * GRID_W, dr * GRID_W + tile)
                term = (cw_ref[3 * dr:3 * dr + 1, sl] * left[rs] + cw_ref[3 * dr + 1:3 * dr + 2, sl] * buf[rs]
                        + cw_ref[3 * dr + 2:3 * dr + 3, sl] * right[rs])
                conv = term if conv is None else conv + term
            hbuf_ref[:, sl] = act(conv, _dot(xn, win_ref[:, sv]), sl)
        finish(x_mid)

    if tile_off + n_tiles <= n_ctx:
        ctx_branch()
    elif tile_off >= n_ctx:
        lat_branch()
    else:
        pl.when(i < n_ctx)(ctx_branch)
        pl.when(i >= n_ctx)(lat_branch)


def _layer_tail(h_ctx, h_lat, w_o, w_in, conv_w, conv_b, w_out, x, mods, g_mix_post, g_ffn_pre, g_post, g_next,
                layer, group, x_dst=None, xn_dst=None):
    last = layer == DEPTH - 1
    tile = TAIL_TILE_CTX if group == 0 else TAIL_TILE_LAT
    tile_off = 0 if group == 0 else N_CTX_TOK // tile
    n_tiles = (N_CTX_TOK if group == 0 else N_TOK - N_CTX_TOK) // tile
    halo_per_tile = tile // GRID_W
    n_halo = N_TOK // GRID_W
    n_ctx = N_CTX_TOK // tile
    n_lat_halo = (N_TOK - N_CTX_TOK) // GRID_W
    row = pl.BlockSpec((tile, D), lambda i: (i + tile_off, 0))
    resident = dict(pipeline_mode=pl.Buffered(1))
    layer_vec = pl.BlockSpec((None, 1, D), lambda i: (layer, 0, 0))

    def lat_halo(i, shift):
        return jnp.clip((i + tile_off - n_ctx + shift) * halo_per_tile - (1 - shift), 0, n_lat_halo - 1)

    h_ctx_spec = (pl.BlockSpec((tile, D), lambda i: (i, 0)) if group == 0
                  else pl.BlockSpec((GRID_W, D), lambda i: (0, 0)))
    h_lat_spec = (pl.BlockSpec((tile, D), lambda i: (i, 0)) if group == 1
                  else pl.BlockSpec((GRID_W, D), lambda i: (0, 0)))
    in_specs = [h_ctx_spec, h_lat_spec,
                pl.BlockSpec((GRID_W, D), lambda i: (lat_halo(i, 0), 0)),
                pl.BlockSpec((GRID_W, D), lambda i: (lat_halo(i, 1), 0)),
                row,
                pl.BlockSpec((GRID_W, D), lambda i: (jnp.maximum((i + tile_off) * halo_per_tile - 1, 0), 0)),
                pl.BlockSpec((GRID_W, D), lambda i: (jnp.minimum((i + tile_off + 1) * halo_per_tile, n_halo - 1), 0)),
                pl.BlockSpec((D, D), lambda i: (0, 0), **resident),
                pl.BlockSpec((None, D, 2 * D_FF), lambda i: (layer, 0, 0), **resident),
                pl.BlockSpec((None, 9, D_FF), lambda i: (layer, 0, 0)),
                pl.BlockSpec((None, 1, D_FF), lambda i: (layer, 0, 0)),
                pl.BlockSpec((None, D_FF, D), lambda i: (layer, 0, 0), **resident),
                pl.BlockSpec((None, None, 6, D), lambda i: (layer, _mod_row(i + tile_off, tile), 0, 0)),
                layer_vec, layer_vec, layer_vec]
    args = [h_ctx, h_lat, h_lat, h_lat, x, x, x, w_o, w_in, conv_w.reshape(DEPTH, 9, D_FF), conv_b, w_out, mods,
            g_mix_post, g_ffn_pre, g_post]
    aliases = {}
    if last:
        out_specs = [pl.BlockSpec((tile, D), lambda i: (i, 0))]
        out_shape = [jax.ShapeDtypeStruct((n_tiles * tile, D), F32)]
    else:
        in_specs += [pl.BlockSpec((None, 1, D), lambda i: (layer + 1, 0, 0)),
                     pl.BlockSpec((None, None, 6, D), lambda i: (layer + 1, _mod_row(i + tile_off, tile), 0, 0))]
        args += [g_next, mods]
        aliases = {len(args): 0, len(args) + 1: 1}
        in_specs += [pl.BlockSpec(memory_space=pl.ANY)] * 2
        args += [x_dst, xn_dst]
        out_specs = [row, row]
        out_shape = [jax.ShapeDtypeStruct((N_TOK, D), F32), jax.ShapeDtypeStruct((N_TOK, D), BF16)]
    return pl.pallas_call(
        functools.partial(_tail_kernel, layer, tile, tile_off, n_tiles, len(aliases)),
        grid=(n_tiles,),
        in_specs=in_specs,
        out_specs=out_specs,
        out_shape=out_shape,
        input_output_aliases=aliases,
        scratch_shapes=[pltpu.VMEM((tile, D_FF), BF16)],
        compiler_params=_cparams("parallel"),
        name="layer_tail",
    )(*args)


def _even_weights(w_gate_cols, w_out):
    src = [0] * (8 * HEADS)
    used = [0.0] * (8 * HEADS)
    for h in range(HEADS):
        for k, col in enumerate((h, 8 + h, 4 + h, 12 + h)):
            src[8 * h + k] = col
            used[8 * h + k] = 1.0
    w_gate = (w_gate_cols[:, jnp.array(src)] * jnp.array(used, F32)).T.astype(BF16)
    w_o = w_out.reshape(2, HEADS, 128, D).transpose(1, 0, 2, 3).reshape(D, D).astype(BF16)
    return w_gate, w_o


def _odd_weights(w_lr_cols, w2, b2):
    w_lr = jnp.tile(w_lr_cols, (1, 128 // (2 * GLA_RANK))).astype(BF16)
    w2h = w2.reshape(2, GLA_RANK, HEADS, DK).transpose(2, 0, 1, 3)
    w32 = jnp.zeros((HEADS, 2 * GLA_RANK, 256), F32)
    w32 = w32.at[:, :GLA_RANK, :DK].set(w2h[:, 0]).at[:, GLA_RANK:, DK:].set(w2h[:, 1])
    hi = w32.astype(BF16)
    mid = (w32 - hi.astype(F32)).astype(BF16)
    low = (w32 - hi.astype(F32) - mid.astype(F32)).astype(BF16)
    none = jnp.zeros_like(hi)
    wz = jnp.concatenate([hi, hi, hi, mid, mid, low, none, none], axis=1)
    bz = b2.reshape(2, HEADS, DK).transpose(1, 0, 2).reshape(HEADS, 1, 2 * DK)
    return w_lr, wz, bz


def kernel(x_prompt, x_sample, c, c_ctx, state_mlstm_C, state_mlstm_n, state_mlstm_m, state_ret, state_gla, ada_w, ada_b, norm_mix_pre, norm_mix_post, norm_ffn_pre, norm_ffn_post, w_in_even, w_out_even, mlstm_igate_b, mlstm_fgate_b, ret_decay_logit, w_in_odd, gla_gate_w2, gla_gate_b, w_out_odd, ffn_w_in, ffn_conv_w, ffn_conv_b, ffn_w_out):
    cond = jnp.concatenate([c_ctx[None, :], c, jnp.zeros((8 - 1 - N_LAT_SEQ, D), F32)], axis=0)
    mods = _mod_table(cond, ada_w, ada_b)
    norm_mix_pre, norm_mix_post, norm_ffn_pre, norm_ffn_post = (
        g.reshape(DEPTH, 1, D) for g in (norm_mix_pre, norm_mix_post, norm_ffn_pre, norm_ffn_post))
    ffn_conv_b = ffn_conv_b.reshape(DEPTH, 1, D_FF)
    ffn_w_in_bf, ffn_w_out_bf = ffn_w_in.astype(BF16), ffn_w_out.astype(BF16)
    w_even_bf = jnp.concatenate([w_in_even[:, :, :2048], w_in_even[:, :, 2064:]], axis=2).astype(BF16)
    w_odd_bf = w_in_odd.astype(BF16)
    gate_cols = w_in_even[:, :, 2048:2064]
    lr_cols = w_in_odd[:, :, 3072:3072 + 2 * GLA_RANK]

    x, xn = _prenorm(x_prompt.reshape(N_CTX_TOK, D), x_sample.reshape(N_LAT_SEQ * LAT_LEN, D),
                     norm_mix_pre, mods, 0)
    x_spare = jnp.zeros((N_TOK, D), F32)
    even_finals, gla_finals = None, None
    for layer in range(DEPTH):
        idx = layer // 2
        if layer % 2 == 0:
            w_gate, w_o = _even_weights(gate_cols[idx], w_out_even[idx])
            proj, gates = _proj(xn, w_even_bf, idx, 4096, w_gate, True, "proj_even")
            params = jnp.concatenate([mlstm_igate_b[idx].reshape(-1), mlstm_fgate_b[idx].reshape(-1),
                                      ret_decay_logit[idx].reshape(-1)])
            h_ctx, *even_finals = _even_mixer(proj, gates, params, 0, None, idx, even_finals)
            init = (state_mlstm_C, state_mlstm_n, state_mlstm_m, state_ret)
            (h_lat,) = _even_mixer(proj, gates, params, 1, init, idx)
        else:
            w_lr, wz, bz = _odd_weights(lr_cols[idx], gla_gate_w2[idx], gla_gate_b[idx])
            w_o = w_out_odd[idx].astype(BF16)
            proj, lowrank = _proj(xn, w_odd_bf, idx, 3072, w_lr, False, "proj_odd")
            h_ctx, gla_finals = _gla_mixer(proj, lowrank, wz, bz, 0, None, idx, gla_finals)
            (h_lat,) = _gla_mixer(proj, lowrank, wz, bz, 1, state_gla, idx)
        tail = functools.partial(_layer_tail, h_ctx, h_lat, w_o, ffn_w_in_bf, ffn_conv_w, ffn_conv_b, ffn_w_out_bf,
                                 x, mods, norm_mix_post, norm_ffn_pre, norm_ffn_post, norm_mix_pre, layer)
        if layer < DEPTH - 1:
            x_new, xn_new = tail(0, x_spare, xn)
            x_new, xn = tail(1, x_new, xn_new)
            x, x_spare = x_new, x
        else:
            (y_ctx,) = tail(0)
            (y_lat,) = tail(1)

    y_prompt = y_ctx.reshape(N_CTX_SEQ, CTX_LEN, D)
    y_sample = y_lat.reshape(N_LAT_SEQ, LAT_LEN, D)
    new_c, new_n, new_m, new_r = even_finals
    return (y_prompt, y_sample, new_c, new_n[..., 0, :], new_m[:, :, :, 0:2, 0].transpose(0, 1, 3, 2), new_r, gla_finals)
```

```python
import functools
import math

import jax
import jax.numpy as jnp
from jax import lax
from jax.experimental import pallas as pl
from jax.experimental.pallas import tpu as pltpu

F32 = jnp.float32
BF16 = jnp.bfloat16

D = 1024
DEPTH = 4
N_CTX_SEQ, CTX_LEN = 16, 256
N_LAT_SEQ, LAT_LEN = 2, 2048
GRID_W = 64
N_CTX_TOK = N_CTX_SEQ * CTX_LEN
N_TOK = N_CTX_TOK + N_LAT_SEQ * LAT_LEN
HEADS = 4
DK = 128
DV_GLA = 256
GLA_RANK = 16
GLA_TAU = 16.0
D_FF = 2816
EPS = 1e-6
NEG = -1e30

ROW_TILE = 1024
PROJ_ROW_TILE = 1024
PROJ_COL_TILE = 1024
EVEN_CHUNK = 256
EVEN_UNROLL = 4
EVEN_CTX_SEQS = 8
GLA_CHUNK = 64
GLA_UNROLL = 8
GLA_CTX_SEQS = 8
GLA_OUT_UNROLL = 16
FF_CHUNK = 256
TAIL_TILE_CTX = 512
TAIL_TILE_LAT = 512
VMEM_LIMIT = 56 * 1024 * 1024


def _cparams(*sem):
    return pltpu.CompilerParams(dimension_semantics=sem, vmem_limit_bytes=VMEM_LIMIT)


def _mod_row(i, tile):
    n_ctx = N_CTX_TOK // tile
    return jnp.where(i < n_ctx, 0, 1 + (i - n_ctx) // (LAT_LEN // tile))


def _rms(x):
    return x * lax.rsqrt(jnp.mean(x * x, axis=-1, keepdims=True) + EPS)


def _log_sigmoid(x):
    return jnp.minimum(x, 0.0) - jnp.log(1.0 + jnp.exp(-jnp.abs(x)))


def _dot(a, b):
    return jnp.dot(a, b, preferred_element_type=F32)


def _dot_nt(a, b):
    return lax.dot_general(a, b, (((1,), (1,)), ((), ())), preferred_element_type=F32)


def _dot_tn(a, b):
    return lax.dot_general(a, b, (((0,), (0,)), ((), ())), preferred_element_type=F32)


def _seg_scan(x, seg, pos, op, ident, reverse):
    n = x.shape[0]
    s = 1
    while s < seg:
        if reverse:
            x = op(x, jnp.where(pos < seg - s, pltpu.roll(x, n - s, 0), ident))
        else:
            x = op(x, jnp.where(pos >= s, pltpu.roll(x, s, 0), ident))
        s *= 2
    return x


def _mod_kernel(c_ref, w_ref, b_ref, o_ref):
    c = c_ref[...]
    s = (c * jax.nn.sigmoid(c)).astype(BF16)
    o_ref[0] = _dot(s, w_ref[0].astype(BF16)) + b_ref[0]


def _mod_table(cond, ada_w, ada_b):
    tn = 2048
    out = pl.pallas_call(
        _mod_kernel,
        grid=(DEPTH, 6 * D // tn),
        in_specs=[pl.BlockSpec((8, D), lambda l, j: (0, 0)),
                  pl.BlockSpec((1, D, tn), lambda l, j: (l, 0, j)),
                  pl.BlockSpec((1, 1, tn), lambda l, j: (l, 0, j))],
        out_specs=pl.BlockSpec((1, 8, tn), lambda l, j: (l, 0, j)),
        out_shape=jax.ShapeDtypeStruct((DEPTH, 8, 6 * D), F32),
        compiler_params=_cparams("parallel", "parallel"),
        name="mod_table",
    )(cond, ada_w, ada_b.reshape(DEPTH, 1, 6 * D))
    return out.reshape(DEPTH, 8, 6, D)


def _prenorm_kernel(xp_ref, xs_ref, g_ref, mod_ref, x_ref, xn_ref):
    i = pl.program_id(0)

    def run(src_ref):
        x = src_ref[...]
        m = mod_ref[...]
        x_ref[...] = x
        xn_ref[...] = (_rms(x) * g_ref[...] * (1.0 + m[1:2]) + m[0:1]).astype(BF16)

    pl.when(i < N_CTX_TOK // ROW_TILE)(lambda: run(xp_ref))
    pl.when(i >= N_CTX_TOK // ROW_TILE)(lambda: run(xs_ref))


def _prenorm(xp, xs, g, mods, layer):
    nc = N_CTX_TOK // ROW_TILE
    return pl.pallas_call(
        _prenorm_kernel,
        grid=(N_TOK // ROW_TILE,),
        in_specs=[pl.BlockSpec((ROW_TILE, D), lambda i: (jnp.minimum(i, nc - 1), 0)),
                  pl.BlockSpec((ROW_TILE, D), lambda i: (jnp.maximum(i - nc, 0), 0)),
                  pl.BlockSpec((None, 1, D), lambda i: (layer, 0, 0)),
                  pl.BlockSpec((None, None, 6, D), lambda i: (layer, _mod_row(i, ROW_TILE), 0, 0))],
        out_specs=[pl.BlockSpec((ROW_TILE, D), lambda i: (i, 0)),
                   pl.BlockSpec((ROW_TILE, D), lambda i: (i, 0))],
        out_shape=[jax.ShapeDtypeStruct((N_TOK, D), F32), jax.ShapeDtypeStruct((N_TOK, D), BF16)],
        compiler_params=_cparams("parallel"),
        name="prenorm",
    )(xp, xs, g, mods)


def _proj_kernel(small_transposed, head_blocks, *refs):
    if head_blocks is None:
        x_ref, w_ref, ws_ref, o_ref, os_ref = refs
    else:
        x_ref, w_ref, wt_ref, ws_ref, o_ref, os_ref = refs
    x = x_ref[...]

    def emit(ref):
        o_ref[...] = _dot(x, ref[...].astype(BF16)).astype(o_ref.dtype)

    if head_blocks is None:
        emit(w_ref)
    else:
        pl.when(pl.program_id(1) < head_blocks)(lambda: emit(w_ref))
        pl.when(pl.program_id(1) >= head_blocks)(lambda: emit(wt_ref))

    @pl.when(pl.program_id(1) == 0)
    def _():
        os_ref[...] = _dot_nt(ws_ref[...], x) if small_transposed else _dot(x, ws_ref[...])


def _proj(xn, w_all, layer_idx, n, w_small, small_transposed, name, w_tail=None):
    tm, tn = PROJ_ROW_TILE, PROJ_COL_TILE
    m, k = xn.shape
    head_blocks = None if w_tail is None else (n - w_tail.shape[2]) // tn
    if w_tail is None:
        w_specs = [pl.BlockSpec((None, k, tn), lambda i, j: (layer_idx, 0, j))]
        w_args = [w_all]
    else:
        w_specs = [pl.BlockSpec((None, k, tn), lambda i, j: (layer_idx, 0, jnp.minimum(j, head_blocks - 1))),
                   pl.BlockSpec((None, k, tn), lambda i, j: (layer_idx, 0, jnp.maximum(j - head_blocks, 0)))]
        w_args = [w_all, w_tail]
    if small_transposed:
        r = w_small.shape[0]
        small_spec = pl.BlockSpec((r, tm), lambda i, j: (0, i))
        small_shape = jax.ShapeDtypeStruct((r, m), F32)
    else:
        small_spec = pl.BlockSpec((tm, w_small.shape[1]), lambda i, j: (i, 0))
        small_shape = jax.ShapeDtypeStruct((m, w_small.shape[1]), F32)
    return pl.pallas_call(
        functools.partial(_proj_kernel, small_transposed, head_blocks),
        grid=(m // tm, n // tn),
        in_specs=[pl.BlockSpec((tm, k), lambda i, j: (i, 0)), *w_specs,
                  pl.BlockSpec(w_small.shape, lambda i, j: (0, 0))],
        out_specs=[pl.BlockSpec((tm, tn), lambda i, j: (i, j)), small_spec],
        out_shape=[jax.ShapeDtypeStruct((m, n), BF16), small_shape],
        compiler_params=_cparams("parallel", "arbitrary"),
        name=name,
    )(xn, *w_args, w_small)


def _even_kernel(seq_len, n_seqs, init_layer, want_final, n_carried, out_layer, *refs):
    has_init = init_layer is not None
    layer_idx, n_layers = init_layer if has_init else (0, 1)
    chunk = EVEN_CHUNK
    n_chunks = seq_len // chunk
    total = n_seqs * n_chunks
    use_cross = has_init or n_chunks > 1
    unroll = min(EVEN_UNROLL, n_chunks)
    assert not has_init or n_seqs == 1
    it = iter(refs)
    par_ref = next(it)
    m0_ref = next(it) if has_init else None
    mq_ref, mk_ref, mv_ref, mo_ref, rq_ref, rk_ref, rv_ref, rg_ref, gate_ref = (next(it) for _ in range(9))
    if has_init:
        c0_ref, n0_ref, r0_ref = next(it), next(it), next(it)
    for _ in range(n_carried):
        next(it)
    out_ref = next(it)
    if want_final:
        finals = [next(it) for _ in range(4)]
        if n_carried == 0:
            for ref in finals:
                ref[...] = jnp.zeros(ref.shape, F32)
            finals = [ref.at[:, out_layer] for ref in finals]
        c_out_ref, n_out_ref, m_out_ref, r_out_ref = finals
    grow_ref, rep_ref, edge_ref, cn_ref, s_ref = (next(it) for _ in range(5))
    if use_cross:
        cns_ref, ss_ref, ms_ref = next(it), next(it), next(it)

    b = pl.program_id(0)
    h = pl.program_id(1)
    ii = lax.broadcasted_iota(jnp.int32, (chunk, chunk), 0)
    jj = lax.broadcasted_iota(jnp.int32, (chunk, chunk), 1)
    pos_col = lax.broadcasted_iota(jnp.int32, (chunk, 1), 0).astype(F32)
    pos_int = lax.broadcasted_iota(jnp.int32, (chunk, 1), 0)
    ones_blk = jnp.ones((chunk, 128), BF16)
    ones_sum = jnp.ones((256, 128), BF16)

    sub = lax.broadcasted_iota(jnp.int32, (8, 1), 0)
    bias = jnp.where(sub == 0, par_ref[h], jnp.where(sub == 1, par_ref[4 + h],
                     jnp.where(sub == 2, par_ref[8 + h], jnp.where(sub == 3, par_ref[12 + h], 0.0))))
    lane = lax.broadcasted_iota(jnp.int32, (1, 128), 1)
    tri_pre = (ii <= jj).astype(BF16)
    tri_suf = (ii >= jj).astype(BF16)
    def lanes(c):
        if isinstance(c, int):
            return slice(c * chunk, (c + 1) * chunk)
        return pl.ds(pl.multiple_of(c * chunk, chunk), chunk)

    def rows(c):
        if isinstance(c, int):
            return pl.ds(c * chunk, chunk)
        return pl.ds(pl.multiple_of(c * chunk, chunk), chunk)

    def prepare_chunk(c):
        pre = gate_ref[:, lanes(c)] + bias
        log_f = pltpu.roll(_log_sigmoid(pre), 6, 0)
        hi = log_f.astype(BF16)
        rest = log_f - hi.astype(F32)
        mid = rest.astype(BF16)
        low = (rest - mid.astype(F32)).astype(BF16)
        parts = jnp.concatenate([hi, mid, low, jnp.zeros_like(hi)], axis=0)
        cum_pre = _dot(parts, tri_pre)
        cum_suf = _dot(parts, tri_suf)
        fcum = jnp.where(sub == 0, cum_pre[0:8] + cum_pre[8:16] + cum_pre[16:24],
                         cum_suf[0:8] + cum_suf[8:16] + cum_suf[16:24])
        gsc = pre - fcum
        packed = jnp.where(sub < 2, gsc, pltpu.roll(fcum, 2, 0))
        grow_ref[:, lanes(c)] = packed
        pcol = packed.T
        g_rep = [jnp.broadcast_to(pcol[:, k:k + 1], (chunk, 128)) for k in range(4)]
        g_rep.append(_seg_scan(g_rep[0], chunk, pos_int, jnp.maximum, NEG, False))
        g_rep.append(_seg_scan(g_rep[1], chunk, pos_int, jnp.maximum, NEG, True))
        rep_ref[rows(c), :] = jnp.concatenate(g_rep, axis=1)
        f_tot = jnp.sum(log_f, axis=1, keepdims=True)
        g_top = jnp.max(gsc, axis=1, keepdims=True)
        edge_ref[c] = jnp.where(lane == 0, f_tot, jnp.where(lane == 1, g_top, 0.0))

    def for_all_chunks(body):
        if total <= 2 * EVEN_UNROLL:
            for c in range(total):
                body(c)
        else:
            def group(g, carry):
                for u in range(EVEN_UNROLL):
                    body(g * EVEN_UNROLL + u)
                return carry
            lax.fori_loop(0, total // EVEN_UNROLL, group, 0)

    for_all_chunks(prepare_chunk)

    k_scale = DK ** -0.5
    lg_f = _log_sigmoid(jnp.full((1, 1), par_ref[16 + h], F32))
    lg_b = _log_sigmoid(jnp.full((1, 1), par_ref[20 + h], F32))
    diff = (ii - jj).astype(F32)
    ret_mask = k_scale * (jnp.where(diff >= 0, jnp.exp(lg_f * jnp.maximum(diff, 0.0)), 0.0)
                          + jnp.where(diff <= 0, jnp.exp(lg_b * jnp.maximum(-diff, 0.0)), 0.0))
    rq_dec = (jnp.exp(lg_f * (pos_col + 1.0)), jnp.exp(lg_b * (chunk - pos_col)))
    rk_dec = (k_scale * jnp.exp(lg_f * (chunk - 1.0 - pos_col)), k_scale * jnp.exp(lg_b * pos_col))
    r_chunk = (jnp.exp(lg_f * chunk), jnp.exp(lg_b * chunk))

    def load_kv(c):
        return mk_ref[rows(c), :], jnp.concatenate([mv_ref[rows(c), :], ones_blk], axis=1)

    def ret_kv(c):
        return rk_ref[rows(c), :], rv_ref[rows(c), :]

    def advance(c, d, m, slot):
        if use_cross:
            cns_ref[c, :, 256 * d:256 * d + 256] = cn_ref[slot].astype(BF16)
            ss_ref[c, 128 * d:128 * d + 128, :] = s_ref[slot].astype(BF16)
            ms_ref[c, d:d + 1, :] = jnp.broadcast_to(m, (1, 128))
        edge = edge_ref[c]
        mu = jnp.maximum(m, edge[d:d + 1, 1:2])
        k, vext = load_kv(c)
        kw = (k.astype(F32) * (k_scale * jnp.exp(rep_ref[rows(c), 128 * d:128 * d + 128] - mu))).astype(BF16)
        cn_ref[slot] = jnp.exp(m - mu) * cn_ref[slot] + _dot_tn(kw, vext)
        rk, rv = ret_kv(c)
        s_ref[slot] = r_chunk[d] * s_ref[slot] + _dot_tn((rk.astype(F32) * rk_dec[d]).astype(BF16), rv)
        return edge[d:d + 1, 0:1] + mu

    if has_init:
        base = (b * n_layers + layer_idx) * 8
        m_init = (jnp.full((1, 1), m0_ref[base + h], F32), jnp.full((1, 1), m0_ref[base + 4 + h], F32))
        for d in range(2):
            cn_ref[d] = jnp.concatenate([c0_ref[d], jnp.broadcast_to(n0_ref[d], (DK, 128))], axis=1)
            s_ref[d] = r0_ref[d]
    else:
        m_init = (jnp.zeros((1, 1), F32), jnp.zeros((1, 1), F32))
        cn_ref[...] = jnp.zeros((2 * n_seqs, DK, 256), F32)
        s_ref[...] = jnp.zeros((2 * n_seqs, DK, DK), F32)

    m_fin = []
    for seq in range(n_seqs):
        def state_group(g, carry, seq=seq):
            m_f, m_b = carry
            for u in range(unroll):
                c = g * unroll + u
                m_f = advance(seq * n_chunks + c, 0, m_f, 2 * seq)
                m_b = advance(seq * n_chunks + n_chunks - 1 - c, 1, m_b, 2 * seq + 1)
            return m_f, m_b

        if n_chunks // unroll > 1:
            m_fin.append(lax.fori_loop(0, n_chunks // unroll, state_group, m_init))
        elif use_cross or want_final:
            m_fin.append(state_group(0, m_init))

    def rms_rep(x):
        sq = x * x
        hi = sq.astype(BF16)
        low = (sq - hi.astype(F32)).astype(BF16)
        ssum = _dot(jnp.concatenate([hi, low], axis=1), ones_sum)
        return x * lax.rsqrt(ssum * (1.0 / 128) + EPS)

    def output_chunk(c):
        q = mq_ref[rows(c), :]
        k, vext = load_kv(c)
        scores = _dot_nt(q, k)
        cols = rep_ref[rows(c), :]
        f_f, f_b, gm_f, gm_b = cols[:, 256:384], cols[:, 384:512], cols[:, 512:640], cols[:, 640:768]
        row = grow_ref[:, lanes(c)] + math.log(k_scale)
        if use_cross:
            m_f = ms_ref[c, 0:1, :][:, 0:1]
            m_b = ms_ref[c, 1:2, :][:, 0:1]
        else:
            m_f, m_b = m_init
        mu_f = jnp.maximum(m_f, gm_f)
        mu_b = jnp.maximum(m_b, gm_b)
        p_f = jnp.exp(jnp.where(jj <= ii, row[0:1, :], NEG) - jnp.concatenate([mu_f, mu_f], axis=1)) * scores
        p_b = jnp.exp(jnp.where(jj >= ii, row[1:2, :], NEG) - jnp.concatenate([mu_b, mu_b], axis=1)) * scores
        tot = _dot(jnp.concatenate([p_f, p_b], axis=0).astype(BF16), vext)
        num_f, den_f, num_b, den_b = tot[:chunk, :128], tot[:chunk, 128:], tot[chunk:, :128], tot[chunk:, 128:]
        if use_cross:
            cross = _dot(q, cns_ref[c])
            w_f = jnp.exp(m_f - mu_f)
            w_b = jnp.exp(m_b - mu_b)
            num_f, den_f = num_f + w_f * cross[:, 0:128], den_f + w_f * cross[:, 128:256]
            num_b, den_b = num_b + w_b * cross[:, 256:384], den_b + w_b * cross[:, 384:512]
        h_f = num_f / jnp.maximum(jnp.abs(den_f), jnp.exp(-(f_f + mu_f)))
        h_b = num_b / jnp.maximum(jnp.abs(den_b), jnp.exp(-(f_b + mu_b)))
        out_m = rms_rep(h_f + h_b) * jax.nn.sigmoid(mo_ref[rows(c), :].astype(F32))
        rq = rq_ref[rows(c), :]
        rk, rv = ret_kv(c)
        o_r = _dot((_dot_nt(rq, rk) * ret_mask).astype(BF16), rv)
        if use_cross:
            rq32 = rq.astype(F32)
            q_in = jnp.concatenate([rq32 * rq_dec[0], rq32 * rq_dec[1]], axis=1).astype(BF16)
            o_r = o_r + _dot(q_in, ss_ref[c])
        rg = rg_ref[rows(c), :].astype(F32)
        out_r = rms_rep(o_r) * (rg * jax.nn.sigmoid(rg))
        out_ref[rows(c), :] = jnp.concatenate([out_m, out_r], axis=1).astype(BF16)

    for_all_chunks(output_chunk)

    if want_final:
        for seq in range(n_seqs):
            for d in range(2):
                cn = cn_ref[2 * seq + d]
                c_out_ref[seq, d] = cn[:, :DK]
                n_out_ref[seq, d] = cn[:, DK:].T[0:8, :]
                r_out_ref[seq, d] = s_ref[2 * seq + d]
            m_out_ref[seq] = jnp.concatenate([jnp.broadcast_to(m_fin[seq][0], (1, 128)),
                                              jnp.broadcast_to(m_fin[seq][1], (1, 128)), jnp.zeros((6, 128), F32)], axis=0)


def _even_mixer(proj, gates_t, params, group, init, layer_idx=0, carried=None):
    ctx = group == 0
    n_seq, seq_len = (N_CTX_SEQ, CTX_LEN) if ctx else (N_LAT_SEQ, LAT_LEN)
    row_off = 0 if ctx else N_CTX_TOK // LAT_LEN
    has_init, want_final = not ctx, ctx
    n_seqs = EVEN_CTX_SEQS if ctx else 1
    rows = n_seqs * seq_len
    n_chunks = seq_len // EVEN_CHUNK
    total = n_seqs * n_chunks
    use_cross = has_init or n_chunks > 1

    def col(k):
        return pl.BlockSpec((rows, 128), lambda b, h: (row_off + b, 4 * k + h))

    smem = pl.BlockSpec(memory_space=pltpu.SMEM)
    in_specs = [smem]
    args = [params]
    if has_init:
        c0, n0, m0, r0 = init
        n_layers = c0.shape[1]
        in_specs.append(smem)
        args.append(m0.reshape(-1))
    in_specs += [col(k) for k in range(8)]
    in_specs.append(pl.BlockSpec((8, rows), lambda b, h: (h, row_off + b)))
    args += [proj] * 8 + [gates_t]
    if has_init:
        in_specs += [pl.BlockSpec((None, None, 2, None, DK, 128), lambda b, h: (b, layer_idx, 0, h, 0, 0)),
                     pl.BlockSpec((None, None, 2, None, DK, 1), lambda b, h: (b, layer_idx, 0, h, 0, 0)),
                     pl.BlockSpec((None, None, 2, None, DK, 128), lambda b, h: (b, layer_idx, 0, h, 0, 0))]
        args += [c0, n0.reshape(n0.shape + (1,)), r0]
    out_specs = [pl.BlockSpec((rows, 256), lambda b, h: (b, h))]
    out_shape = [jax.ShapeDtypeStruct((n_seq * seq_len, D), BF16)]
    aliases = {}
    if want_final:
        n_even = (DEPTH + 1) // 2
        if carried is None:
            lay, at = n_even, 0
        else:
            lay, at = None, layer_idx
        out_specs += [pl.BlockSpec((n_seqs, lay, 2, None, DK, DK), lambda b, h: (b, at, 0, h, 0, 0)),
                      pl.BlockSpec((n_seqs, lay, 2, None, 8, DK), lambda b, h: (b, at, 0, h, 0, 0)),
                      pl.BlockSpec((n_seqs, lay, None, 8, 128), lambda b, h: (b, at, h, 0, 0)),
                      pl.BlockSpec((n_seqs, lay, 2, None, DK, DK), lambda b, h: (b, at, 0, h, 0, 0))]
        out_shape += [jax.ShapeDtypeStruct((n_seq, n_even, 2, HEADS, DK, DK), F32),
                      jax.ShapeDtypeStruct((n_seq, n_even, 2, HEADS, 8, DK), F32),
                      jax.ShapeDtypeStruct((n_seq, n_even, HEADS, 8, 128), F32),
                      jax.ShapeDtypeStruct((n_seq, n_even, 2, HEADS, DK, DK), F32)]
        if carried is not None:
            aliases = {len(args) + k: 1 + k for k in range(len(carried))}
            in_specs += [pl.BlockSpec(memory_space=pl.ANY)] * len(carried)
            args += list(carried)
    scratch = [pltpu.VMEM((8, rows), F32), pltpu.VMEM((rows, 768), F32), pltpu.VMEM((total, 8, 128), F32),
               pltpu.VMEM((2 * n_seqs, DK, 256), F32), pltpu.VMEM((2 * n_seqs, DK, DK), F32)]
    if use_cross:
        scratch += [pltpu.VMEM((total, DK, 512), BF16), pltpu.VMEM((total, 2 * DK, DK), BF16),
                    pltpu.VMEM((total, 8, 128), F32)]
    return pl.pallas_call(
        functools.partial(_even_kernel, seq_len, n_seqs, (layer_idx, n_layers) if has_init else None, want_final,
                          len(aliases), layer_idx),
        grid=(n_seq // n_seqs, HEADS),
        in_specs=in_specs,
        out_specs=out_specs,
        out_shape=out_shape,
        input_output_aliases=aliases,
        scratch_shapes=scratch,
        compiler_params=_cparams("parallel", "parallel"),
        name="even_mixer_ctx" if ctx else "even_mixer_lat",
    )(*args)


def _gla_kernel(seq_len, n_seqs, has_init, want_final, n_carried, out_layer, *refs):
    chunk = GLA_CHUNK
    n_chunks = seq_len // chunk
    total = n_seqs * n_chunks
    n_rows = n_seqs * seq_len
    assert not has_init or n_seqs == 1
    it = iter(refs)
    q_ref, k_ref, v_ref, gr_ref, lr_ref, wz_ref, bz_ref = (next(it) for _ in range(7))
    s0_ref = next(it) if has_init else None
    for _ in range(n_carried):
        next(it)
    out_ref = next(it)
    s_out_ref = next(it) if want_final else None
    if want_final and n_carried == 0:
        s_out_ref[...] = jnp.zeros(s_out_ref.shape, F32)
        s_out_ref = s_out_ref.at[:, out_layer]
    qe_ref, ke_ref, cum_ref, sts_ref, st_ref = (next(it) for _ in range(5))
    unroll = min(GLA_UNROLL, n_chunks)

    x4 = lr_ref[...]
    hi = x4.astype(BF16).astype(F32)
    mid = (x4 - hi).astype(BF16).astype(F32)
    low = ((x4 - hi) - mid).astype(BF16).astype(F32)
    grp = lax.broadcasted_iota(jnp.int32, (1, 128), 1) // (2 * GLA_RANK)
    lhs = jnp.concatenate([jnp.where(grp == 1, mid, jnp.where(grp == 2, low, hi)),
                           jnp.where(grp == 0, mid, jnp.where(grp == 1, hi, 0.0))], axis=1).astype(BF16)
    z = _dot(lhs, wz_ref[...]) + bz_ref[...]
    la = _log_sigmoid(z) / GLA_TAU
    pos = lax.broadcasted_iota(jnp.int32, (n_rows, 1), 0) % chunk
    b_f = _seg_scan(la[:, :128], chunk, pos, jnp.add, 0.0, False)
    b_b = _seg_scan(la[:, 128:], chunk, pos, jnp.add, 0.0, True)
    q = q_ref[...].astype(F32) * (DK ** -0.5)
    k = k_ref[...].astype(F32)
    qe_ref[...] = jnp.concatenate([q * jnp.exp(b_f), q * jnp.exp(b_b)], axis=1).astype(BF16)
    ke_ref[...] = jnp.concatenate([k * jnp.exp(-b_f), k * jnp.exp(-b_b)], axis=1).astype(BF16)
    cum_ref[...] = jnp.concatenate([b_f, b_b], axis=1)

    ii = lax.broadcasted_iota(jnp.int32, (chunk, chunk), 0)
    jj = lax.broadcasted_iota(jnp.int32, (chunk, chunk), 1)

    def rows(c):
        if isinstance(c, int):
            return pl.ds(c * chunk, chunk)
        return pl.ds(pl.multiple_of(c * chunk, chunk), chunk)

    def advance(c, d, slot):
        lanes = slice(128 * d, 128 * d + 128)
        edge = c * chunk + (chunk - 1 if d == 0 else 0)
        decay = cum_ref[pl.ds(edge, 1), :][:, lanes]
        kw = (k_ref[rows(c), :].astype(F32) * jnp.exp(decay - cum_ref[rows(c), :][:, lanes])).astype(BF16)
        st = st_ref[slot]
        sts_ref[c, :, lanes] = st.astype(BF16)
        st_ref[slot] = st * jnp.exp(decay) + _dot_tn(v_ref[rows(c), :], kw)

    if has_init:
        st_ref[0] = s0_ref[0].T
        st_ref[1] = s0_ref[1].T
    else:
        st_ref[...] = jnp.zeros((2 * n_seqs, DV_GLA, DK), F32)

    for seq in range(n_seqs):
        def state_group(g, carry, seq=seq):
            for u in range(unroll):
                c = g * unroll + u
                advance(seq * n_chunks + c, 0, 2 * seq)
                advance(seq * n_chunks + n_chunks - 1 - c, 1, 2 * seq + 1)
            return carry

        if n_chunks // unroll == 1:
            state_group(0, 0)
        else:
            lax.fori_loop(0, n_chunks // unroll, state_group, 0)

    out_unroll = min(GLA_OUT_UNROLL, total)

    def output_group(g, carry):
        for u in range(out_unroll):
            c = g * out_unroll + u
            qe = qe_ref[rows(c), :]
            ke = ke_ref[rows(c), :]
            att = (jnp.where(jj <= ii, _dot_nt(qe[:, :128], ke[:, :128]), 0.0)
                   + jnp.where(jj >= ii, _dot_nt(qe[:, 128:], ke[:, 128:]), 0.0))
            o = _dot(att.astype(BF16), v_ref[rows(c), :]) + _dot_nt(qe, sts_ref[c])
            gr = gr_ref[rows(c), :].astype(F32)
            out_ref[rows(c), :] = (_rms(o) * (gr * jax.nn.sigmoid(gr))).astype(BF16)
        return carry

    if total // out_unroll == 1:
        output_group(0, 0)
    else:
        lax.fori_loop(0, total // out_unroll, output_group, 0)
    if want_final:
        for seq in range(n_seqs):
            s_out_ref[seq, 0] = st_ref[2 * seq].T
            s_out_ref[seq, 1] = st_ref[2 * seq + 1].T


def _gla_mixer(proj, lowrank, wz, bz, group, init, layer_idx=0, carried=None):
    ctx = group == 0
    n_seq, seq_len = (N_CTX_SEQ, CTX_LEN) if ctx else (N_LAT_SEQ, LAT_LEN)
    row_off = 0 if ctx else N_CTX_TOK // LAT_LEN
    has_init, want_final = not ctx, ctx
    n_seqs = GLA_CTX_SEQS if ctx else 1
    n_rows = n_seqs * seq_len
    n_chunks = seq_len // GLA_CHUNK
    total = n_seqs * n_chunks
    in_specs = [pl.BlockSpec((n_rows, 128), lambda b, h: (row_off + b, h)),
                pl.BlockSpec((n_rows, 128), lambda b, h: (row_off + b, 4 + h)),
                pl.BlockSpec((n_rows, 256), lambda b, h: (row_off + b, 4 + h)),
                pl.BlockSpec((n_rows, 256), lambda b, h: (row_off + b, 8 + h)),
                pl.BlockSpec((n_rows, 128), lambda b, h: (row_off + b, 0)),
                pl.BlockSpec((None, 256, 256), lambda b, h: (h, 0, 0)),
                pl.BlockSpec((None, 1, 256), lambda b, h: (h, 0, 0))]
    args = [proj, proj, proj, proj, lowrank, wz, bz]
    if has_init:
        in_specs.append(pl.BlockSpec((None, None, 2, None, DK, DV_GLA), lambda b, h: (b, layer_idx, 0, h, 0, 0)))
        args.append(init)
    out_specs = [pl.BlockSpec((n_rows, 256), lambda b, h: (b, h))]
    out_shape = [jax.ShapeDtypeStruct((n_seq * seq_len, D), BF16)]
    aliases = {}
    if want_final:
        lay, at = (DEPTH // 2, 0) if carried is None else (None, layer_idx)
        out_specs.append(pl.BlockSpec((n_seqs, lay, 2, None, DK, DV_GLA), lambda b, h: (b, at, 0, h, 0, 0)))
        out_shape.append(jax.ShapeDtypeStruct((n_seq, DEPTH // 2, 2, HEADS, DK, DV_GLA), F32))
        if carried is not None:
            aliases = {len(args): 1}
            in_specs.append(pl.BlockSpec(memory_space=pl.ANY))
            args.append(carried)
    scratch = [pltpu.VMEM((n_rows, 256), BF16)] * 2 + [pltpu.VMEM((n_rows, 256), F32),
                                                       pltpu.VMEM((total, DV_GLA, 2 * DK), BF16),
                                                       pltpu.VMEM((2 * n_seqs, DV_GLA, DK), F32)]
    return pl.pallas_call(
        functools.partial(_gla_kernel, seq_len, n_seqs, has_init, want_final, len(aliases), layer_idx),
        grid=(n_seq // n_seqs, HEADS),
        in_specs=in_specs,
        out_specs=out_specs,
        out_shape=out_shape,
        input_output_aliases=aliases,
        scratch_shapes=scratch,
        compiler_params=_cparams("parallel", "parallel"),
        name="gla_mixer_ctx" if ctx else "gla_mixer_lat",
    )(*args)


def _tail_kernel(layer, tile, tile_off, n_tiles, n_carried, *refs):
    last = layer == DEPTH - 1
    it = iter(refs)
    hc_ref, hl_ref, hup_ref, hdn_ref, x_ref, xup_ref, xdn_ref = (next(it) for _ in range(7))
    wo_ref, win_ref, cw_ref, cb_ref, w_ref, mod_ref, g_mix_ref, g_pre_ref, g_post_ref = (next(it) for _ in range(9))
    if not last:
        g_next_ref, mod_next_ref = next(it), next(it)
    for _ in range(n_carried):
        next(it)
    x_out_ref = next(it)
    xn_out_ref = None if last else next(it)
    hbuf_ref = next(it)
    i = pl.program_id(0) + tile_off
    n_ctx = N_CTX_TOK // tile
    tiles_per_seq = LAT_LEN // tile
    n_chunks = D_FF // FF_CHUNK

    def cols(f):
        return slice(f * FF_CHUNK, (f + 1) * FF_CHUNK), slice(D_FF + f * FF_CHUNK, D_FF + (f + 1) * FF_CHUNK)

    def act(conv, v, sl):
        return (jax.nn.gelu(conv + cb_ref[:, sl]) * v).astype(BF16)

    def mixer_out(h, x, keep=None):
        m = mod_ref[...]
        x_mid = x + m[2:3] * (_rms(_dot(h, wo_ref[...])) * g_mix_ref[...])
        xn = _rms(x_mid) * g_pre_ref[...] * (1.0 + m[4:5]) + m[3:4]
        if keep is not None:
            xn = jnp.where(keep, xn, 0.0)
        return x_mid, xn.astype(BF16)

    def finish(x_mid):
        x_new = x_mid + mod_ref[5:6, :] * (_rms(_dot(hbuf_ref[...], w_ref[...])) * g_post_ref[...])
        x_out_ref[...] = x_new
        if not last:
            mn = mod_next_ref[...]
            xn_out_ref[...] = (_rms(x_new) * g_next_ref[...] * (1.0 + mn[1:2]) + mn[0:1]).astype(BF16)

    def ctx_branch():
        x_mid, xn = mixer_out(hc_ref[...], x_ref[...])
        pos = lax.broadcasted_iota(jnp.int32, (tile, 1), 0) % CTX_LEN
        for f in range(n_chunks):
            sl, sv = cols(f)
            a = _dot(xn, win_ref[:, sl])
            left = jnp.where(pos >= 1, pltpu.roll(a, 1, 0), 0.0)
            right = jnp.where(pos < CTX_LEN - 1, pltpu.roll(a, tile - 1, 0), 0.0)
            conv = cw_ref[3:4, sl] * left + cw_ref[4:5, sl] * a + cw_ref[5:6, sl] * right
            hbuf_ref[:, sl] = act(conv, _dot(xn, win_ref[:, sv]), sl)
        finish(x_mid)

    def lat_branch():
        t = (i - n_ctx) % tiles_per_seq
        n_rows = tile + 2 * GRID_W
        rowpos = lax.broadcasted_iota(jnp.int32, (n_rows, 1), 0)
        first = jnp.where(t > 0, 0, GRID_W)
        stop = jnp.where(t < tiles_per_seq - 1, n_rows, GRID_W + tile)
        x_mid, xe = mixer_out(jnp.concatenate([hup_ref[...], hl_ref[...], hdn_ref[...]], axis=0),
                              jnp.concatenate([xup_ref[...], x_ref[...], xdn_ref[...]], axis=0),
                              (rowpos >= first) & (rowpos < stop))
        x_mid = x_mid[GRID_W:GRID_W + tile]
        xn = xe[GRID_W:GRID_W + tile]
        colpos = rowpos % GRID_W
        for f in range(n_chunks):
            sl, sv = cols(f)
            buf = _dot(xe, win_ref[:, sl])
            left = jnp.where(colpos >= 1, pltpu.roll(buf, 1, 0), 0.0)
            right = jnp.where(colpos < GRID_W - 1, pltpu.roll(buf, n_rows - 1, 0), 0.0)
            conv = None
            for dr in range(3):
                rs = slice(dr * GRID_W, dr * GRID_W + tile)
                term = (cw_ref[3 * dr:3 * dr + 1, sl] * left[rs] + cw_ref[3 * dr + 1:3 * dr + 2, sl] * buf[rs]
                        + cw_ref[3 * dr + 2:3 * dr + 3, sl] * right[rs])
                conv = term if conv is None else conv + term
            hbuf_ref[:, sl] = act(conv, _dot(xn, win_ref[:, sv]), sl)
        finish(x_mid)

    if tile_off + n_tiles <= n_ctx:
        ctx_branch()
    elif tile_off >= n_ctx:
        lat_branch()
    else:
        pl.when(i < n_ctx)(ctx_branch)
        pl.when(i >= n_ctx)(lat_branch)


def _layer_tail(h_ctx, h_lat, w_o, w_in, conv_w, conv_b, w_out, x, mods, g_mix_post, g_ffn_pre, g_post, g_next,
                layer, group, x_dst=None, xn_dst=None):
    last = layer == DEPTH - 1
    tile = TAIL_TILE_CTX if group == 0 else TAIL_TILE_LAT
    tile_off = 0 if group == 0 else N_CTX_TOK // tile
    n_tiles = (N_CTX_TOK if group == 0 else N_TOK - N_CTX_TOK) // tile
    halo_per_tile = tile // GRID_W
    n_halo = N_TOK // GRID_W
    n_ctx = N_CTX_TOK // tile
    n_lat_halo = (N_TOK - N_CTX_TOK) // GRID_W
    row = pl.BlockSpec((tile, D), lambda i: (i + tile_off, 0))
    resident = dict(pipeline_mode=pl.Buffered(1))
    layer_vec = pl.BlockSpec((None, 1, D), lambda i: (layer, 0, 0))

    def lat_halo(i, shift):
        return jnp.clip((i + tile_off - n_ctx + shift) * halo_per_tile - (1 - shift), 0, n_lat_halo - 1)

    h_ctx_spec = (pl.BlockSpec((tile, D), lambda i: (i, 0)) if group == 0
                  else pl.BlockSpec((GRID_W, D), lambda i: (0, 0)))
    h_lat_spec = (pl.BlockSpec((tile, D), lambda i: (i, 0)) if group == 1
                  else pl.BlockSpec((GRID_W, D), lambda i: (0, 0)))
    in_specs = [h_ctx_spec, h_lat_spec,
                pl.BlockSpec((GRID_W, D), lambda i: (lat_halo(i, 0), 0)),
                pl.BlockSpec((GRID_W, D), lambda i: (lat_halo(i, 1), 0)),
                row,
                pl.BlockSpec((GRID_W, D), lambda i: (jnp.maximum((i + tile_off) * halo_per_tile - 1, 0), 0)),
                pl.BlockSpec((GRID_W, D), lambda i: (jnp.minimum((i + tile_off + 1) * halo_per_tile, n_halo - 1), 0)),
                pl.BlockSpec((D, D), lambda i: (0, 0), **resident),
                pl.BlockSpec((None, D, 2 * D_FF), lambda i: (layer, 0, 0), **resident),
                pl.BlockSpec((None, 9, D_FF), lambda i: (layer, 0, 0)),
                pl.BlockSpec((None, 1, D_FF), lambda i: (layer, 0, 0)),
                pl.BlockSpec((None, D_FF, D), lambda i: (layer, 0, 0), **resident),
                pl.BlockSpec((None, None, 6, D), lambda i: (layer, _mod_row(i + tile_off, tile), 0, 0)),
                layer_vec, layer_vec, layer_vec]
    args = [h_ctx, h_lat, h_lat, h_lat, x, x, x, w_o, w_in, conv_w.reshape(DEPTH, 9, D_FF), conv_b, w_out, mods,
            g_mix_post, g_ffn_pre, g_post]
    aliases = {}
    if last:
        out_specs = [pl.BlockSpec((tile, D), lambda i: (i, 0))]
        out_shape = [jax.ShapeDtypeStruct((n_tiles * tile, D), F32)]
    else:
        in_specs += [pl.BlockSpec((None, 1, D), lambda i: (layer + 1, 0, 0)),
                     pl.BlockSpec((None, None, 6, D), lambda i: (layer + 1, _mod_row(i + tile_off, tile), 0, 0))]
        args += [g_next, mods]
        aliases = {len(args): 0, len(args) + 1: 1}
        in_specs += [pl.BlockSpec(memory_space=pl.ANY)] * 2
        args += [x_dst, xn_dst]
        out_specs = [row, row]
        out_shape = [jax.ShapeDtypeStruct((N_TOK, D), F32), jax.ShapeDtypeStruct((N_TOK, D), BF16)]
    return pl.pallas_call(
        functools.partial(_tail_kernel, layer, tile, tile_off, n_tiles, len(aliases)),
        grid=(n_tiles,),
        in_specs=in_specs,
        out_specs=out_specs,
        out_shape=out_shape,
        input_output_aliases=aliases,
        scratch_shapes=[pltpu.VMEM((tile, D_FF), BF16)],
        compiler_params=_cparams("parallel"),
        name="layer_tail",
    )(*args)


def _even_weights(w_gate_cols, w_out):
    src = [0] * (8 * HEADS)
    used = [0.0] * (8 * HEADS)
    for h in range(HEADS):
        for k, col in enumerate((h, 8 + h, 4 + h, 12 + h)):
            src[8 * h + k] = col
            used[8 * h + k] = 1.0
    w_gate = (w_gate_cols[:, jnp.array(src)] * jnp.array(used, F32)).T.astype(BF16)
    w_o = w_out.reshape(2, HEADS, 128, D).transpose(1, 0, 2, 3).reshape(D, D).astype(BF16)
    return w_gate, w_o


def _odd_weights(w_lr_cols, w2, b2):
    w_lr = jnp.tile(w_lr_cols, (1, 128 // (2 * GLA_RANK))).astype(BF16)
    w2h = w2.reshape(2, GLA_RANK, HEADS, DK).transpose(2, 0, 1, 3)
    w32 = jnp.zeros((HEADS, 2 * GLA_RANK, 256), F32)
    w32 = w32.at[:, :GLA_RANK, :DK].set(w2h[:, 0]).at[:, GLA_RANK:, DK:].set(w2h[:, 1])
    hi = w32.astype(BF16)
    mid = (w32 - hi.astype(F32)).astype(BF16)
    low = (w32 - hi.astype(F32) - mid.astype(F32)).astype(BF16)
    none = jnp.zeros_like(hi)
    wz = jnp.concatenate([hi, hi, hi, mid, mid, low, none, none], axis=1)
    bz = b2.reshape(2, HEADS, DK).transpose(1, 0, 2).reshape(HEADS, 1, 2 * DK)
    return w_lr, wz, bz


def kernel(x_prompt, x_sample, c, c_ctx, state_mlstm_C, state_mlstm_n, state_mlstm_m, state_ret, state_gla, ada_w, ada_b, norm_mix_pre, norm_mix_post, norm_ffn_pre, norm_ffn_post, w_in_even, w_out_even, mlstm_igate_b, mlstm_fgate_b, ret_decay_logit, w_in_odd, gla_gate_w2, gla_gate_b, w_out_odd, ffn_w_in, ffn_conv_w, ffn_conv_b, ffn_w_out):
    cond = jnp.concatenate([c_ctx[None, :], c, jnp.zeros((8 - 1 - N_LAT_SEQ, D), F32)], axis=0)
    mods = _mod_table(cond, ada_w, ada_b)
    norm_mix_pre, norm_mix_post, norm_ffn_pre, norm_ffn_post = (
        g.reshape(DEPTH, 1, D) for g in (norm_mix_pre, norm_mix_post, norm_ffn_pre, norm_ffn_post))
    ffn_conv_b = ffn_conv_b.reshape(DEPTH, 1, D_FF)
    ffn_w_in_bf, ffn_w_out_bf = ffn_w_in.astype(BF16), ffn_w_out.astype(BF16)
    w_even_tail = w_in_even[:, :, 2064:]
    gate_cols = w_in_even[:, :, 2048:2064]
    lr_cols = w_in_odd[:, :, 3072:3072 + 2 * GLA_RANK]

    x, xn = _prenorm(x_prompt.reshape(N_CTX_TOK, D), x_sample.reshape(N_LAT_SEQ * LAT_LEN, D),
                     norm_mix_pre, mods, 0)
    x_spare = jnp.zeros((N_TOK, D), F32)
    even_finals, gla_finals = None, None
    for layer in range(DEPTH):
        idx = layer // 2
        if layer % 2 == 0:
            w_gate, w_o = _even_weights(gate_cols[idx], w_out_even[idx])
            proj, gates = _proj(xn, w_in_even, idx, 4096, w_gate, True, "proj_even", w_even_tail)
            params = jnp.concatenate([mlstm_igate_b[idx].reshape(-1), mlstm_fgate_b[idx].reshape(-1),
                                      ret_decay_logit[idx].reshape(-1)])
            h_ctx, *even_finals = _even_mixer(proj, gates, params, 0, None, idx, even_finals)
            init = (state_mlstm_C, state_mlstm_n, state_mlstm_m, state_ret)
            (h_lat,) = _even_mixer(proj, gates, params, 1, init, idx)
        else:
            w_lr, wz, bz = _odd_weights(lr_cols[idx], gla_gate_w2[idx], gla_gate_b[idx])
            w_o = w_out_odd[idx].astype(BF16)
            proj, lowrank = _proj(xn, w_in_odd, idx, 3072, w_lr, False, "proj_odd")
            h_ctx, gla_finals = _gla_mixer(proj, lowrank, wz, bz, 0, None, idx, gla_finals)
            (h_lat,) = _gla_mixer(proj, lowrank, wz, bz, 1, state_gla, idx)
        tail = functools.partial(_layer_tail, h_ctx, h_lat, w_o, ffn_w_in_bf, ffn_conv_w, ffn_conv_b, ffn_w_out_bf,
                                 x, mods, norm_mix_post, norm_ffn_pre, norm_ffn_post, norm_mix_pre, layer)
        if layer < DEPTH - 1:
            x_new, xn_new = tail(0, x_spare, xn)
            x_new, xn = tail(1, x_new, xn_new)
            x, x_spare = x_new, x
        else:
            (y_ctx,) = tail(0)
            (y_lat,) = tail(1)

    y_prompt = y_ctx.reshape(N_CTX_SEQ, CTX_LEN, D)
    y_sample = y_lat.reshape(N_LAT_SEQ, LAT_LEN, D)
    new_c, new_n, new_m, new_r = even_finals
    return (y_prompt, y_sample, new_c, new_n[..., 0, :], new_m[:, :, :, 0:2, 0].transpose(0, 1, 3, 2), new_r, gla_finals)
```

```python
import functools
import math

import jax
import jax.numpy as jnp
from jax import lax
from jax.experimental import pallas as pl
from jax.experimental.pallas import tpu as pltpu

F32 = jnp.float32
BF16 = jnp.bfloat16

D = 1024
DEPTH = 4
N_CTX_SEQ, CTX_LEN = 16, 256
N_LAT_SEQ, LAT_LEN = 2, 2048
GRID_W = 64
N_CTX_TOK = N_CTX_SEQ * CTX_LEN
N_TOK = N_CTX_TOK + N_LAT_SEQ * LAT_LEN
HEADS = 4
DK = 128
DV_GLA = 256
GLA_RANK = 16
GLA_TAU = 16.0
D_FF = 2816
EPS = 1e-6
NEG = -1e30

ROW_TILE = 1024
PROJ_ROW_TILE = 1024
EVEN_CHUNK = 256
EVEN_UNROLL = 4
EVEN_CTX_SEQS = 8
GLA_CHUNK = 64
GLA_UNROLL = 8
GLA_CTX_SEQS = 8
GLA_OUT_UNROLL = 16
FF_CHUNK = 256
TAIL_TILE_CTX = 512
TAIL_TILE_LAT = 512
VMEM_LIMIT = 56 * 1024 * 1024


def _cparams(*sem):
    return pltpu.CompilerParams(dimension_semantics=sem, vmem_limit_bytes=VMEM_LIMIT)


def _mod_row(i, tile):
    n_ctx = N_CTX_TOK // tile
    return jnp.where(i < n_ctx, 0, 1 + (i - n_ctx) // (LAT_LEN // tile))


def _rms(x):
    return x * lax.rsqrt(jnp.mean(x * x, axis=-1, keepdims=True) + EPS)


def _log_sigmoid(x):
    return jnp.minimum(x, 0.0) - jnp.log(1.0 + jnp.exp(-jnp.abs(x)))


def _dot(a, b):
    return jnp.dot(a, b, preferred_element_type=F32)


def _dot_nt(a, b):
    return lax.dot_general(a, b, (((1,), (1,)), ((), ())), preferred_element_type=F32)


def _dot_tn(a, b):
    return lax.dot_general(a, b, (((0,), (0,)), ((), ())), preferred_element_type=F32)


def _seg_scan(x, seg, pos, op, ident, reverse):
    n = x.shape[0]
    s = 1
    while s < seg:
        if reverse:
            x = op(x, jnp.where(pos < seg - s, pltpu.roll(x, n - s, 0), ident))
        else:
            x = op(x, jnp.where(pos >= s, pltpu.roll(x, s, 0), ident))
        s *= 2
    return x


def _mod_kernel(c_ref, w_ref, b_ref, o_ref):
    c = c_ref[...]
    s = (c * jax.nn.sigmoid(c)).astype(BF16)
    o_ref[0] = _dot(s, w_ref[0].astype(BF16)) + b_ref[0]


def _mod_table(cond, ada_w, ada_b):
    tn = 2048
    out = pl.pallas_call(
        _mod_kernel,
        grid=(DEPTH, 6 * D // tn),
        in_specs=[pl.BlockSpec((8, D), lambda l, j: (0, 0)),
                  pl.BlockSpec((1, D, tn), lambda l, j: (l, 0, j)),
                  pl.BlockSpec((1, 1, tn), lambda l, j: (l, 0, j))],
        out_specs=pl.BlockSpec((1, 8, tn), lambda l, j: (l, 0, j)),
        out_shape=jax.ShapeDtypeStruct((DEPTH, 8, 6 * D), F32),
        compiler_params=_cparams("parallel", "parallel"),
        name="mod_table",
    )(cond, ada_w, ada_b.reshape(DEPTH, 1, 6 * D))
    return out.reshape(DEPTH, 8, 6, D)


def _prenorm_kernel(xp_ref, xs_ref, g_ref, mod_ref, x_ref, xn_ref):
    i = pl.program_id(0)

    def run(src_ref):
        x = src_ref[...]
        m = mod_ref[...]
        x_ref[...] = x
        xn_ref[...] = (_rms(x) * g_ref[...] * (1.0 + m[1:2]) + m[0:1]).astype(BF16)

    pl.when(i < N_CTX_TOK // ROW_TILE)(lambda: run(xp_ref))
    pl.when(i >= N_CTX_TOK // ROW_TILE)(lambda: run(xs_ref))


def _prenorm(xp, xs, g, mods, layer):
    nc = N_CTX_TOK // ROW_TILE
    return pl.pallas_call(
        _prenorm_kernel,
        grid=(N_TOK // ROW_TILE,),
        in_specs=[pl.BlockSpec((ROW_TILE, D), lambda i: (jnp.minimum(i, nc - 1), 0)),
                  pl.BlockSpec((ROW_TILE, D), lambda i: (jnp.maximum(i - nc, 0), 0)),
                  pl.BlockSpec((None, 1, D), lambda i: (layer, 0, 0)),
                  pl.BlockSpec((None, None, 6, D), lambda i: (layer, _mod_row(i, ROW_TILE), 0, 0))],
        out_specs=[pl.BlockSpec((ROW_TILE, D), lambda i: (i, 0)),
                   pl.BlockSpec((ROW_TILE, D), lambda i: (i, 0))],
        out_shape=[jax.ShapeDtypeStruct((N_TOK, D), F32), jax.ShapeDtypeStruct((N_TOK, D), BF16)],
        compiler_params=_cparams("parallel"),
        name="prenorm",
    )(xp, xs, g, mods)


def _proj_kernel(small_transposed, x_ref, w_ref, ws_ref, o_ref, os_ref):
    x = x_ref[...]
    o_ref[...] = _dot(x, w_ref[...]).astype(o_ref.dtype)

    @pl.when(pl.program_id(1) == 0)
    def _():
        os_ref[...] = _dot_nt(ws_ref[...], x) if small_transposed else _dot(x, ws_ref[...])


def _proj(xn, w_all, layer_idx, n, w_small, small_transposed, name):
    tm = PROJ_ROW_TILE
    tn = n // 2
    m, k = xn.shape
    if small_transposed:
        r = w_small.shape[0]
        small_spec = pl.BlockSpec((r, tm), lambda i, j: (0, i))
        small_shape = jax.ShapeDtypeStruct((r, m), F32)
    else:
        small_spec = pl.BlockSpec((tm, w_small.shape[1]), lambda i, j: (i, 0))
        small_shape = jax.ShapeDtypeStruct((m, w_small.shape[1]), F32)
    return pl.pallas_call(
        functools.partial(_proj_kernel, small_transposed),
        grid=(m // tm, n // tn),
        in_specs=[pl.BlockSpec((tm, k), lambda i, j: (i, 0)),
                  pl.BlockSpec((None, k, tn), lambda i, j: (layer_idx, 0, j)),
                  pl.BlockSpec(w_small.shape, lambda i, j: (0, 0))],
        out_specs=[pl.BlockSpec((tm, tn), lambda i, j: (i, j)), small_spec],
        out_shape=[jax.ShapeDtypeStruct((m, n), BF16), small_shape],
        compiler_params=_cparams("parallel", "arbitrary"),
        name=name,
    )(xn, w_all, w_small)


def _even_kernel(seq_len, n_seqs, init_layer, want_final, n_carried, out_layer, *refs):
    has_init = init_layer is not None
    layer_idx, n_layers = init_layer if has_init else (0, 1)
    chunk = EVEN_CHUNK
    n_chunks = seq_len // chunk
    total = n_seqs * n_chunks
    use_cross = has_init or n_chunks > 1
    unroll = min(EVEN_UNROLL, n_chunks)
    assert not has_init or n_seqs == 1
    it = iter(refs)
    par_ref = next(it)
    m0_ref = next(it) if has_init else None
    mq_ref, mk_ref, mv_ref, mo_ref, rq_ref, rk_ref, rv_ref, rg_ref, gate_ref = (next(it) for _ in range(9))
    if has_init:
        c0_ref, n0_ref, r0_ref = next(it), next(it), next(it)
    for _ in range(n_carried):
        next(it)
    out_ref = next(it)
    if want_final:
        finals = [next(it) for _ in range(4)]
        if n_carried == 0:
            for ref in finals:
                ref[...] = jnp.zeros(ref.shape, F32)
            finals = [ref.at[:, out_layer] for ref in finals]
        c_out_ref, n_out_ref, m_out_ref, r_out_ref = finals
    grow_ref, rep_ref, edge_ref, cn_ref, s_ref = (next(it) for _ in range(5))
    if use_cross:
        cns_ref, ss_ref, ms_ref = next(it), next(it), next(it)

    b = pl.program_id(0)
    h = pl.program_id(1)
    ii = lax.broadcasted_iota(jnp.int32, (chunk, chunk), 0)
    jj = lax.broadcasted_iota(jnp.int32, (chunk, chunk), 1)
    pos_col = lax.broadcasted_iota(jnp.int32, (chunk, 1), 0).astype(F32)
    pos_int = lax.broadcasted_iota(jnp.int32, (chunk, 1), 0)
    ones_blk = jnp.ones((chunk, 128), BF16)
    ones_sum = jnp.ones((256, 128), BF16)

    sub = lax.broadcasted_iota(jnp.int32, (8, 1), 0)
    bias = jnp.where(sub == 0, par_ref[h], jnp.where(sub == 1, par_ref[4 + h],
                     jnp.where(sub == 2, par_ref[8 + h], jnp.where(sub == 3, par_ref[12 + h], 0.0))))
    lane = lax.broadcasted_iota(jnp.int32, (1, 128), 1)
    tri_pre = (ii <= jj).astype(BF16)
    tri_suf = (ii >= jj).astype(BF16)
    def lanes(c):
        if isinstance(c, int):
            return slice(c * chunk, (c + 1) * chunk)
        return pl.ds(pl.multiple_of(c * chunk, chunk), chunk)

    def rows(c):
        if isinstance(c, int):
            return pl.ds(c * chunk, chunk)
        return pl.ds(pl.multiple_of(c * chunk, chunk), chunk)

    def prepare_chunk(c):
        pre = gate_ref[:, lanes(c)] + bias
        log_f = pltpu.roll(_log_sigmoid(pre), 6, 0)
        hi = log_f.astype(BF16)
        rest = log_f - hi.astype(F32)
        mid = rest.astype(BF16)
        low = (rest - mid.astype(F32)).astype(BF16)
        parts = jnp.concatenate([hi, mid, low, jnp.zeros_like(hi)], axis=0)
        cum_pre = _dot(parts, tri_pre)
        cum_suf = _dot(parts, tri_suf)
        fcum = jnp.where(sub == 0, cum_pre[0:8] + cum_pre[8:16] + cum_pre[16:24],
                         cum_suf[0:8] + cum_suf[8:16] + cum_suf[16:24])
        gsc = pre - fcum
        packed = jnp.where(sub < 2, gsc, pltpu.roll(fcum, 2, 0))
        grow_ref[:, lanes(c)] = packed
        pcol = packed.T
        g_rep = [jnp.broadcast_to(pcol[:, k:k + 1], (chunk, 128)) for k in range(4)]
        g_rep.append(_seg_scan(g_rep[0], chunk, pos_int, jnp.maximum, NEG, False))
        g_rep.append(_seg_scan(g_rep[1], chunk, pos_int, jnp.maximum, NEG, True))
        rep_ref[rows(c), :] = jnp.concatenate(g_rep, axis=1)
        f_tot = jnp.sum(log_f, axis=1, keepdims=True)
        g_top = jnp.max(gsc, axis=1, keepdims=True)
        edge_ref[c] = jnp.where(lane == 0, f_tot, jnp.where(lane == 1, g_top, 0.0))

    def for_all_chunks(body):
        if total <= 2 * EVEN_UNROLL:
            for c in range(total):
                body(c)
        else:
            def group(g, carry):
                for u in range(EVEN_UNROLL):
                    body(g * EVEN_UNROLL + u)
                return carry
            lax.fori_loop(0, total // EVEN_UNROLL, group, 0)

    for_all_chunks(prepare_chunk)

    k_scale = DK ** -0.5
    lg_f = _log_sigmoid(jnp.full((1, 1), par_ref[16 + h], F32))
    lg_b = _log_sigmoid(jnp.full((1, 1), par_ref[20 + h], F32))
    diff = (ii - jj).astype(F32)
    ret_mask = k_scale * (jnp.where(diff >= 0, jnp.exp(lg_f * jnp.maximum(diff, 0.0)), 0.0)
                          + jnp.where(diff <= 0, jnp.exp(lg_b * jnp.maximum(-diff, 0.0)), 0.0))
    rq_dec = (jnp.exp(lg_f * (pos_col + 1.0)), jnp.exp(lg_b * (chunk - pos_col)))
    rk_dec = (k_scale * jnp.exp(lg_f * (chunk - 1.0 - pos_col)), k_scale * jnp.exp(lg_b * pos_col))
    r_chunk = (jnp.exp(lg_f * chunk), jnp.exp(lg_b * chunk))

    def load_kv(c):
        return mk_ref[rows(c), :], jnp.concatenate([mv_ref[rows(c), :], ones_blk], axis=1)

    def ret_kv(c):
        return rk_ref[rows(c), :], rv_ref[rows(c), :]

    def advance(c, d, m, slot):
        if use_cross:
            cns_ref[c, :, 256 * d:256 * d + 256] = cn_ref[slot].astype(BF16)
            ss_ref[c, 128 * d:128 * d + 128, :] = s_ref[slot].astype(BF16)
            ms_ref[c, d:d + 1, :] = jnp.broadcast_to(m, (1, 128))
        edge = edge_ref[c]
        mu = jnp.maximum(m, edge[d:d + 1, 1:2])
        k, vext = load_kv(c)
        kw = (k.astype(F32) * (k_scale * jnp.exp(rep_ref[rows(c), 128 * d:128 * d + 128] - mu))).astype(BF16)
        cn_ref[slot] = jnp.exp(m - mu) * cn_ref[slot] + _dot_tn(kw, vext)
        rk, rv = ret_kv(c)
        s_ref[slot] = r_chunk[d] * s_ref[slot] + _dot_tn((rk.astype(F32) * rk_dec[d]).astype(BF16), rv)
        return edge[d:d + 1, 0:1] + mu

    if has_init:
        base = (b * n_layers + layer_idx) * 8
        m_init = (jnp.full((1, 1), m0_ref[base + h], F32), jnp.full((1, 1), m0_ref[base + 4 + h], F32))
        for d in range(2):
            cn_ref[d] = jnp.concatenate([c0_ref[d], jnp.broadcast_to(n0_ref[d], (DK, 128))], axis=1)
            s_ref[d] = r0_ref[d]
    else:
        m_init = (jnp.zeros((1, 1), F32), jnp.zeros((1, 1), F32))
        cn_ref[...] = jnp.zeros((2 * n_seqs, DK, 256), F32)
        s_ref[...] = jnp.zeros((2 * n_seqs, DK, DK), F32)

    m_fin = []
    for seq in range(n_seqs):
        def state_group(g, carry, seq=seq):
            m_f, m_b = carry
            for u in range(unroll):
                c = g * unroll + u
                m_f = advance(seq * n_chunks + c, 0, m_f, 2 * seq)
                m_b = advance(seq * n_chunks + n_chunks - 1 - c, 1, m_b, 2 * seq + 1)
            return m_f, m_b

        if n_chunks // unroll > 1:
            m_fin.append(lax.fori_loop(0, n_chunks // unroll, state_group, m_init))
        elif use_cross or want_final:
            m_fin.append(state_group(0, m_init))

    def rms_rep(x):
        sq = x * x
        hi = sq.astype(BF16)
        low = (sq - hi.astype(F32)).astype(BF16)
        ssum = _dot(jnp.concatenate([hi, low], axis=1), ones_sum)
        return x * lax.rsqrt(ssum * (1.0 / 128) + EPS)

    def output_chunk(c):
        q = mq_ref[rows(c), :]
        k, vext = load_kv(c)
        scores = _dot_nt(q, k)
        cols = rep_ref[rows(c), :]
        f_f, f_b, gm_f, gm_b = cols[:, 256:384], cols[:, 384:512], cols[:, 512:640], cols[:, 640:768]
        row = grow_ref[:, lanes(c)] + math.log(k_scale)
        if use_cross:
            m_f = ms_ref[c, 0:1, :][:, 0:1]
            m_b = ms_ref[c, 1:2, :][:, 0:1]
        else:
            m_f, m_b = m_init
        mu_f = jnp.maximum(m_f, gm_f)
        mu_b = jnp.maximum(m_b, gm_b)
        p_f = jnp.exp(jnp.where(jj <= ii, row[0:1, :], NEG) - jnp.concatenate([mu_f, mu_f], axis=1)) * scores
        p_b = jnp.exp(jnp.where(jj >= ii, row[1:2, :], NEG) - jnp.concatenate([mu_b, mu_b], axis=1)) * scores
        tot = _dot(jnp.concatenate([p_f, p_b], axis=0).astype(BF16), vext)
        num_f, den_f, num_b, den_b = tot[:chunk, :128], tot[:chunk, 128:], tot[chunk:, :128], tot[chunk:, 128:]
        if use_cross:
            cross = _dot(q, cns_ref[c])
            w_f = jnp.exp(m_f - mu_f)
            w_b = jnp.exp(m_b - mu_b)
            num_f, den_f = num_f + w_f * cross[:, 0:128], den_f + w_f * cross[:, 128:256]
            num_b, den_b = num_b + w_b * cross[:, 256:384], den_b + w_b * cross[:, 384:512]
        h_f = num_f / jnp.maximum(jnp.abs(den_f), jnp.exp(-(f_f + mu_f)))
        h_b = num_b / jnp.maximum(jnp.abs(den_b), jnp.exp(-(f_b + mu_b)))
        out_m = rms_rep(h_f + h_b) * jax.nn.sigmoid(mo_ref[rows(c), :].astype(F32))
        rq = rq_ref[rows(c), :]
        rk, rv = ret_kv(c)
        o_r = _dot((_dot_nt(rq, rk) * ret_mask).astype(BF16), rv)
        if use_cross:
            rq32 = rq.astype(F32)
            q_in = jnp.concatenate([rq32 * rq_dec[0], rq32 * rq_dec[1]], axis=1).astype(BF16)
            o_r = o_r + _dot(q_in, ss_ref[c])
        rg = rg_ref[rows(c), :].astype(F32)
        out_r = rms_rep(o_r) * (rg * jax.nn.sigmoid(rg))
        out_ref[rows(c), :] = jnp.concatenate([out_m, out_r], axis=1).astype(BF16)

    for_all_chunks(output_chunk)

    if want_final:
        for seq in range(n_seqs):
            for d in range(2):
                cn = cn_ref[2 * seq + d]
                c_out_ref[seq, d] = cn[:, :DK]
                n_out_ref[seq, d] = cn[:, DK:].T[0:8, :]
                r_out_ref[seq, d] = s_ref[2 * seq + d]
            m_out_ref[seq] = jnp.concatenate([jnp.broadcast_to(m_fin[seq][0], (1, 128)),
                                              jnp.broadcast_to(m_fin[seq][1], (1, 128)), jnp.zeros((6, 128), F32)], axis=0)


def _even_mixer(proj, gates_t, params, group, init, layer_idx=0, carried=None):
    ctx = group == 0
    n_seq, seq_len = (N_CTX_SEQ, CTX_LEN) if ctx else (N_LAT_SEQ, LAT_LEN)
    row_off = 0 if ctx else N_CTX_TOK // LAT_LEN
    has_init, want_final = not ctx, ctx
    n_seqs = EVEN_CTX_SEQS if ctx else 1
    rows = n_seqs * seq_len
    n_chunks = seq_len // EVEN_CHUNK
    total = n_seqs * n_chunks
    use_cross = has_init or n_chunks > 1

    def col(k):
        return pl.BlockSpec((rows, 128), lambda b, h: (row_off + b, 4 * k + h))

    smem = pl.BlockSpec(memory_space=pltpu.SMEM)
    in_specs = [smem]
    args = [params]
    if has_init:
        c0, n0, m0, r0 = init
        n_layers = c0.shape[1]
        in_specs.append(smem)
        args.append(m0.reshape(-1))
    in_specs += [col(k) for k in range(8)]
    in_specs.append(pl.BlockSpec((8, rows), lambda b, h: (h, row_off + b)))
    args += [proj] * 8 + [gates_t]
    if has_init:
        in_specs += [pl.BlockSpec((None, None, 2, None, DK, 128), lambda b, h: (b, layer_idx, 0, h, 0, 0)),
                     pl.BlockSpec((None, None, 2, None, DK, 1), lambda b, h: (b, layer_idx, 0, h, 0, 0)),
                     pl.BlockSpec((None, None, 2, None, DK, 128), lambda b, h: (b, layer_idx, 0, h, 0, 0))]
        args += [c0, n0.reshape(n0.shape + (1,)), r0]
    out_specs = [pl.BlockSpec((rows, 256), lambda b, h: (b, h))]
    out_shape = [jax.ShapeDtypeStruct((n_seq * seq_len, D), BF16)]
    aliases = {}
    if want_final:
        n_even = (DEPTH + 1) // 2
        if carried is None:
            lay, at = n_even, 0
        else:
            lay, at = None, layer_idx
        out_specs += [pl.BlockSpec((n_seqs, lay, 2, None, DK, DK), lambda b, h: (b, at, 0, h, 0, 0)),
                      pl.BlockSpec((n_seqs, lay, 2, None, 8, DK), lambda b, h: (b, at, 0, h, 0, 0)),
                      pl.BlockSpec((n_seqs, lay, None, 8, 128), lambda b, h: (b, at, h, 0, 0)),
                      pl.BlockSpec((n_seqs, lay, 2, None, DK, DK), lambda b, h: (b, at, 0, h, 0, 0))]
        out_shape += [jax.ShapeDtypeStruct((n_seq, n_even, 2, HEADS, DK, DK), F32),
                      jax.ShapeDtypeStruct((n_seq, n_even, 2, HEADS, 8, DK), F32),
                      jax.ShapeDtypeStruct((n_seq, n_even, HEADS, 8, 128), F32),
                      jax.ShapeDtypeStruct((n_seq, n_even, 2, HEADS, DK, DK), F32)]
        if carried is not None:
            aliases = {len(args) + k: 1 + k for k in range(len(carried))}
            in_specs += [pl.BlockSpec(memory_space=pl.ANY)] * len(carried)
            args += list(carried)
    scratch = [pltpu.VMEM((8, rows), F32), pltpu.VMEM((rows, 768), F32), pltpu.VMEM((total, 8, 128), F32),
               pltpu.VMEM((2 * n_seqs, DK, 256), F32), pltpu.VMEM((2 * n_seqs, DK, DK), F32)]
    if use_cross:
        scratch += [pltpu.VMEM((total, DK, 512), BF16), pltpu.VMEM((total, 2 * DK, DK), BF16),
                    pltpu.VMEM((total, 8, 128), F32)]
    return pl.pallas_call(
        functools.partial(_even_kernel, seq_len, n_seqs, (layer_idx, n_layers) if has_init else None, want_final,
                          len(aliases), layer_idx),
        grid=(n_seq // n_seqs, HEADS),
        in_specs=in_specs,
        out_specs=out_specs,
        out_shape=out_shape,
        input_output_aliases=aliases,
        scratch_shapes=scratch,
        compiler_params=_cparams("parallel", "parallel"),
        name="even_mixer_ctx" if ctx else "even_mixer_lat",
    )(*args)


def _gla_kernel(seq_len, n_seqs, has_init, want_final, n_carried, out_layer, *refs):
    chunk = GLA_CHUNK
    n_chunks = seq_len // chunk
    total = n_seqs * n_chunks
    n_rows = n_seqs * seq_len
    assert not has_init or n_seqs == 1
    it = iter(refs)
    q_ref, k_ref, v_ref, gr_ref, lr_ref, wz_ref, bz_ref = (next(it) for _ in range(7))
    s0_ref = next(it) if has_init else None
    for _ in range(n_carried):
        next(it)
    out_ref = next(it)
    s_out_ref = next(it) if want_final else None
    if want_final and n_carried == 0:
        s_out_ref[...] = jnp.zeros(s_out_ref.shape, F32)
        s_out_ref = s_out_ref.at[:, out_layer]
    qe_ref, ke_ref, cum_ref, sts_ref, st_ref = (next(it) for _ in range(5))
    unroll = min(GLA_UNROLL, n_chunks)

    x4 = lr_ref[...]
    hi = x4.astype(BF16).astype(F32)
    mid = (x4 - hi).astype(BF16).astype(F32)
    low = ((x4 - hi) - mid).astype(BF16).astype(F32)
    grp = lax.broadcasted_iota(jnp.int32, (1, 128), 1) // (2 * GLA_RANK)
    lhs = jnp.concatenate([jnp.where(grp == 1, mid, jnp.where(grp == 2, low, hi)),
                           jnp.where(grp == 0, mid, jnp.where(grp == 1, hi, 0.0))], axis=1).astype(BF16)
    z = _dot(lhs, wz_ref[...]) + bz_ref[...]
    la = _log_sigmoid(z) / GLA_TAU
    pos = lax.broadcasted_iota(jnp.int32, (n_rows, 1), 0) % chunk
    b_f = _seg_scan(la[:, :128], chunk, pos, jnp.add, 0.0, False)
    b_b = _seg_scan(la[:, 128:], chunk, pos, jnp.add, 0.0, True)
    q = q_ref[...].astype(F32) * (DK ** -0.5)
    k = k_ref[...].astype(F32)
    qe_ref[...] = jnp.concatenate([q * jnp.exp(b_f), q * jnp.exp(b_b)], axis=1).astype(BF16)
    ke_ref[...] = jnp.concatenate([k * jnp.exp(-b_f), k * jnp.exp(-b_b)], axis=1).astype(BF16)
    cum_ref[...] = jnp.concatenate([b_f, b_b], axis=1)

    ii = lax.broadcasted_iota(jnp.int32, (chunk, chunk), 0)
    jj = lax.broadcasted_iota(jnp.int32, (chunk, chunk), 1)

    def rows(c):
        if isinstance(c, int):
            return pl.ds(c * chunk, chunk)
        return pl.ds(pl.multiple_of(c * chunk, chunk), chunk)

    def advance(c, d, slot):
        lanes = slice(128 * d, 128 * d + 128)
        edge = c * chunk + (chunk - 1 if d == 0 else 0)
        decay = cum_ref[pl.ds(edge, 1), :][:, lanes]
        kw = (k_ref[rows(c), :].astype(F32) * jnp.exp(decay - cum_ref[rows(c), :][:, lanes])).astype(BF16)
        st = st_ref[slot]
        sts_ref[c, :, lanes] = st.astype(BF16)
        st_ref[slot] = st * jnp.exp(decay) + _dot_tn(v_ref[rows(c), :], kw)

    if has_init:
        st_ref[0] = s0_ref[0].T
        st_ref[1] = s0_ref[1].T
    else:
        st_ref[...] = jnp.zeros((2 * n_seqs, DV_GLA, DK), F32)

    for seq in range(n_seqs):
        def state_group(g, carry, seq=seq):
            for u in range(unroll):
                c = g * unroll + u
                advance(seq * n_chunks + c, 0, 2 * seq)
                advance(seq * n_chunks + n_chunks - 1 - c, 1, 2 * seq + 1)
            return carry

        if n_chunks // unroll == 1:
            state_group(0, 0)
        else:
            lax.fori_loop(0, n_chunks // unroll, state_group, 0)

    out_unroll = min(GLA_OUT_UNROLL, total)

    def output_group(g, carry):
        for u in range(out_unroll):
            c = g * out_unroll + u
            qe = qe_ref[rows(c), :]
            ke = ke_ref[rows(c), :]
            att = (jnp.where(jj <= ii, _dot_nt(qe[:, :128], ke[:, :128]), 0.0)
                   + jnp.where(jj >= ii, _dot_nt(qe[:, 128:], ke[:, 128:]), 0.0))
            o = _dot(att.astype(BF16), v_ref[rows(c), :]) + _dot_nt(qe, sts_ref[c])
            gr = gr_ref[rows(c), :].astype(F32)
            out_ref[rows(c), :] = (_rms(o) * (gr * jax.nn.sigmoid(gr))).astype(BF16)
        return carry

    if total // out_unroll == 1:
        output_group(0, 0)
    else:
        lax.fori_loop(0, total // out_unroll, output_group, 0)
    if want_final:
        for seq in range(n_seqs):
            s_out_ref[seq, 0] = st_ref[2 * seq].T
            s_out_ref[seq, 1] = st_ref[2 * seq + 1].T


def _gla_mixer(proj, lowrank, wz, bz, group, init, layer_idx=0, carried=None):
    ctx = group == 0
    n_seq, seq_len = (N_CTX_SEQ, CTX_LEN) if ctx else (N_LAT_SEQ, LAT_LEN)
    row_off = 0 if ctx else N_CTX_TOK // LAT_LEN
    has_init, want_final = not ctx, ctx
    n_seqs = GLA_CTX_SEQS if ctx else 1
    n_rows = n_seqs * seq_len
    n_chunks = seq_len // GLA_CHUNK
    total = n_seqs * n_chunks
    in_specs = [pl.BlockSpec((n_rows, 128), lambda b, h: (row_off + b, h)),
                pl.BlockSpec((n_rows, 128), lambda b, h: (row_off + b, 4 + h)),
                pl.BlockSpec((n_rows, 256), lambda b, h: (row_off + b, 4 + h)),
                pl.BlockSpec((n_rows, 256), lambda b, h: (row_off + b, 8 + h)),
                pl.BlockSpec((n_rows, 128), lambda b, h: (row_off + b, 0)),
                pl.BlockSpec((None, 256, 256), lambda b, h: (h, 0, 0)),
                pl.BlockSpec((None, 1, 256), lambda b, h: (h, 0, 0))]
    args = [proj, proj, proj, proj, lowrank, wz, bz]
    if has_init:
        in_specs.append(pl.BlockSpec((None, None, 2, None, DK, DV_GLA), lambda b, h: (b, layer_idx, 0, h, 0, 0)))
        args.append(init)
    out_specs = [pl.BlockSpec((n_rows, 256), lambda b, h: (b, h))]
    out_shape = [jax.ShapeDtypeStruct((n_seq * seq_len, D), BF16)]
    aliases = {}
    if want_final:
        lay, at = (DEPTH // 2, 0) if carried is None else (None, layer_idx)
        out_specs.append(pl.BlockSpec((n_seqs, lay, 2, None, DK, DV_GLA), lambda b, h: (b, at, 0, h, 0, 0)))
        out_shape.append(jax.ShapeDtypeStruct((n_seq, DEPTH // 2, 2, HEADS, DK, DV_GLA), F32))
        if carried is not None:
            aliases = {len(args): 1}
            in_specs.append(pl.BlockSpec(memory_space=pl.ANY))
            args.append(carried)
    scratch = [pltpu.VMEM((n_rows, 256), BF16)] * 2 + [pltpu.VMEM((n_rows, 256), F32),
                                                       pltpu.VMEM((total, DV_GLA, 2 * DK), BF16),
                                                       pltpu.VMEM((2 * n_seqs, DV_GLA, DK), F32)]
    return pl.pallas_call(
        functools.partial(_gla_kernel, seq_len, n_seqs, has_init, want_final, len(aliases), layer_idx),
        grid=(n_seq // n_seqs, HEADS),
        in_specs=in_specs,
        out_specs=out_specs,
        out_shape=out_shape,
        input_output_aliases=aliases,
        scratch_shapes=scratch,
        compiler_params=_cparams("parallel", "parallel"),
        name="gla_mixer_ctx" if ctx else "gla_mixer_lat",
    )(*args)


def _tail_kernel(layer, tile, tile_off, n_tiles, n_carried, *refs):
    last = layer == DEPTH - 1
    it = iter(refs)
    hc_ref, hl_ref, hup_ref, hdn_ref, x_ref, xup_ref, xdn_ref = (next(it) for _ in range(7))
    wo_ref, win_ref, cw_ref, cb_ref, w_ref, mod_ref, g_mix_ref, g_pre_ref, g_post_ref = (next(it) for _ in range(9))
    if not last:
        g_next_ref, mod_next_ref = next(it), next(it)
    for _ in range(n_carried):
        next(it)
    x_out_ref = next(it)
    xn_out_ref = None if last else next(it)
    hbuf_ref = next(it)
    i = pl.program_id(0) + tile_off
    n_ctx = N_CTX_TOK // tile
    tiles_per_seq = LAT_LEN // tile
    n_chunks = D_FF // FF_CHUNK

    def cols(f):
        return slice(f * FF_CHUNK, (f + 1) * FF_CHUNK), slice(D_FF + f * FF_CHUNK, D_FF + (f + 1) * FF_CHUNK)

    def act(conv, v, sl):
        return (jax.nn.gelu(conv + cb_ref[:, sl]) * v).astype(BF16)

    def mixer_out(h, x, keep=None):
        m = mod_ref[...]
        x_mid = x + m[2:3] * (_rms(_dot(h, wo_ref[...])) * g_mix_ref[...])
        xn = _rms(x_mid) * g_pre_ref[...] * (1.0 + m[4:5]) + m[3:4]
        if keep is not None:
            xn = jnp.where(keep, xn, 0.0)
        return x_mid, xn.astype(BF16)

    def finish(x_mid):
        x_new = x_mid + mod_ref[5:6, :] * (_rms(_dot(hbuf_ref[...], w_ref[...])) * g_post_ref[...])
        x_out_ref[...] = x_new
        if not last:
            mn = mod_next_ref[...]
            xn_out_ref[...] = (_rms(x_new) * g_next_ref[...] * (1.0 + mn[1:2]) + mn[0:1]).astype(BF16)

    def ctx_branch():
        x_mid, xn = mixer_out(hc_ref[...], x_ref[...])
        pos = lax.broadcasted_iota(jnp.int32, (tile, 1), 0) % CTX_LEN
        for f in range(n_chunks):
            sl, sv = cols(f)
            a = _dot(xn, win_ref[:, sl])
            left = jnp.where(pos >= 1, pltpu.roll(a, 1, 0), 0.0)
            right = jnp.where(pos < CTX_LEN - 1, pltpu.roll(a, tile - 1, 0), 0.0)
            conv = cw_ref[3:4, sl] * left + cw_ref[4:5, sl] * a + cw_ref[5:6, sl] * right
            hbuf_ref[:, sl] = act(conv, _dot(xn, win_ref[:, sv]), sl)
        finish(x_mid)

    def lat_branch():
        t = (i - n_ctx) % tiles_per_seq
        n_rows = tile + 2 * GRID_W
        rowpos = lax.broadcasted_iota(jnp.int32, (n_rows, 1), 0)
        first = jnp.where(t > 0, 0, GRID_W)
        stop = jnp.where(t < tiles_per_seq - 1, n_rows, GRID_W + tile)
        x_mid, xe = mixer_out(jnp.concatenate([hup_ref[...], hl_ref[...], hdn_ref[...]], axis=0),
                              jnp.concatenate([xup_ref[...], x_ref[...], xdn_ref[...]], axis=0),
                              (rowpos >= first) & (rowpos < stop))
        x_mid = x_mid[GRID_W:GRID_W + tile]
        xn = xe[GRID_W:GRID_W + tile]
        colpos = rowpos % GRID_W
        for f in range(n_chunks):
            sl, sv = cols(f)
            buf = _dot(xe, win_ref[:, sl])
            left = jnp.where(colpos >= 1, pltpu.roll(buf, 1, 0), 0.0)
            right = jnp.where(colpos < GRID_W - 1, pltpu.roll(buf, n_rows - 1, 0), 0.0)
            conv = None
            for dr in range(3):
                rs = slice(dr * GRID_W, dr * GRID_W + tile)
                term = (cw_ref[3 * dr:3 * dr + 1, sl] * left[rs] + cw_ref[3 * dr + 1:3 * dr + 2, sl] * buf[rs]
                        + cw_ref[3 * dr + 2:3 * dr + 3, sl] * right[rs])
                conv = term if conv is None else conv + term
            hbuf_ref[:, sl] = act(conv, _dot(xn, win_ref[:, sv]), sl)
        finish(x_mid)

    if tile_off + n_tiles <= n_ctx:
        ctx_branch()
    elif tile_off >= n_ctx:
        lat_branch()
    else:
        pl.when(i < n_ctx)(ctx_branch)
        pl.when(i >= n_ctx)(lat_branch)


def _layer_tail(h_ctx, h_lat, w_o, w_in, conv_w, conv_b, w_out, x, mods, g_mix_post, g_ffn_pre, g_post, g_next,
                layer, group, x_dst=None, xn_dst=None):
    last = layer == DEPTH - 1
    tile = TAIL_TILE_CTX if group == 0 else TAIL_TILE_LAT
    tile_off = 0 if group == 0 else N_CTX_TOK // tile
    n_tiles = (N_CTX_TOK if group == 0 else N_TOK - N_CTX_TOK) // tile
    halo_per_tile = tile // GRID_W
    n_halo = N_TOK // GRID_W
    n_ctx = N_CTX_TOK // tile
    n_lat_halo = (N_TOK - N_CTX_TOK) // GRID_W
    row = pl.BlockSpec((tile, D), lambda i: (i + tile_off, 0))
    resident = dict(pipeline_mode=pl.Buffered(1))
    layer_vec = pl.BlockSpec((None, 1, D), lambda i: (layer, 0, 0))

    def lat_halo(i, shift):
        return jnp.clip((i + tile_off - n_ctx + shift) * halo_per_tile - (1 - shift), 0, n_lat_halo - 1)

    h_ctx_spec = (pl.BlockSpec((tile, D), lambda i: (i, 0)) if group == 0
                  else pl.BlockSpec((GRID_W, D), lambda i: (0, 0)))
    h_lat_spec = (pl.BlockSpec((tile, D), lambda i: (i, 0)) if group == 1
                  else pl.BlockSpec((GRID_W, D), lambda i: (0, 0)))
    in_specs = [h_ctx_spec, h_lat_spec,
                pl.BlockSpec((GRID_W, D), lambda i: (lat_halo(i, 0), 0)),
                pl.BlockSpec((GRID_W, D), lambda i: (lat_halo(i, 1), 0)),
                row,
                pl.BlockSpec((GRID_W, D), lambda i: (jnp.maximum((i + tile_off) * halo_per_tile - 1, 0), 0)),
                pl.BlockSpec((GRID_W, D), lambda i: (jnp.minimum((i + tile_off + 1) * halo_per_tile, n_halo - 1), 0)),
                pl.BlockSpec((D, D), lambda i: (0, 0), **resident),
                pl.BlockSpec((None, D, 2 * D_FF), lambda i: (layer, 0, 0), **resident),
                pl.BlockSpec((None, 9, D_FF), lambda i: (layer, 0, 0)),
                pl.BlockSpec((None, 1, D_FF), lambda i: (layer, 0, 0)),
                pl.BlockSpec((None, D_FF, D), lambda i: (layer, 0, 0), **resident),
                pl.BlockSpec((None, None, 6, D), lambda i: (layer, _mod_row(i + tile_off, tile), 0, 0)),
                layer_vec, layer_vec, layer_vec]
    args = [h_ctx, h_lat, h_lat, h_lat, x, x, x, w_o, w_in, conv_w.reshape(DEPTH, 9, D_FF), conv_b, w_out, mods,
            g_mix_post, g_ffn_pre, g_post]
    aliases = {}
    if last:
        out_specs = [pl.BlockSpec((tile, D), lambda i: (i, 0))]
        out_shape = [jax.ShapeDtypeStruct((n_tiles * tile, D), F32)]
    else:
        in_specs += [pl.BlockSpec((None, 1, D), lambda i: (layer + 1, 0, 0)),
                     pl.BlockSpec((None, None, 6, D), lambda i: (layer + 1, _mod_row(i + tile_off, tile), 0, 0))]
        args += [g_next, mods]
        aliases = {len(args): 0, len(args) + 1: 1}
        in_specs += [pl.BlockSpec(memory_space=pl.ANY)] * 2
        args += [x_dst, xn_dst]
        out_specs = [row, row]
        out_shape = [jax.ShapeDtypeStruct((N_TOK, D), F32), jax.ShapeDtypeStruct((N_TOK, D), BF16)]
    return pl.pallas_call(
        functools.partial(_tail_kernel, layer, tile, tile_off, n_tiles, len(aliases)),
        grid=(n_tiles,),
        in_specs=in_specs,
        out_specs=out_specs,
        out_shape=out_shape,
        input_output_aliases=aliases,
        scratch_shapes=[pltpu.VMEM((tile, D_FF), BF16)],
        compiler_params=_cparams("parallel"),
        name="layer_tail",
    )(*args)


def _even_weights(w_gate_cols, w_out):
    src = [0] * (8 * HEADS)
    used = [0.0] * (8 * HEADS)
    for h in range(HEADS):
        for k, col in enumerate((h, 8 + h, 4 + h, 12 + h)):
            src[8 * h + k] = col
            used[8 * h + k] = 1.0
    w_gate = (w_gate_cols[:, jnp.array(src)] * jnp.array(used, F32)).T.astype(BF16)
    w_o = w_out.reshape(2, HEADS, 128, D).transpose(1, 0, 2, 3).reshape(D, D).astype(BF16)
    return w_gate, w_o


def _odd_weights(w_lr_cols, w2, b2):
    w_lr = jnp.tile(w_lr_cols, (1, 128 // (2 * GLA_RANK))).astype(BF16)
    w2h = w2.reshape(2, GLA_RANK, HEADS, DK).transpose(2, 0, 1, 3)
    w32 = jnp.zeros((HEADS, 2 * GLA_RANK, 256), F32)
    w32 = w32.at[:, :GLA_RANK, :DK].set(w2h[:, 0]).at[:, GLA_RANK:, DK:].set(w2h[:, 1])
    hi = w32.astype(BF16)
    mid = (w32 - hi.astype(F32)).astype(BF16)
    low = (w32 - hi.astype(F32) - mid.astype(F32)).astype(BF16)
    none = jnp.zeros_like(hi)
    wz = jnp.concatenate([hi, hi, hi, mid, mid, low, none, none], axis=1)
    bz = b2.reshape(2, HEADS, DK).transpose(1, 0, 2).reshape(HEADS, 1, 2 * DK)
    return w_lr, wz, bz


def kernel(x_prompt, x_sample, c, c_ctx, state_mlstm_C, state_mlstm_n, state_mlstm_m, state_ret, state_gla, ada_w, ada_b, norm_mix_pre, norm_mix_post, norm_ffn_pre, norm_ffn_post, w_in_even, w_out_even, mlstm_igate_b, mlstm_fgate_b, ret_decay_logit, w_in_odd, gla_gate_w2, gla_gate_b, w_out_odd, ffn_w_in, ffn_conv_w, ffn_conv_b, ffn_w_out):
    cond = jnp.concatenate([c_ctx[None, :], c, jnp.zeros((8 - 1 - N_LAT_SEQ, D), F32)], axis=0)
    mods = _mod_table(cond, ada_w, ada_b)
    norm_mix_pre, norm_mix_post, norm_ffn_pre, norm_ffn_post = (
        g.reshape(DEPTH, 1, D) for g in (norm_mix_pre, norm_mix_post, norm_ffn_pre, norm_ffn_post))
    ffn_conv_b = ffn_conv_b.reshape(DEPTH, 1, D_FF)
    ffn_w_in_bf, ffn_w_out_bf = ffn_w_in.astype(BF16), ffn_w_out.astype(BF16)
    w_even_bf = jnp.concatenate([w_in_even[:, :, :2048], w_in_even[:, :, 2064:]], axis=2).astype(BF16)
    w_odd_bf = w_in_odd.astype(BF16)
    gate_cols = w_in_even[:, :, 2048:2064]
    lr_cols = w_in_odd[:, :, 3072:3072 + 2 * GLA_RANK]

    x, xn = _prenorm(x_prompt.reshape(N_CTX_TOK, D), x_sample.reshape(N_LAT_SEQ * LAT_LEN, D),
                     norm_mix_pre, mods, 0)
    x_spare = jnp.zeros((N_TOK, D), F32)
    even_finals, gla_finals = None, None
    for layer in range(DEPTH):
        idx = layer // 2
        if layer % 2 == 0:
            w_gate, w_o = _even_weights(gate_cols[idx], w_out_even[idx])
            proj, gates = _proj(xn, w_even_bf, idx, 4096, w_gate, True, "proj_even")
            params = jnp.concatenate([mlstm_igate_b[idx].reshape(-1), mlstm_fgate_b[idx].reshape(-1),
                                      ret_decay_logit[idx].reshape(-1)])
            h_ctx, *even_finals = _even_mixer(proj, gates, params, 0, None, idx, even_finals)
            init = (state_mlstm_C, state_mlstm_n, state_mlstm_m, state_ret)
            (h_lat,) = _even_mixer(proj, gates, params, 1, init, idx)
        else:
            w_lr, wz, bz = _odd_weights(lr_cols[idx], gla_gate_w2[idx], gla_gate_b[idx])
            w_o = w_out_odd[idx].astype(BF16)
            proj, lowrank = _proj(xn, w_odd_bf, idx, 3072, w_lr, False, "proj_odd")
            h_ctx, gla_finals = _gla_mixer(proj, lowrank, wz, bz, 0, None, idx, gla_finals)
            (h_lat,) = _gla_mixer(proj, lowrank, wz, bz, 1, state_gla, idx)
        tail = functools.partial(_layer_tail, h_ctx, h_lat, w_o, ffn_w_in_bf, ffn_conv_w, ffn_conv_b, ffn_w_out_bf,
                                 x, mods, norm_mix_post, norm_ffn_pre, norm_ffn_post, norm_mix_pre, layer)
        if layer < DEPTH - 1:
            x_new, xn_new = tail(0, x_spare, xn)
            x_new, xn = tail(1, x_new, xn_new)
            x, x_spare = x_new, x
        else:
            (y_ctx,) = tail(0)
            (y_lat,) = tail(1)

    y_prompt = y_ctx.reshape(N_CTX_SEQ, CTX_LEN, D)
    y_sample = y_lat.reshape(N_LAT_SEQ, LAT_LEN, D)
    new_c, new_n, new_m, new_r = even_finals
    return (y_prompt, y_sample, new_c, new_n[..., 0, :], new_m[:, :, :, 0:2, 0].transpose(0, 1, 3, 2), new_r, gla_finals)
```

```python
import functools
import math

import jax
import jax.numpy as jnp
from jax import lax
from jax.experimental import pallas as pl
from jax.experimental.pallas import tpu as pltpu

F32 = jnp.float32
BF16 = jnp.bfloat16

D = 1024
DEPTH = 4
N_CTX_SEQ, CTX_LEN = 16, 256
N_LAT_SEQ, LAT_LEN = 2, 2048
GRID_W = 64
N_CTX_TOK = N_CTX_SEQ * CTX_LEN
N_TOK = N_CTX_TOK + N_LAT_SEQ * LAT_LEN
HEADS = 4
DK = 128
DV_GLA = 256
GLA_RANK = 16
GLA_TAU = 16.0
D_FF = 2816
EPS = 1e-6
NEG = -1e30

ROW_TILE = 1024
PROJ_ROW_TILE = 2048
EVEN_CHUNK = 256
EVEN_UNROLL = 4
EVEN_CTX_SEQS = 8
GLA_CHUNK = 64
GLA_UNROLL = 8
GLA_CTX_SEQS = 8
GLA_OUT_UNROLL = 16
FF_CHUNK = 256
TAIL_TILE_CTX = 512
TAIL_TILE_LAT = 512
VMEM_LIMIT = 56 * 1024 * 1024


def _cparams(*sem):
    return pltpu.CompilerParams(dimension_semantics=sem, vmem_limit_bytes=VMEM_LIMIT)


def _mod_row(i, tile):
    n_ctx = N_CTX_TOK // tile
    return jnp.where(i < n_ctx, 0, 1 + (i - n_ctx) // (LAT_LEN // tile))


def _rms(x):
    return x * lax.rsqrt(jnp.mean(x * x, axis=-1, keepdims=True) + EPS)


def _log_sigmoid(x):
    return jnp.minimum(x, 0.0) - jnp.log(1.0 + jnp.exp(-jnp.abs(x)))


def _dot(a, b):
    return jnp.dot(a, b, preferred_element_type=F32)


def _dot_nt(a, b):
    return lax.dot_general(a, b, (((1,), (1,)), ((), ())), preferred_element_type=F32)


def _dot_tn(a, b):
    return lax.dot_general(a, b, (((0,), (0,)), ((), ())), preferred_element_type=F32)


def _seg_scan(x, seg, pos, op, ident, reverse):
    n = x.shape[0]
    s = 1
    while s < seg:
        if reverse:
            x = op(x, jnp.where(pos < seg - s, pltpu.roll(x, n - s, 0), ident))
        else:
            x = op(x, jnp.where(pos >= s, pltpu.roll(x, s, 0), ident))
        s *= 2
    return x


def _mod_kernel(c_ref, w_ref, b_ref, o_ref):
    c = c_ref[...]
    s = (c * jax.nn.sigmoid(c)).astype(BF16)
    o_ref[0] = _dot(s, w_ref[0].astype(BF16)) + b_ref[0]


def _mod_table(cond, ada_w, ada_b):
    tn = 2048
    out = pl.pallas_call(
        _mod_kernel,
        grid=(DEPTH, 6 * D // tn),
        in_specs=[pl.BlockSpec((8, D), lambda l, j: (0, 0)),
                  pl.BlockSpec((1, D, tn), lambda l, j: (l, 0, j)),
                  pl.BlockSpec((1, 1, tn), lambda l, j: (l, 0, j))],
        out_specs=pl.BlockSpec((1, 8, tn), lambda l, j: (l, 0, j)),
        out_shape=jax.ShapeDtypeStruct((DEPTH, 8, 6 * D), F32),
        compiler_params=_cparams("parallel", "parallel"),
        name="mod_table",
    )(cond, ada_w, ada_b.reshape(DEPTH, 1, 6 * D))
    return out.reshape(DEPTH, 8, 6, D)


def _prenorm_kernel(xp_ref, xs_ref, g_ref, mod_ref, x_ref, xn_ref):
    i = pl.program_id(0)

    def run(src_ref):
        x = src_ref[...]
        m = mod_ref[...]
        x_ref[...] = x
        xn_ref[...] = (_rms(x) * g_ref[...] * (1.0 + m[1:2]) + m[0:1]).astype(BF16)

    pl.when(i < N_CTX_TOK // ROW_TILE)(lambda: run(xp_ref))
    pl.when(i >= N_CTX_TOK // ROW_TILE)(lambda: run(xs_ref))


def _prenorm(xp, xs, g, mods, layer):
    nc = N_CTX_TOK // ROW_TILE
    return pl.pallas_call(
        _prenorm_kernel,
        grid=(N_TOK // ROW_TILE,),
        in_specs=[pl.BlockSpec((ROW_TILE, D), lambda i: (jnp.minimum(i, nc - 1), 0)),
                  pl.BlockSpec((ROW_TILE, D), lambda i: (jnp.maximum(i - nc, 0), 0)),
                  pl.BlockSpec((None, 1, D), lambda i: (layer, 0, 0)),
                  pl.BlockSpec((None, None, 6, D), lambda i: (layer, _mod_row(i, ROW_TILE), 0, 0))],
        out_specs=[pl.BlockSpec((ROW_TILE, D), lambda i: (i, 0)),
                   pl.BlockSpec((ROW_TILE, D), lambda i: (i, 0))],
        out_shape=[jax.ShapeDtypeStruct((N_TOK, D), F32), jax.ShapeDtypeStruct((N_TOK, D), BF16)],
        compiler_params=_cparams("parallel"),
        name="prenorm",
    )(xp, xs, g, mods)


def _proj_kernel(small_transposed, x_ref, w_ref, ws_ref, o_ref, os_ref):
    x = x_ref[...]
    o_ref[...] = _dot(x, w_ref[...]).astype(o_ref.dtype)

    @pl.when(pl.program_id(1) == 0)
    def _():
        os_ref[...] = _dot_nt(ws_ref[...], x) if small_transposed else _dot(x, ws_ref[...])


def _proj(xn, w_all, layer_idx, n, w_small, small_transposed, name):
    tm = PROJ_ROW_TILE
    tn = n // 2
    m, k = xn.shape
    if small_transposed:
        r = w_small.shape[0]
        small_spec = pl.BlockSpec((r, tm), lambda i, j: (0, i))
        small_shape = jax.ShapeDtypeStruct((r, m), F32)
    else:
        small_spec = pl.BlockSpec((tm, w_small.shape[1]), lambda i, j: (i, 0))
        small_shape = jax.ShapeDtypeStruct((m, w_small.shape[1]), F32)
    return pl.pallas_call(
        functools.partial(_proj_kernel, small_transposed),
        grid=(m // tm, n // tn),
        in_specs=[pl.BlockSpec((tm, k), lambda i, j: (i, 0)),
                  pl.BlockSpec((None, k, tn), lambda i, j: (layer_idx, 0, j)),
                  pl.BlockSpec(w_small.shape, lambda i, j: (0, 0))],
        out_specs=[pl.BlockSpec((tm, tn), lambda i, j: (i, j)), small_spec],
        out_shape=[jax.ShapeDtypeStruct((m, n), BF16), small_shape],
        compiler_params=_cparams("parallel", "arbitrary"),
        name=name,
    )(xn, w_all, w_small)


def _even_kernel(seq_len, n_seqs, init_layer, want_final, n_carried, out_layer, *refs):
    has_init = init_layer is not None
    layer_idx, n_layers = init_layer if has_init else (0, 1)
    chunk = EVEN_CHUNK
    n_chunks = seq_len // chunk
    total = n_seqs * n_chunks
    use_cross = has_init or n_chunks > 1
    unroll = min(EVEN_UNROLL, n_chunks)
    assert not has_init or n_seqs == 1
    it = iter(refs)
    par_ref = next(it)
    m0_ref = next(it) if has_init else None
    mq_ref, mk_ref, mv_ref, mo_ref, rq_ref, rk_ref, rv_ref, rg_ref, gate_ref = (next(it) for _ in range(9))
    if has_init:
        c0_ref, n0_ref, r0_ref = next(it), next(it), next(it)
    for _ in range(n_carried):
        next(it)
    out_ref = next(it)
    if want_final:
        finals = [next(it) for _ in range(4)]
        if n_carried == 0:
            for ref in finals:
                ref[...] = jnp.zeros(ref.shape, F32)
            finals = [ref.at[:, out_layer] for ref in finals]
        c_out_ref, n_out_ref, m_out_ref, r_out_ref = finals
    grow_ref, rep_ref, edge_ref, cn_ref, s_ref = (next(it) for _ in range(5))
    if use_cross:
        cns_ref, ss_ref, ms_ref = next(it), next(it), next(it)

    b = pl.program_id(0)
    h = pl.program_id(1)
    ii = lax.broadcasted_iota(jnp.int32, (chunk, chunk), 0)
    jj = lax.broadcasted_iota(jnp.int32, (chunk, chunk), 1)
    pos_col = lax.broadcasted_iota(jnp.int32, (chunk, 1), 0).astype(F32)
    pos_int = lax.broadcasted_iota(jnp.int32, (chunk, 1), 0)
    ones_blk = jnp.ones((chunk, 128), BF16)
    ones_sum = jnp.ones((256, 128), BF16)

    sub = lax.broadcasted_iota(jnp.int32, (8, 1), 0)
    bias = jnp.where(sub == 0, par_ref[h], jnp.where(sub == 1, par_ref[4 + h],
                     jnp.where(sub == 2, par_ref[8 + h], jnp.where(sub == 3, par_ref[12 + h], 0.0))))
    lane = lax.broadcasted_iota(jnp.int32, (1, 128), 1)
    tri_pre = (ii <= jj).astype(BF16)
    tri_suf = (ii >= jj).astype(BF16)
    def lanes(c):
        if isinstance(c, int):
            return slice(c * chunk, (c + 1) * chunk)
        return pl.ds(pl.multiple_of(c * chunk, chunk), chunk)

    def rows(c):
        if isinstance(c, int):
            return pl.ds(c * chunk, chunk)
        return pl.ds(pl.multiple_of(c * chunk, chunk), chunk)

    def prepare_chunk(c):
        pre = gate_ref[:, lanes(c)] + bias
        log_f = pltpu.roll(_log_sigmoid(pre), 6, 0)
        hi = log_f.astype(BF16)
        rest = log_f - hi.astype(F32)
        mid = rest.astype(BF16)
        low = (rest - mid.astype(F32)).astype(BF16)
        parts = jnp.concatenate([hi, mid, low, jnp.zeros_like(hi)], axis=0)
        cum_pre = _dot(parts, tri_pre)
        cum_suf = _dot(parts, tri_suf)
        fcum = jnp.where(sub == 0, cum_pre[0:8] + cum_pre[8:16] + cum_pre[16:24],
                         cum_suf[0:8] + cum_suf[8:16] + cum_suf[16:24])
        gsc = pre - fcum
        packed = jnp.where(sub < 2, gsc, pltpu.roll(fcum, 2, 0))
        grow_ref[:, lanes(c)] = packed
        pcol = packed.T
        g_rep = [jnp.broadcast_to(pcol[:, k:k + 1], (chunk, 128)) for k in range(4)]
        g_rep.append(_seg_scan(g_rep[0], chunk, pos_int, jnp.maximum, NEG, False))
        g_rep.append(_seg_scan(g_rep[1], chunk, pos_int, jnp.maximum, NEG, True))
        rep_ref[rows(c), :] = jnp.concatenate(g_rep, axis=1)
        f_tot = jnp.sum(log_f, axis=1, keepdims=True)
        g_top = jnp.max(gsc, axis=1, keepdims=True)
        edge_ref[c] = jnp.where(lane == 0, f_tot, jnp.where(lane == 1, g_top, 0.0))

    def for_all_chunks(body):
        if total <= 2 * EVEN_UNROLL:
            for c in range(total):
                body(c)
        else:
            def group(g, carry):
                for u in range(EVEN_UNROLL):
                    body(g * EVEN_UNROLL + u)
                return carry
            lax.fori_loop(0, total // EVEN_UNROLL, group, 0)

    for_all_chunks(prepare_chunk)

    k_scale = DK ** -0.5
    lg_f = _log_sigmoid(jnp.full((1, 1), par_ref[16 + h], F32))
    lg_b = _log_sigmoid(jnp.full((1, 1), par_ref[20 + h], F32))
    diff = (ii - jj).astype(F32)
    ret_mask = k_scale * (jnp.where(diff >= 0, jnp.exp(lg_f * jnp.maximum(diff, 0.0)), 0.0)
                          + jnp.where(diff <= 0, jnp.exp(lg_b * jnp.maximum(-diff, 0.0)), 0.0))
    rq_dec = (jnp.exp(lg_f * (pos_col + 1.0)), jnp.exp(lg_b * (chunk - pos_col)))
    rk_dec = (k_scale * jnp.exp(lg_f * (chunk - 1.0 - pos_col)), k_scale * jnp.exp(lg_b * pos_col))
    r_chunk = (jnp.exp(lg_f * chunk), jnp.exp(lg_b * chunk))

    def load_kv(c):
        return mk_ref[rows(c), :], jnp.concatenate([mv_ref[rows(c), :], ones_blk], axis=1)

    def ret_kv(c):
        return rk_ref[rows(c), :], rv_ref[rows(c), :]

    def advance(c, d, m, slot):
        if use_cross:
            cns_ref[c, :, 256 * d:256 * d + 256] = cn_ref[slot].astype(BF16)
            ss_ref[c, 128 * d:128 * d + 128, :] = s_ref[slot].astype(BF16)
            ms_ref[c, d:d + 1, :] = jnp.broadcast_to(m, (1, 128))
        edge = edge_ref[c]
        mu = jnp.maximum(m, edge[d:d + 1, 1:2])
        k, vext = load_kv(c)
        kw = (k.astype(F32) * (k_scale * jnp.exp(rep_ref[rows(c), 128 * d:128 * d + 128] - mu))).astype(BF16)
        cn_ref[slot] = jnp.exp(m - mu) * cn_ref[slot] + _dot_tn(kw, vext)
        rk, rv = ret_kv(c)
        s_ref[slot] = r_chunk[d] * s_ref[slot] + _dot_tn((rk.astype(F32) * rk_dec[d]).astype(BF16), rv)
        return edge[d:d + 1, 0:1] + mu

    if has_init:
        base = (b * n_layers + layer_idx) * 8
        m_init = (jnp.full((1, 1), m0_ref[base + h], F32), jnp.full((1, 1), m0_ref[base + 4 + h], F32))
        for d in range(2):
            cn_ref[d] = jnp.concatenate([c0_ref[d], jnp.broadcast_to(n0_ref[d], (DK, 128))], axis=1)
            s_ref[d] = r0_ref[d]
    else:
        m_init = (jnp.zeros((1, 1), F32), jnp.zeros((1, 1), F32))
        cn_ref[...] = jnp.zeros((2 * n_seqs, DK, 256), F32)
        s_ref[...] = jnp.zeros((2 * n_seqs, DK, DK), F32)

    m_fin = []
    for seq in range(n_seqs):
        def state_group(g, carry, seq=seq):
            m_f, m_b = carry
            for u in range(unroll):
                c = g * unroll + u
                m_f = advance(seq * n_chunks + c, 0, m_f, 2 * seq)
                m_b = advance(seq * n_chunks + n_chunks - 1 - c, 1, m_b, 2 * seq + 1)
            return m_f, m_b

        if n_chunks // unroll > 1:
            m_fin.append(lax.fori_loop(0, n_chunks // unroll, state_group, m_init))
        elif use_cross or want_final:
            m_fin.append(state_group(0, m_init))

    def rms_rep(x):
        sq = x * x
        hi = sq.astype(BF16)
        low = (sq - hi.astype(F32)).astype(BF16)
        ssum = _dot(jnp.concatenate([hi, low], axis=1), ones_sum)
        return x * lax.rsqrt(ssum * (1.0 / 128) + EPS)

    def output_chunk(c):
        q = mq_ref[rows(c), :]
        k, vext = load_kv(c)
        scores = _dot_nt(q, k)
        cols = rep_ref[rows(c), :]
        f_f, f_b, gm_f, gm_b = cols[:, 256:384], cols[:, 384:512], cols[:, 512:640], cols[:, 640:768]
        row = grow_ref[:, lanes(c)] + math.log(k_scale)
        if use_cross:
            m_f = ms_ref[c, 0:1, :][:, 0:1]
            m_b = ms_ref[c, 1:2, :][:, 0:1]
        else:
            m_f, m_b = m_init
        mu_f = jnp.maximum(m_f, gm_f)
        mu_b = jnp.maximum(m_b, gm_b)
        p_f = jnp.exp(jnp.where(jj <= ii, row[0:1, :], NEG) - jnp.concatenate([mu_f, mu_f], axis=1)) * scores
        p_b = jnp.exp(jnp.where(jj >= ii, row[1:2, :], NEG) - jnp.concatenate([mu_b, mu_b], axis=1)) * scores
        tot = _dot(jnp.concatenate([p_f, p_b], axis=0).astype(BF16), vext)
        num_f, den_f, num_b, den_b = tot[:chunk, :128], tot[:chunk, 128:], tot[chunk:, :128], tot[chunk:, 128:]
        if use_cross:
            cross = _dot(q, cns_ref[c])
            w_f = jnp.exp(m_f - mu_f)
            w_b = jnp.exp(m_b - mu_b)
            num_f, den_f = num_f + w_f * cross[:, 0:128], den_f + w_f * cross[:, 128:256]
            num_b, den_b = num_b + w_b * cross[:, 256:384], den_b + w_b * cross[:, 384:512]
        h_f = num_f / jnp.maximum(jnp.abs(den_f), jnp.exp(-(f_f + mu_f)))
        h_b = num_b / jnp.maximum(jnp.abs(den_b), jnp.exp(-(f_b + mu_b)))
        out_m = rms_rep(h_f + h_b) * jax.nn.sigmoid(mo_ref[rows(c), :].astype(F32))
        rq = rq_ref[rows(c), :]
        rk, rv = ret_kv(c)
        o_r = _dot((_dot_nt(rq, rk) * ret_mask).astype(BF16), rv)
        if use_cross:
            rq32 = rq.astype(F32)
            q_in = jnp.concatenate([rq32 * rq_dec[0], rq32 * rq_dec[1]], axis=1).astype(BF16)
            o_r = o_r + _dot(q_in, ss_ref[c])
        rg = rg_ref[rows(c), :].astype(F32)
        out_r = rms_rep(o_r) * (rg * jax.nn.sigmoid(rg))
        out_ref[rows(c), :] = jnp.concatenate([out_m, out_r], axis=1).astype(BF16)

    for_all_chunks(output_chunk)

    if want_final:
        for seq in range(n_seqs):
            for d in range(2):
                cn = cn_ref[2 * seq + d]
                c_out_ref[seq, d] = cn[:, :DK]
                n_out_ref[seq, d] = cn[:, DK:].T[0:8, :]
                r_out_ref[seq, d] = s_ref[2 * seq + d]
            m_out_ref[seq] = jnp.concatenate([jnp.broadcast_to(m_fin[seq][0], (1, 128)),
                                              jnp.broadcast_to(m_fin[seq][1], (1, 128)), jnp.zeros((6, 128), F32)], axis=0)


def _even_mixer(proj, gates_t, params, group, init, layer_idx=0, carried=None):
    ctx = group == 0
    n_seq, seq_len = (N_CTX_SEQ, CTX_LEN) if ctx else (N_LAT_SEQ, LAT_LEN)
    row_off = 0 if ctx else N_CTX_TOK // LAT_LEN
    has_init, want_final = not ctx, ctx
    n_seqs = EVEN_CTX_SEQS if ctx else 1
    rows = n_seqs * seq_len
    n_chunks = seq_len // EVEN_CHUNK
    total = n_seqs * n_chunks
    use_cross = has_init or n_chunks > 1

    def col(k):
        return pl.BlockSpec((rows, 128), lambda b, h: (row_off + b, 4 * k + h))

    smem = pl.BlockSpec(memory_space=pltpu.SMEM)
    in_specs = [smem]
    args = [params]
    if has_init:
        c0, n0, m0, r0 = init
        n_layers = c0.shape[1]
        in_specs.append(smem)
        args.append(m0.reshape(-1))
    in_specs += [col(k) for k in range(8)]
    in_specs.append(pl.BlockSpec((8, rows), lambda b, h: (h, row_off + b)))
    args += [proj] * 8 + [gates_t]
    if has_init:
        in_specs += [pl.BlockSpec((None, None, 2, None, DK, 128), lambda b, h: (b, layer_idx, 0, h, 0, 0)),
                     pl.BlockSpec((None, None, 2, None, DK, 1), lambda b, h: (b, layer_idx, 0, h, 0, 0)),
                     pl.BlockSpec((None, None, 2, None, DK, 128), lambda b, h: (b, layer_idx, 0, h, 0, 0))]
        args += [c0, n0.reshape(n0.shape + (1,)), r0]
    out_specs = [pl.BlockSpec((rows, 256), lambda b, h: (b, h))]
    out_shape = [jax.ShapeDtypeStruct((n_seq * seq_len, D), BF16)]
    aliases = {}
    if want_final:
        n_even = (DEPTH + 1) // 2
        if carried is None:
            lay, at = n_even, 0
        else:
            lay, at = None, layer_idx
        out_specs += [pl.BlockSpec((n_seqs, lay, 2, None, DK, DK), lambda b, h: (b, at, 0, h, 0, 0)),
                      pl.BlockSpec((n_seqs, lay, 2, None, 8, DK), lambda b, h: (b, at, 0, h, 0, 0)),
                      pl.BlockSpec((n_seqs, lay, None, 8, 128), lambda b, h: (b, at, h, 0, 0)),
                      pl.BlockSpec((n_seqs, lay, 2, None, DK, DK), lambda b, h: (b, at, 0, h, 0, 0))]
        out_shape += [jax.ShapeDtypeStruct((n_seq, n_even, 2, HEADS, DK, DK), F32),
                      jax.ShapeDtypeStruct((n_seq, n_even, 2, HEADS, 8, DK), F32),
                      jax.ShapeDtypeStruct((n_seq, n_even, HEADS, 8, 128), F32),
                      jax.ShapeDtypeStruct((n_seq, n_even, 2, HEADS, DK, DK), F32)]
        if carried is not None:
            aliases = {len(args) + k: 1 + k for k in range(len(carried))}
            in_specs += [pl.BlockSpec(memory_space=pl.ANY)] * len(carried)
            args += list(carried)
    scratch = [pltpu.VMEM((8, rows), F32), pltpu.VMEM((rows, 768), F32), pltpu.VMEM((total, 8, 128), F32),
               pltpu.VMEM((2 * n_seqs, DK, 256), F32), pltpu.VMEM((2 * n_seqs, DK, DK), F32)]
    if use_cross:
        scratch += [pltpu.VMEM((total, DK, 512), BF16), pltpu.VMEM((total, 2 * DK, DK), BF16),
                    pltpu.VMEM((total, 8, 128), F32)]
    return pl.pallas_call(
        functools.partial(_even_kernel, seq_len, n_seqs, (layer_idx, n_layers) if has_init else None, want_final,
                          len(aliases), layer_idx),
        grid=(n_seq // n_seqs, HEADS),
        in_specs=in_specs,
        out_specs=out_specs,
        out_shape=out_shape,
        input_output_aliases=aliases,
        scratch_shapes=scratch,
        compiler_params=_cparams("parallel", "parallel"),
        name="even_mixer_ctx" if ctx else "even_mixer_lat",
    )(*args)


def _gla_kernel(seq_len, n_seqs, has_init, want_final, n_carried, out_layer, *refs):
    chunk = GLA_CHUNK
    n_chunks = seq_len // chunk
    total = n_seqs * n_chunks
    n_rows = n_seqs * seq_len
    assert not has_init or n_seqs == 1
    it = iter(refs)
    q_ref, k_ref, v_ref, gr_ref, lr_ref, wz_ref, bz_ref = (next(it) for _ in range(7))
    s0_ref = next(it) if has_init else None
    for _ in range(n_carried):
        next(it)
    out_ref = next(it)
    s_out_ref = next(it) if want_final else None
    if want_final and n_carried == 0:
        s_out_ref[...] = jnp.zeros(s_out_ref.shape, F32)
        s_out_ref = s_out_ref.at[:, out_layer]
    qe_ref, ke_ref, cum_ref, sts_ref, st_ref = (next(it) for _ in range(5))
    unroll = min(GLA_UNROLL, n_chunks)

    x4 = lr_ref[...]
    hi = x4.astype(BF16).astype(F32)
    mid = (x4 - hi).astype(BF16).astype(F32)
    low = ((x4 - hi) - mid).astype(BF16).astype(F32)
    grp = lax.broadcasted_iota(jnp.int32, (1, 128), 1) // (2 * GLA_RANK)
    lhs = jnp.concatenate([jnp.where(grp == 1, mid, jnp.where(grp == 2, low, hi)),
                           jnp.where(grp == 0, mid, jnp.where(grp == 1, hi, 0.0))], axis=1).astype(BF16)
    z = _dot(lhs, wz_ref[...]) + bz_ref[...]
    la = _log_sigmoid(z) / GLA_TAU
    pos = lax.broadcasted_iota(jnp.int32, (n_rows, 1), 0) % chunk
    b_f = _seg_scan(la[:, :128], chunk, pos, jnp.add, 0.0, False)
    b_b = _seg_scan(la[:, 128:], chunk, pos, jnp.add, 0.0, True)
    q = q_ref[...].astype(F32) * (DK ** -0.5)
    k = k_ref[...].astype(F32)
    qe_ref[...] = jnp.concatenate([q * jnp.exp(b_f), q * jnp.exp(b_b)], axis=1).astype(BF16)
    ke_ref[...] = jnp.concatenate([k * jnp.exp(-b_f), k * jnp.exp(-b_b)], axis=1).astype(BF16)
    cum_ref[...] = jnp.concatenate([b_f, b_b], axis=1)

    ii = lax.broadcasted_iota(jnp.int32, (chunk, chunk), 0)
    jj = lax.broadcasted_iota(jnp.int32, (chunk, chunk), 1)

    def rows(c):
        if isinstance(c, int):
            return pl.ds(c * chunk, chunk)
        return pl.ds(pl.multiple_of(c * chunk, chunk), chunk)

    def advance(c, d, slot):
        lanes = slice(128 * d, 128 * d + 128)
        edge = c * chunk + (chunk - 1 if d == 0 else 0)
        decay = cum_ref[pl.ds(edge, 1), :][:, lanes]
        kw = (k_ref[rows(c), :].astype(F32) * jnp.exp(decay - cum_ref[rows(c), :][:, lanes])).astype(BF16)
        st = st_ref[slot]
        sts_ref[c, :, lanes] = st.astype(BF16)
        st_ref[slot] = st * jnp.exp(decay) + _dot_tn(v_ref[rows(c), :], kw)

    if has_init:
        st_ref[0] = s0_ref[0].T
        st_ref[1] = s0_ref[1].T
    else:
        st_ref[...] = jnp.zeros((2 * n_seqs, DV_GLA, DK), F32)

    for seq in range(n_seqs):
        def state_group(g, carry, seq=seq):
            for u in range(unroll):
                c = g * unroll + u
                advance(seq * n_chunks + c, 0, 2 * seq)
                advance(seq * n_chunks + n_chunks - 1 - c, 1, 2 * seq + 1)
            return carry

        if n_chunks // unroll == 1:
            state_group(0, 0)
        else:
            lax.fori_loop(0, n_chunks // unroll, state_group, 0)

    out_unroll = min(GLA_OUT_UNROLL, total)

    def output_group(g, carry):
        for u in range(out_unroll):
            c = g * out_unroll + u
            qe = qe_ref[rows(c), :]
            ke = ke_ref[rows(c), :]
            att = (jnp.where(jj <= ii, _dot_nt(qe[:, :128], ke[:, :128]), 0.0)
                   + jnp.where(jj >= ii, _dot_nt(qe[:, 128:], ke[:, 128:]), 0.0))
            o = _dot(att.astype(BF16), v_ref[rows(c), :]) + _dot_nt(qe, sts_ref[c])
            gr = gr_ref[rows(c), :].astype(F32)
            out_ref[rows(c), :] = (_rms(o) * (gr * jax.nn.sigmoid(gr))).astype(BF16)
        return carry

    if total // out_unroll == 1:
        output_group(0, 0)
    else:
        lax.fori_loop(0, total // out_unroll, output_group, 0)
    if want_final:
        for seq in range(n_seqs):
            s_out_ref[seq, 0] = st_ref[2 * seq].T
            s_out_ref[seq, 1] = st_ref[2 * seq + 1].T


def _gla_mixer(proj, lowrank, wz, bz, group, init, layer_idx=0, carried=None):
    ctx = group == 0
    n_seq, seq_len = (N_CTX_SEQ, CTX_LEN) if ctx else (N_LAT_SEQ, LAT_LEN)
    row_off = 0 if ctx else N_CTX_TOK // LAT_LEN
    has_init, want_final = not ctx, ctx
    n_seqs = GLA_CTX_SEQS if ctx else 1
    n_rows = n_seqs * seq_len
    n_chunks = seq_len // GLA_CHUNK
    total = n_seqs * n_chunks
    in_specs = [pl.BlockSpec((n_rows, 128), lambda b, h: (row_off + b, h)),
                pl.BlockSpec((n_rows, 128), lambda b, h: (row_off + b, 4 + h)),
                pl.BlockSpec((n_rows, 256), lambda b, h: (row_off + b, 4 + h)),
                pl.BlockSpec((n_rows, 256), lambda b, h: (row_off + b, 8 + h)),
                pl.BlockSpec((n_rows, 128), lambda b, h: (row_off + b, 0)),
                pl.BlockSpec((None, 256, 256), lambda b, h: (h, 0, 0)),
                pl.BlockSpec((None, 1, 256), lambda b, h: (h, 0, 0))]
    args = [proj, proj, proj, proj, lowrank, wz, bz]
    if has_init:
        in_specs.append(pl.BlockSpec((None, None, 2, None, DK, DV_GLA), lambda b, h: (b, layer_idx, 0, h, 0, 0)))
        args.append(init)
    out_specs = [pl.BlockSpec((n_rows, 256), lambda b, h: (b, h))]
    out_shape = [jax.ShapeDtypeStruct((n_seq * seq_len, D), BF16)]
    aliases = {}
    if want_final:
        lay, at = (DEPTH // 2, 0) if carried is None else (None, layer_idx)
        out_specs.append(pl.BlockSpec((n_seqs, lay, 2, None, DK, DV_GLA), lambda b, h: (b, at, 0, h, 0, 0)))
        out_shape.append(jax.ShapeDtypeStruct((n_seq, DEPTH // 2, 2, HEADS, DK, DV_GLA), F32))
        if carried is not None:
            aliases = {len(args): 1}
            in_specs.append(pl.BlockSpec(memory_space=pl.ANY))
            args.append(carried)
    scratch = [pltpu.VMEM((n_rows, 256), BF16)] * 2 + [pltpu.VMEM((n_rows, 256), F32),
                                                       pltpu.VMEM((total, DV_GLA, 2 * DK), BF16),
                                                       pltpu.VMEM((2 * n_seqs, DV_GLA, DK), F32)]
    return pl.pallas_call(
        functools.partial(_gla_kernel, seq_len, n_seqs, has_init, want_final, len(aliases), layer_idx),
        grid=(n_seq // n_seqs, HEADS),
        in_specs=in_specs,
        out_specs=out_specs,
        out_shape=out_shape,
        input_output_aliases=aliases,
        scratch_shapes=scratch,
        compiler_params=_cparams("parallel", "parallel"),
        name="gla_mixer_ctx" if ctx else "gla_mixer_lat",
    )(*args)


def _tail_kernel(layer, tile, tile_off, n_tiles, n_carried, *refs):
    last = layer == DEPTH - 1
    it = iter(refs)
    hc_ref, hl_ref, hup_ref, hdn_ref, x_ref, xup_ref, xdn_ref = (next(it) for _ in range(7))
    wo_ref, win_ref, cw_ref, cb_ref, w_ref, mod_ref, g_mix_ref, g_pre_ref, g_post_ref = (next(it) for _ in range(9))
    if not last:
        g_next_ref, mod_next_ref = next(it), next(it)
    for _ in range(n_carried):
        next(it)
    x_out_ref = next(it)
    xn_out_ref = None if last else next(it)
    hbuf_ref = next(it)
    i = pl.program_id(0) + tile_off
    n_ctx = N_CTX_TOK // tile
    tiles_per_seq = LAT_LEN // tile
    n_chunks = D_FF // FF_CHUNK

    def cols(f):
        return slice(f * FF_CHUNK, (f + 1) * FF_CHUNK), slice(D_FF + f * FF_CHUNK, D_FF + (f + 1) * FF_CHUNK)

    def act(conv, v, sl):
        return (jax.nn.gelu(conv + cb_ref[:, sl]) * v).astype(BF16)

    def mixer_out(h, x, keep=None):
        m = mod_ref[...]
        x_mid = x + m[2:3] * (_rms(_dot(h, wo_ref[...])) * g_mix_ref[...])
        xn = _rms(x_mid) * g_pre_ref[...] * (1.0 + m[4:5]) + m[3:4]
        if keep is not None:
            xn = jnp.where(keep, xn, 0.0)
        return x_mid, xn.astype(BF16)

    def finish(x_mid):
        x_new = x_mid + mod_ref[5:6, :] * (_rms(_dot(hbuf_ref[...], w_ref[...])) * g_post_ref[...])
        x_out_ref[...] = x_new
        if not last:
            mn = mod_next_ref[...]
            xn_out_ref[...] = (_rms(x_new) * g_next_ref[...] * (1.0 + mn[1:2]) + mn[0:1]).astype(BF16)

    def ctx_branch():
        x_mid, xn = mixer_out(hc_ref[...], x_ref[...])
        pos = lax.broadcasted_iota(jnp.int32, (tile, 1), 0) % CTX_LEN
        for f in range(n_chunks):
            sl, sv = cols(f)
            a = _dot(xn, win_ref[:, sl])
            left = jnp.where(pos >= 1, pltpu.roll(a, 1, 0), 0.0)
            right = jnp.where(pos < CTX_LEN - 1, pltpu.roll(a, tile - 1, 0), 0.0)
            conv = cw_ref[3:4, sl] * left + cw_ref[4:5, sl] * a + cw_ref[5:6, sl] * right
            hbuf_ref[:, sl] = act(conv, _dot(xn, win_ref[:, sv]), sl)
        finish(x_mid)

    def lat_branch():
        t = (i - n_ctx) % tiles_per_seq
        n_rows = tile + 2 * GRID_W
        rowpos = lax.broadcasted_iota(jnp.int32, (n_rows, 1), 0)
        first = jnp.where(t > 0, 0, GRID_W)
        stop = jnp.where(t < tiles_per_seq - 1, n_rows, GRID_W + tile)
        x_mid, xe = mixer_out(jnp.concatenate([hup_ref[...], hl_ref[...], hdn_ref[...]], axis=0),
                              jnp.concatenate([xup_ref[...], x_ref[...], xdn_ref[...]], axis=0),
                              (rowpos >= first) & (rowpos < stop))
        x_mid = x_mid[GRID_W:GRID_W + tile]
        xn = xe[GRID_W:GRID_W + tile]
        colpos = rowpos % GRID_W
        for f in range(n_chunks):
            sl, sv = cols(f)
            buf = _dot(xe, win_ref[:, sl])
            left = jnp.where(colpos >= 1, pltpu.roll(buf, 1, 0), 0.0)
            right = jnp.where(colpos < GRID_W - 1, pltpu.roll(buf, n_rows - 1, 0), 0.0)
            conv = None
            for dr in range(3):
                rs = slice(dr * GRID_W, dr * GRID_W + tile)
                term = (cw_ref[3 * dr:3 * dr + 1, sl] * left[rs] + cw_ref[3 * dr + 1:3 * dr + 2, sl] * buf[rs]
                        + cw_ref[3 * dr + 2:3 * dr + 3, sl] * right[rs])
                conv = term if conv is None else conv + term
            hbuf_ref[:, sl] = act(conv, _dot(xn, win_ref[:, sv]), sl)
        finish(x_mid)

    if tile_off + n_tiles <= n_ctx:
        ctx_branch()
    elif tile_off >= n_ctx:
        lat_branch()
    else:
        pl.when(i < n_ctx)(ctx_branch)
        pl.when(i >= n_ctx)(lat_branch)


def _layer_tail(h_ctx, h_lat, w_o, w_in, conv_w, conv_b, w_out, x, mods, g_mix_post, g_ffn_pre, g_post, g_next,
                layer, group, x_dst=None, xn_dst=None):
    last = layer == DEPTH - 1
    tile = TAIL_TILE_CTX if group == 0 else TAIL_TILE_LAT
    tile_off = 0 if group == 0 else N_CTX_TOK // tile
    n_tiles = (N_CTX_TOK if group == 0 else N_TOK - N_CTX_TOK) // tile
    halo_per_tile = tile // GRID_W
    n_halo = N_TOK // GRID_W
    n_ctx = N_CTX_TOK // tile
    n_lat_halo = (N_TOK - N_CTX_TOK) // GRID_W
    row = pl.BlockSpec((tile, D), lambda i: (i + tile_off, 0))
    resident = dict(pipeline_mode=pl.Buffered(1))
    layer_vec = pl.BlockSpec((None, 1, D), lambda i: (layer, 0, 0))

    def lat_halo(i, shift):
        return jnp.clip((i + tile_off - n_ctx + shift) * halo_per_tile - (1 - shift), 0, n_lat_halo - 1)

    h_ctx_spec = (pl.BlockSpec((tile, D), lambda i: (i, 0)) if group == 0
                  else pl.BlockSpec((GRID_W, D), lambda i: (0, 0)))
    h_lat_spec = (pl.BlockSpec((tile, D), lambda i: (i, 0)) if group == 1
                  else pl.BlockSpec((GRID_W, D), lambda i: (0, 0)))
    in_specs = [h_ctx_spec, h_lat_spec,
                pl.BlockSpec((GRID_W, D), lambda i: (lat_halo(i, 0), 0)),
                pl.BlockSpec((GRID_W, D), lambda i: (lat_halo(i, 1), 0)),
                row,
                pl.BlockSpec((GRID_W, D), lambda i: (jnp.maximum((i + tile_off) * halo_per_tile - 1, 0), 0)),
                pl.BlockSpec((GRID_W, D), lambda i: (jnp.minimum((i + tile_off + 1) * halo_per_tile, n_halo - 1), 0)),
                pl.BlockSpec((D, D), lambda i: (0, 0), **resident),
                pl.BlockSpec((None, D, 2 * D_FF), lambda i: (layer, 0, 0), **resident),
                pl.BlockSpec((None, 9, D_FF), lambda i: (layer, 0, 0)),
                pl.BlockSpec((None, 1, D_FF), lambda i: (layer, 0, 0)),
                pl.BlockSpec((None, D_FF, D), lambda i: (layer, 0, 0), **resident),
                pl.BlockSpec((None, None, 6, D), lambda i: (layer, _mod_row(i + tile_off, tile), 0, 0)),
                layer_vec, layer_vec, layer_vec]
    args = [h_ctx, h_lat, h_lat, h_lat, x, x, x, w_o, w_in, conv_w.reshape(DEPTH, 9, D_FF), conv_b, w_out, mods,
            g_mix_post, g_ffn_pre, g_post]
    aliases = {}
    if last:
        out_specs = [pl.BlockSpec((tile, D), lambda i: (i, 0))]
        out_shape = [jax.ShapeDtypeStruct((n_tiles * tile, D), F32)]
    else:
        in_specs += [pl.BlockSpec((None, 1, D), lambda i: (layer + 1, 0, 0)),
                     pl.BlockSpec((None, None, 6, D), lambda i: (layer + 1, _mod_row(i + tile_off, tile), 0, 0))]
        args += [g_next, mods]
        aliases = {len(args): 0, len(args) + 1: 1}
        in_specs += [pl.BlockSpec(memory_space=pl.ANY)] * 2
        args += [x_dst, xn_dst]
        out_specs = [row, row]
        out_shape = [jax.ShapeDtypeStruct((N_TOK, D), F32), jax.ShapeDtypeStruct((N_TOK, D), BF16)]
    return pl.pallas_call(
        functools.partial(_tail_kernel, layer, tile, tile_off, n_tiles, len(aliases)),
        grid=(n_tiles,),
        in_specs=in_specs,
        out_specs=out_specs,
        out_shape=out_shape,
        input_output_aliases=aliases,
        scratch_shapes=[pltpu.VMEM((tile, D_FF), BF16)],
        compiler_params=_cparams("parallel"),
        name="layer_tail",
    )(*args)


def _even_weights(w_gate_cols, w_out):
    src = [0] * (8 * HEADS)
    used = [0.0] * (8 * HEADS)
    for h in range(HEADS):
        for k, col in enumerate((h, 8 + h, 4 + h, 12 + h)):
            src[8 * h + k] = col
            used[8 * h + k] = 1.0
    w_gate = (w_gate_cols[:, jnp.array(src)] * jnp.array(used, F32)).T.astype(BF16)
    w_o = w_out.reshape(2, HEADS, 128, D).transpose(1, 0, 2, 3).reshape(D, D).astype(BF16)
    return w_gate, w_o


def _odd_weights(w_lr_cols, w2, b2):
    w_lr = jnp.tile(w_lr_cols, (1, 128 // (2 * GLA_RANK))).astype(BF16)
    w2h = w2.reshape(2, GLA_RANK, HEADS, DK).transpose(2, 0, 1, 3)
    w32 = jnp.zeros((HEADS, 2 * GLA_RANK, 256), F32)
    w32 = w32.at[:, :GLA_RANK, :DK].set(w2h[:, 0]).at[:, GLA_RANK:, DK:].set(w2h[:, 1])
    hi = w32.astype(BF16)
    mid = (w32 - hi.astype(F32)).astype(BF16)
    low = (w32 - hi.astype(F32) - mid.astype(F32)).astype(BF16)
    none = jnp.zeros_like(hi)
    wz = jnp.concatenate([hi, hi, hi, mid, mid, low, none, none], axis=1)
    bz = b2.reshape(2, HEADS, DK).transpose(1, 0, 2).reshape(HEADS, 1, 2 * DK)
    return w_lr, wz, bz


def kernel(x_prompt, x_sample, c, c_ctx, state_mlstm_C, state_mlstm_n, state_mlstm_m, state_ret, state_gla, ada_w, ada_b, norm_mix_pre, norm_mix_post, norm_ffn_pre, norm_ffn_post, w_in_even, w_out_even, mlstm_igate_b, mlstm_fgate_b, ret_decay_logit, w_in_odd, gla_gate_w2, gla_gate_b, w_out_odd, ffn_w_in, ffn_conv_w, ffn_conv_b, ffn_w_out):
    cond = jnp.concatenate([c_ctx[None, :], c, jnp.zeros((8 - 1 - N_LAT_SEQ, D), F32)], axis=0)
    mods = _mod_table(cond, ada_w, ada_b)
    norm_mix_pre, norm_mix_post, norm_ffn_pre, norm_ffn_post = (
        g.reshape(DEPTH, 1, D) for g in (norm_mix_pre, norm_mix_post, norm_ffn_pre, norm_ffn_post))
    ffn_conv_b = ffn_conv_b.reshape(DEPTH, 1, D_FF)
    ffn_w_in_bf, ffn_w_out_bf = ffn_w_in.astype(BF16), ffn_w_out.astype(BF16)
    w_even_bf = jnp.concatenate([w_in_even[:, :, :2048], w_in_even[:, :, 2064:]], axis=2).astype(BF16)
    w_odd_bf = w_in_odd.astype(BF16)
    gate_cols = w_in_even[:, :, 2048:2064]
    lr_cols = w_in_odd[:, :, 3072:3072 + 2 * GLA_RANK]

    x, xn = _prenorm(x_prompt.reshape(N_CTX_TOK, D), x_sample.reshape(N_LAT_SEQ * LAT_LEN, D),
                     norm_mix_pre, mods, 0)
    x_spare = jnp.zeros((N_TOK, D), F32)
    even_finals, gla_finals = None, None
    for layer in range(DEPTH):
        idx = layer // 2
        if layer % 2 == 0:
            w_gate, w_o = _even_weights(gate_cols[idx], w_out_even[idx])
            proj, gates = _proj(xn, w_even_bf, idx, 4096, w_gate, True, "proj_even")
            params = jnp.concatenate([mlstm_igate_b[idx].reshape(-1), mlstm_fgate_b[idx].reshape(-1),
                                      ret_decay_logit[idx].reshape(-1)])
            h_ctx, *even_finals = _even_mixer(proj, gates, params, 0, None, idx, even_finals)
            init = (state_mlstm_C, state_mlstm_n, state_mlstm_m, state_ret)
            (h_lat,) = _even_mixer(proj, gates, params, 1, init, idx)
        else:
            w_lr, wz, bz = _odd_weights(lr_cols[idx], gla_gate_w2[idx], gla_gate_b[idx])
            w_o = w_out_odd[idx].astype(BF16)
            proj, lowrank = _proj(xn, w_odd_bf, idx, 3072, w_lr, False, "proj_odd")
            h_ctx, gla_finals = _gla_mixer(proj, lowrank, wz, bz, 0, None, idx, gla_finals)
            (h_lat,) = _gla_mixer(proj, lowrank, wz, bz, 1, state_gla, idx)
        tail = functools.partial(_layer_tail, h_ctx, h_lat, w_o, ffn_w_in_bf, ffn_conv_w, ffn_conv_b, ffn_w_out_bf,
                                 x, mods, norm_mix_post, norm_ffn_pre, norm_ffn_post, norm_mix_pre, layer)
        if layer < DEPTH - 1:
            x_new, xn_new = tail(0, x_spare, xn)
            x_new, xn = tail(1, x_new, xn_new)
            x, x_spare = x_new, x
        else:
            (y_ctx,) = tail(0)
            (y_lat,) = tail(1)

    y_prompt = y_ctx.reshape(N_CTX_SEQ, CTX_LEN, D)
    y_sample = y_lat.reshape(N_LAT_SEQ, LAT_LEN, D)
    new_c, new_n, new_m, new_r = even_finals
    return (y_prompt, y_sample, new_c, new_n[..., 0, :], new_m[:, :, :, 0:2, 0].transpose(0, 1, 3, 2), new_r, gla_finals)
```
